```python
import math
import jax
import jax.numpy as jnp
from jax import lax
import numpy as np


D_MODEL = 1024
BATCH = 4
SEQ = 8192
DEPTH = 2

MEM_LEN = 256
N_MIXERS = 4
GROUP_WIDTH = D_MODEL // N_MIXERS
DIL_HEADS = 4
DIL_HEAD_DIM = GROUP_WIDTH // DIL_HEADS
DIL_BRANCHES = ((128, 1), (512, 4), (2048, 16))
ATTN_BLOCK = 128
RET_HEADS = 4
RET_DK = GROUP_WIDTH // RET_HEADS
RET_DV = GROUP_WIDTH // RET_HEADS
RET_CHUNK = 128
GLA_HEADS = 4
GLA_DV = GROUP_WIDTH // GLA_HEADS
GLA_DK = GLA_DV // 2
GLA_GATE_RANK = 16
GLA_TAU = 16.0
GLA_CHUNK = 64
S5_CH = 16
S5_GROUPS = GROUP_WIDTH // S5_CH
S5_STATE = 64
S5_DT_MIN = 1e-3
S5_DT_MAX = 1e-1
MEM_HEADS = 4
MEM_HEAD_DIM = D_MODEL // MEM_HEADS
D_FF = (8 * D_MODEL + 767) // 768 * 256
DEEPNORM_ALPHA = (2 * DEPTH) ** 0.25
DEEPNORM_BETA = (8 * DEPTH) ** -0.25
LN_EPS = 1e-5
NEG_INF = -1e30

IN_SIZES = (
    GROUP_WIDTH, GROUP_WIDTH, GROUP_WIDTH,
    RET_HEADS * RET_DK, RET_HEADS * RET_DK, RET_HEADS * RET_DV,
    RET_HEADS * RET_DV,
    GLA_HEADS * GLA_DK, GLA_HEADS * GLA_DK, GLA_HEADS * GLA_DV,
    GLA_GATE_RANK, GLA_HEADS * GLA_DV,
    GROUP_WIDTH,
)
D_IN = sum(IN_SIZES)

kernel_name = 'hybrid_dilated_ret_gla_s5_block'


def layer_norm(x, g, b):
    xf = x.astype(jnp.float32)
    mu = xf.mean(-1, keepdims=True)
    var = jnp.square(xf - mu).mean(-1, keepdims=True)
    return ((xf - mu) * lax.rsqrt(var + LN_EPS) * g + b).astype(x.dtype)


def head_norm(o, g):
    mu = o.mean(-1, keepdims=True)
    var = jnp.square(o - mu).mean(-1, keepdims=True)
    return (o - mu) * lax.rsqrt(var + LN_EPS) * g.reshape(o.shape[-2], o.shape[-1])


def chunk_state_scan(chunk_decay, u):
    def step(r, inp):
        a, uc = inp
        return a[..., None] * r + uc, r
    r0 = jnp.zeros(u.shape[:1] + u.shape[2:], u.dtype)
    _, prev = lax.scan(step, r0, (jnp.moveaxis(chunk_decay, 1, 0), jnp.moveaxis(u, 1, 0)))
    return jnp.moveaxis(prev, 0, 1)


def dilated_branch(q, k, v, window, dilation, slopes):
    bsz, seq, heads, hd = q.shape
    band = window // dilation
    n_prev = -(-band // ATTN_BLOCK)
    span = dilation * ATTN_BLOCK
    seq_pad = -(-seq // span) * span
    nb = seq_pad // span
    kw_len = (n_prev + 1) * ATTN_BLOCK

    def to_sub(t):
        t = jnp.pad(t, ((0, 0), (0, seq_pad - seq), (0, 0), (0, 0)))
        return t.reshape(bsz, nb, ATTN_BLOCK, dilation, heads, hd)

    def key_window(t):
        t = jnp.pad(t, ((0, 0), (n_prev, 0), (0, 0), (0, 0), (0, 0), (0, 0)))
        return jnp.concatenate([t[:, i:i + nb] for i in range(n_prev + 1)], axis=2)

    qs = to_sub(q)
    ks = key_window(to_sub(k))
    vs = key_window(to_sub(v))
    s = jnp.einsum('bnqrhd,bnkrhd->bnrhqk', qs, ks).astype(jnp.float32) * (hd ** -0.5)
    qi = jnp.arange(ATTN_BLOCK)[:, None]
    ki = jnp.arange(kw_len)[None, :]
    dist = qi + n_prev * ATTN_BLOCK - ki
    key_pos = jnp.arange(nb)[:, None] * ATTN_BLOCK - n_prev * ATTN_BLOCK + jnp.arange(kw_len)[None, :]
    valid = ((dist >= 0) & (dist <= band))[None] & (key_pos >= 0)[:, None, :]
    alibi = -slopes[:, None, None] * (dist * dilation).astype(jnp.float32)[None]
    s = jnp.where(valid[None, :, None, None], s + alibi, NEG_INF)
    m = s.max(-1, keepdims=True)
    p = jnp.exp(s - m)
    l = p.sum(-1)
    o = jnp.einsum('bnrhqk,bnkrhd->bnqrhd', p.astype(v.dtype), vs)
    inv = jnp.transpose(1.0 / l, (0, 1, 4, 2, 3))
    o = o * inv[..., None].astype(o.dtype)
    lse = jnp.transpose(m[..., 0] + jnp.log(l), (0, 1, 4, 2, 3))
    o = o.reshape(bsz, seq_pad, heads, hd)[:, :seq]
    lse = lse.reshape(bsz, seq_pad, heads)[:, :seq]
    return o, lse


def dilated_attention(q, k, v):
    bsz, seq, heads, hd = q.shape
    slopes = 2.0 ** (-8.0 * jnp.arange(1, heads + 1, dtype=jnp.float32) / heads)
    results = [dilated_branch(q, k, v, w, r, slopes) for (w, r) in DIL_BRANCHES]
    outs = jnp.stack([o for o, _ in results]).astype(jnp.float32)
    lses = jnp.stack([z for _, z in results])
    wts = jax.nn.softmax(lses, axis=0)
    o = jnp.sum(wts[..., None] * outs, axis=0)
    return o.reshape(bsz, seq, heads * hd).astype(q.dtype)


def retention(q, k, v, gate, gn_g):
    bsz, seq, heads, dk = q.shape
    dv = v.shape[-1]
    c = RET_CHUNK
    nc = seq // c
    log_g = jnp.log(1.0 - 2.0 ** (-5.0 - jnp.arange(heads, dtype=jnp.float32)))
    qc = q.reshape(bsz, nc, c, heads, dk)
    kc = k.reshape(bsz, nc, c, heads, dk) * (dk ** -0.5)
    vc = v.reshape(bsz, nc, c, heads, dv)
    pos = jnp.arange(c, dtype=jnp.float32)
    rel = pos[:, None] - pos[None, :]
    decay = jnp.where(rel[None] >= 0, jnp.exp(jnp.maximum(rel, 0.0)[None] * log_g[:, None, None]), 0.0)
    scores = jnp.einsum('bcnhd,bcmhd->bchnm', qc, kc) * decay
    inner = jnp.einsum('bchnm,bcmhe->bcnhe', scores, vc)
    k_dec = kc * jnp.exp((c - 1 - pos)[:, None] * log_g[None, :])[..., None]
    u = jnp.einsum('bcmhd,bcmhe->bchde', k_dec, vc)
    chunk_decay = jnp.broadcast_to(jnp.exp(c * log_g)[:, None], (bsz, nc, heads, dk))
    r_prev = chunk_state_scan(chunk_decay, u)
    q_dec = qc * jnp.exp((pos + 1.0)[:, None] * log_g[None, :])[..., None]
    cross = jnp.einsum('bcnhd,bchde->bcnhe', q_dec, r_prev)
    o = (inner + cross).astype(jnp.float32).reshape(bsz, seq, heads, dv)
    o = head_norm(o, gn_g) * jax.nn.silu(gate.astype(jnp.float32)).reshape(bsz, seq, heads, dv)
    return o.reshape(bsz, seq, heads * dv).astype(q.dtype)


def gla(q, k, v, g_low, w_gate, b_gate, out_gate, gn_g):
    bsz, seq, heads, dk = q.shape
    dv = v.shape[-1]
    c = GLA_CHUNK
    nc = seq // c
    z = (g_low @ w_gate + b_gate).astype(jnp.float32)
    log_a = jax.nn.log_sigmoid(z).reshape(bsz, nc, c, heads, dk) / GLA_TAU
    b = jnp.cumsum(log_a, axis=2)
    qc = q.astype(jnp.float32).reshape(bsz, nc, c, heads, dk) * (dk ** -0.5)
    kc = k.astype(jnp.float32).reshape(bsz, nc, c, heads, dk)
    vc = v.astype(jnp.float32).reshape(bsz, nc, c, heads, dv)
    q_in = qc * jnp.exp(b)
    att = jnp.einsum('bcnhd,bcmhd->bchnm', q_in, kc * jnp.exp(-b))
    causal = jnp.tril(jnp.ones((c, c), dtype=bool))
    intra = jnp.einsum('bchnm,bcmhe->bcnhe', jnp.where(causal, att, 0.0), vc)
    b_last = b[:, :, -1:]
    u = jnp.einsum('bcmhd,bcmhe->bchde', kc * jnp.exp(b_last - b), vc)
    r_prev = chunk_state_scan(jnp.exp(b_last[:, :, 0]), u)
    cross = jnp.einsum('bcnhd,bchde->bcnhe', q_in, r_prev)
    o = (intra + cross).reshape(bsz, seq, heads, dv)
    o = head_norm(o, gn_g) * jax.nn.silu(out_gate.astype(jnp.float32)).reshape(bsz, seq, heads, dv)
    return o.reshape(bsz, seq, heads * dv).astype(q.dtype)


def s5_layer(u, a_re, a_im, log_dt, b_re, b_im, c_re, c_im, d, w_glu, b_glu):
    bsz, seq, _ = u.shape
    ug = u.astype(jnp.float32).reshape(bsz, seq, S5_GROUPS, S5_CH)
    dt = jnp.exp(log_dt.astype(jnp.float32))[:, None]
    ea = jnp.exp(a_re * dt)
    ab_re = ea * jnp.cos(a_im * dt)
    ab_im = ea * jnp.sin(a_im * dt)
    den = a_re * a_re + a_im * a_im
    nr = ab_re - 1.0
    f_re = (nr * a_re + ab_im * a_im) / den
    f_im = (ab_im * a_re - nr * a_im) / den
    bb_re = f_re[..., None] * b_re - f_im[..., None] * b_im
    bb_im = f_re[..., None] * b_im + f_im[..., None] * b_re
    bu_re = jnp.einsum('bsgi,gpi->bsgp', ug, bb_re)
    bu_im = jnp.einsum('bsgi,gpi->bsgp', ug, bb_im)
    at_re = jnp.broadcast_to(ab_re, bu_re.shape)
    at_im = jnp.broadcast_to(ab_im, bu_im.shape)

    def combine(e1, e2):
        a1r, a1i, b1r, b1i = e1
        a2r, a2i, b2r, b2i = e2
        return (a2r * a1r - a2i * a1i,
                a2r * a1i + a2i * a1r,
                a2r * b1r - a2i * b1i + b2r,
                a2r * b1i + a2i * b1r + b2i)

    _, _, x_re, x_im = lax.associative_scan(combine, (at_re, at_im, bu_re, bu_im), axis=1)
    y = (jnp.einsum('bsgp,gip->bsgi', x_re, c_re) - jnp.einsum('bsgp,gip->bsgi', x_im, c_im)
         + d * ug)
    y = jax.nn.gelu(y).reshape(bsz, seq, S5_GROUPS * S5_CH)
    out = y * jax.nn.sigmoid(y @ w_glu + b_glu)
    return out.astype(u.dtype)


def hybrid_mixer(x, w_in, ret_gn_g, gla_w_gate, gla_b_gate, gla_gn_g, s5_a_re, s5_a_im,
                 s5_log_dt, s5_b_re, s5_b_im, s5_c_re, s5_c_im, s5_d, s5_w_glu, s5_b_glu,
                 w_mix_out):
    bsz, seq, _ = x.shape
    proj = x @ w_in
    (dq, dk_, dv_, rq, rk, rv, rg, gq, gk, gv, glow, gog, su) = jnp.split(
        proj, np.cumsum(IN_SIZES)[:-1].tolist(), axis=-1)

    def heads(t, h):
        return t.reshape(bsz, seq, h, -1)

    y_a = dilated_attention(heads(dq, DIL_HEADS), heads(dk_, DIL_HEADS), heads(dv_, DIL_HEADS))
    y_b = retention(heads(rq, RET_HEADS), heads(rk, RET_HEADS), heads(rv, RET_HEADS), rg, ret_gn_g)
    y_c = gla(heads(gq, GLA_HEADS), heads(gk, GLA_HEADS), heads(gv, GLA_HEADS), glow,
              gla_w_gate, gla_b_gate, gog, gla_gn_g)
    y_d = s5_layer(su, s5_a_re, s5_a_im, s5_log_dt, s5_b_re, s5_b_im, s5_c_re, s5_c_im,
                   s5_d, s5_w_glu, s5_b_glu)
    y = jnp.concatenate([y_a, y_b, y_c, y_d], axis=-1)
    return y @ w_mix_out


def memory_cross_attention(x, mem, w_q, w_kv, w_o):
    bsz, seq, _ = x.shape
    q = (x @ w_q).reshape(bsz, seq, MEM_HEADS, MEM_HEAD_DIM)
    k, v = jnp.split(mem @ w_kv, 2, axis=-1)
    k = k.reshape(bsz, -1, MEM_HEADS, MEM_HEAD_DIM)
    v = v.reshape(bsz, -1, MEM_HEADS, MEM_HEAD_DIM)
    s = jnp.einsum('bshd,bmhd->bhsm', q, k).astype(jnp.float32) * (MEM_HEAD_DIM ** -0.5)
    p = jax.nn.softmax(s, axis=-1)
    o = jnp.einsum('bhsm,bmhd->bshd', p.astype(v.dtype), v).reshape(bsz, seq, D_MODEL)
    return o @ w_o


def swiglu_ffn(x, w_gate, w_up, w_down):
    return (jax.nn.silu(x @ w_gate) * (x @ w_up)) @ w_down


def setup_inputs(seed: int = 0) -> dict:
    key = jax.random.key(seed)
    ks = jax.random.split(key, 32)
    L, D, W = DEPTH, D_MODEL, GROUP_WIDTH
    G, P, CH = S5_GROUPS, S5_STATE, S5_CH
    f32 = jnp.float32

    def nrm(k, shape, scale):
        return scale * jax.random.normal(k, shape, f32)

    return {
        'x': nrm(ks[0], (BATCH, SEQ, D), 1.0),
        'mem': nrm(ks[1], (BATCH, MEM_LEN, D), 1.0),
        'w_in': nrm(ks[2], (L, D, D_IN), D ** -0.5),
        'ret_gn_g': 1.0 + nrm(ks[3], (L, RET_HEADS * RET_DV), 0.02),
        'gla_w_gate': nrm(ks[4], (L, GLA_GATE_RANK, GLA_HEADS * GLA_DK), GLA_GATE_RANK ** -0.5),
        'gla_b_gate': nrm(ks[5], (L, GLA_HEADS * GLA_DK), 0.1),
        'gla_gn_g': 1.0 + nrm(ks[6], (L, GLA_HEADS * GLA_DV), 0.02),
        's5_a_re': -0.5 + nrm(ks[7], (L, G, P), 0.01),
        's5_a_im': math.pi * jnp.arange(P, dtype=f32) + nrm(ks[8], (L, G, P), 0.01),
        's5_log_dt': jax.random.uniform(ks[9], (L, G), f32, math.log(S5_DT_MIN), math.log(S5_DT_MAX)),
        's5_b_re': nrm(ks[10], (L, G, P, CH), (2 * CH) ** -0.5),
        's5_b_im': nrm(ks[11], (L, G, P, CH), (2 * CH) ** -0.5),
        's5_c_re': nrm(ks[12], (L, G, CH, P), P ** -0.5),
        's5_c_im': nrm(ks[13], (L, G, CH, P), P ** -0.5),
        's5_d': nrm(ks[14], (L, G, CH), 1.0),
        's5_w_glu': nrm(ks[15], (L, W, W), W ** -0.5),
        's5_b_glu': nrm(ks[16], (L, W), 0.02),
        'w_mix_out': nrm(ks[17], (L, D, D), D ** -0.5 * DEEPNORM_BETA),
        'ln_mix_g': 1.0 + nrm(ks[18], (L, D), 0.02),
        'ln_mix_b': nrm(ks[19], (L, D), 0.02),
        'w_mem_q': nrm(ks[20], (L, D, D), D ** -0.5),
        'w_mem_kv': nrm(ks[21], (L, D, 2 * D), D ** -0.5),
        'w_mem_o': nrm(ks[22], (L, D, D), D ** -0.5 * DEEPNORM_BETA),
        'ln_mem_g': 1.0 + nrm(ks[23], (L, D), 0.02),
        'ln_mem_b': nrm(ks[24], (L, D), 0.02),
        'w_ff_gate': nrm(ks[25], (L, D, D_FF), D ** -0.5),
        'w_ff_up': nrm(ks[26], (L, D, D_FF), D ** -0.5),
        'w_ff_down': nrm(ks[27], (L, D_FF, D), D_FF ** -0.5 * DEEPNORM_BETA),
        'ln_ff_g': 1.0 + nrm(ks[28], (L, D), 0.02),
        'ln_ff_b': nrm(ks[29], (L, D), 0.02),
    }


def reference(x, mem, w_in, ret_gn_g, gla_w_gate, gla_b_gate, gla_gn_g, s5_a_re, s5_a_im,
              s5_log_dt, s5_b_re, s5_b_im, s5_c_re, s5_c_im, s5_d, s5_w_glu, s5_b_glu,
              w_mix_out, ln_mix_g, ln_mix_b, w_mem_q, w_mem_kv, w_mem_o, ln_mem_g, ln_mem_b,
              w_ff_gate, w_ff_up, w_ff_down, ln_ff_g, ln_ff_b):
    for l in range(DEPTH):
        h = hybrid_mixer(x, w_in[l], ret_gn_g[l], gla_w_gate[l], gla_b_gate[l], gla_gn_g[l],
                         s5_a_re[l], s5_a_im[l], s5_log_dt[l], s5_b_re[l], s5_b_im[l],
                         s5_c_re[l], s5_c_im[l], s5_d[l], s5_w_glu[l], s5_b_glu[l], w_mix_out[l])
        x = layer_norm(DEEPNORM_ALPHA * x + h, ln_mix_g[l], ln_mix_b[l])
        h = memory_cross_attention(x, mem, w_mem_q[l], w_mem_kv[l], w_mem_o[l])
        x = layer_norm(DEEPNORM_ALPHA * x + h, ln_mem_g[l], ln_mem_b[l])
        h = swiglu_ffn(x, w_ff_gate[l], w_ff_up[l], w_ff_down[l])
        x = layer_norm(DEEPNORM_ALPHA * x + h, ln_ff_g[l], ln_ff_b[l])
    return x
```

```python
import functools
import math

import jax
import jax.numpy as jnp
from jax import lax
from jax.experimental import pallas as pl
from jax.experimental.pallas import tpu as pltpu

F32 = jnp.float32
BF16 = jnp.bfloat16
HIGHEST = lax.Precision.HIGHEST

D_MODEL = 1024
DEPTH = 2
GROUP_WIDTH = 256
HEADS = 4
HEAD_DIM = 64
DIL_BRANCHES = ((128, 1), (512, 4), (2048, 16))
ATTN_BLOCK = 128
RET_CHUNK = 128
GLA_DK = 32
GLA_GATE_RANK = 16
GLA_TAU = 16.0
GLA_CHUNK = 64
S5_CH = 16
S5_GROUPS = 16
S5_STATE = 64
S5_T = 8
MEM_LEN = 256
MEM_HEADS = 4
MEM_HEAD_DIM = 256
D_FF = 2816
DEEPNORM_ALPHA = (2 * DEPTH) ** 0.25
LN_EPS = 1e-5
NEG_INF = -1e30

PB_COLS = 2048
PF_COLS = 640
W_IN_COLS = PB_COLS + GROUP_WIDTH + PF_COLS

VMEM_LIMIT = 56 * 1024 * 1024


def _cparams(n_axes):
    return pltpu.CompilerParams(dimension_semantics=("arbitrary",) * n_axes,
                                vmem_limit_bytes=VMEM_LIMIT)


def _dot(a, b):
    return jnp.dot(a, b, preferred_element_type=F32)


def _dot_nt(a, b):
    return lax.dot_general(a, b, (((1,), (1,)), ((), ())), preferred_element_type=F32)


def _dot_tn(a, b):
    return lax.dot_general(a, b, (((0,), (0,)), ((), ())), preferred_element_type=F32)


def _layer_norm(z, g, b):
    mu = jnp.mean(z, axis=-1, keepdims=True)
    d = z - mu
    var = jnp.mean(d * d, axis=-1, keepdims=True)
    return d * lax.rsqrt(var + LN_EPS) * g + b


def _sigmoid(x):
    return 1.0 / (1.0 + jnp.exp(-x))


def _head_of_lane(shape, head_width):
    return lax.broadcasted_iota(jnp.int32, shape, len(shape) - 1) // head_width


def _stack_heads(t, head_width):
    lane_head = _head_of_lane(t.shape, head_width)
    zero = jnp.zeros_like(t)
    return jnp.concatenate([jnp.where(lane_head == h, t, zero) for h in range(HEADS)], axis=0)


def _per_head_rows(cols, lane_head):
    out = jnp.broadcast_to(cols[HEADS - 1], lane_head.shape)
    for h in range(HEADS - 2, -1, -1):
        out = jnp.where(lane_head == h, cols[h], out)
    return out


def _head_norm(o, avg):
    mu = jnp.dot(o, avg, precision=HIGHEST, preferred_element_type=F32)
    d = o - mu
    var = jnp.dot(d * d, avg, precision=HIGHEST, preferred_element_type=F32)
    return d * lax.rsqrt(var + LN_EPS)


def _head_avg_matrix(width, head_width):
    r = lax.broadcasted_iota(jnp.int32, (width, width), 0) // head_width
    c = lax.broadcasted_iota(jnp.int32, (width, width), 1) // head_width
    return jnp.where(r == c, 1.0 / head_width, 0.0).astype(F32)


def _inproj_kernel(x_ref, w_ref, pb_ref, su_ref, pf_ref):
    xb = x_ref[...].astype(BF16)
    step = 512
    for j in range(0, PB_COLS, step):
        pb_ref[:, j:j + step] = _dot(xb, w_ref[:, j:j + step]).astype(BF16)
    su_ref[...] = _dot(xb, w_ref[:, PB_COLS:PB_COLS + GROUP_WIDTH])
    pf_ref[...] = _dot(xb, w_ref[:, PB_COLS + GROUP_WIDTH:])


def _inproj(x2, w_p, tm=512):
    n = x2.shape[0]
    return pl.pallas_call(
        _inproj_kernel,
        grid=(n // tm,),
        in_specs=[pl.BlockSpec((tm, D_MODEL), lambda i: (i, 0)),
                  pl.BlockSpec((D_MODEL, W_IN_COLS), lambda i: (0, 0))],
        out_specs=[pl.BlockSpec((tm, PB_COLS), lambda i: (i, 0)),
                   pl.BlockSpec((tm, GROUP_WIDTH), lambda i: (i, 0)),
                   pl.BlockSpec((tm, PF_COLS), lambda i: (i, 0))],
        out_shape=[jax.ShapeDtypeStruct((n, PB_COLS), BF16),
                   jax.ShapeDtypeStruct((n, GROUP_WIDTH), F32),
                   jax.ShapeDtypeStruct((n, PF_COLS), F32)],
        compiler_params=_cparams(1),
        name="inproj",
    )(x2, w_p)


def _dil_kernel(q_ref, kp_ref, kc_ref, vp_ref, vc_ref, o_ref, lse_ref, tb_ref, *, dilation):
    blk = ATTN_BLOCK
    kw = 2 * blk
    nb = pl.program_id(2)

    @pl.when((pl.program_id(0) == 0) & (pl.program_id(1) == 0) & (nb == 0))
    def _init_tables():
        shape = (blk, HEADS * kw)
        qi = lax.broadcasted_iota(jnp.int32, shape, 0)
        col = lax.broadcasted_iota(jnp.int32, shape, 1)
        ki = col % kw
        head = col // kw
        dist = qi + blk - ki
        valid = (dist >= 0) & (dist <= blk)
        slope = jnp.where(head == 0, 2.0 ** -2, jnp.where(head == 1, 2.0 ** -4,
                          jnp.where(head == 2, 2.0 ** -6, 2.0 ** -8))).astype(F32)
        alibi = -slope * (dist * dilation).astype(F32)
        tb_ref[0] = jnp.where(valid, alibi, NEG_INF)
        tb_ref[1] = jnp.where(valid & (ki >= blk), alibi, NEG_INF)

    tb = tb_ref[jnp.where(nb == 0, 1, 0)]
    q = q_ref[0]
    kblk = _stack_heads(jnp.concatenate([kp_ref[0], kc_ref[0]], axis=0), HEAD_DIM)
    vblk = _stack_heads(jnp.concatenate([vp_ref[0], vc_ref[0]], axis=0), HEAD_DIM)
    s = _dot_nt(q, kblk) * (HEAD_DIM ** -0.5)
    s = jnp.where(tb > 0.5 * NEG_INF, s + tb, NEG_INF)
    ps, inv_l, lse = [], [], []
    for h in range(HEADS):
        sh = s[:, h * kw:(h + 1) * kw]
        m = jnp.max(sh, axis=-1, keepdims=True)
        p = jnp.exp(sh - m)
        l = jnp.sum(p, axis=-1, keepdims=True)
        ps.append(p.astype(BF16))
        inv_l.append(1.0 / l)
        lse.append(m + jnp.log(l))
    o = _dot(jnp.concatenate(ps, axis=1), vblk)
    lane_head = _head_of_lane(o.shape, HEAD_DIM)
    o_ref[0] = o * _per_head_rows(inv_l, lane_head)
    lse_ref[0] = _per_head_rows(lse, lane_head)


def _dilated_branch(pb, bsz, seq, dilation):
    blk = ATTN_BLOCK
    rows = seq // dilation
    nblk = rows // blk
    cb = PB_COLS // GROUP_WIDTH
    pv = pb.reshape(bsz, rows, dilation * PB_COLS)
    bs = (1, blk, GROUP_WIDTH)

    def cur(col):
        return pl.BlockSpec(bs, lambda b, c, n: (b, n, c * cb + col))

    def prev(col):
        return pl.BlockSpec(bs, lambda b, c, n: (b, jnp.maximum(n - 1, 0), c * cb + col))

    out_spec = pl.BlockSpec(bs, lambda b, c, n: (b, n, c))
    out_sd = jax.ShapeDtypeStruct((bsz, rows, dilation * GROUP_WIDTH), F32)
    o, lse = pl.pallas_call(
        functools.partial(_dil_kernel, dilation=dilation),
        grid=(bsz, dilation, nblk),
        in_specs=[cur(0), prev(1), cur(1), prev(2), cur(2)],
        out_specs=[out_spec, out_spec],
        out_shape=[out_sd, out_sd],
        scratch_shapes=[pltpu.VMEM((2, blk, HEADS * 2 * blk), F32)],
        compiler_params=_cparams(3),
        name=f"dilated_r{dilation}",
    )(pv, pv, pv, pv, pv)
    n = bsz * seq
    return o.reshape(n, GROUP_WIDTH), lse.reshape(n, GROUP_WIDTH)


def _ret_kernel(q_ref, k_ref, v_ref, g_ref, gn_ref, o_ref, r_ref, dcat_ref, qd_ref, kd_ref, *, nchunk):
    c = RET_CHUNK
    width = GROUP_WIDTH
    log_g = [math.log(1.0 - 2.0 ** (-5.0 - h)) for h in range(HEADS)]

    def lane_log_g(shape):
        head = _head_of_lane(shape, HEAD_DIM)
        return jnp.where(head == 0, log_g[0], jnp.where(head == 1, log_g[1],
                         jnp.where(head == 2, log_g[2], log_g[3]))).astype(F32)

    @pl.when((pl.program_id(0) == 0) & (pl.program_id(1) == 0))
    def _init_tables():
        n = lax.broadcasted_iota(jnp.int32, (c, HEADS * c), 0)
        col = lax.broadcasted_iota(jnp.int32, (c, HEADS * c), 1)
        rel = (n - col % c).astype(F32)
        head = col // c
        lg = jnp.where(head == 0, log_g[0], jnp.where(head == 1, log_g[1],
                       jnp.where(head == 2, log_g[2], log_g[3]))).astype(F32)
        dcat_ref[...] = jnp.where(rel >= 0, jnp.exp(jnp.maximum(rel, 0.0) * lg), 0.0)
        pos = lax.broadcasted_iota(jnp.int32, (c, width), 0).astype(F32)
        lgl = lane_log_g((c, width))
        qd_ref[...] = jnp.exp((pos + 1.0) * lgl)
        kd_ref[...] = jnp.exp((c - 1.0 - pos) * lgl)

    @pl.when(pl.program_id(1) == 0)
    def _reset_state():
        r_ref[...] = jnp.zeros_like(r_ref)

    rr = lax.broadcasted_iota(jnp.int32, (width, width), 0) // HEAD_DIM
    cc = lax.broadcasted_iota(jnp.int32, (width, width), 1) // HEAD_DIM
    same_head = rr == cc
    avg = jnp.where(same_head, 1.0 / HEAD_DIM, 0.0).astype(F32)
    chunk_decay = jnp.exp(c * lane_log_g((1, width)))
    scale = HEAD_DIM ** -0.5

    for ci in range(nchunk):
        sl = slice(ci * c, (ci + 1) * c)
        q = q_ref[sl, :]
        k = k_ref[sl, :]
        v = v_ref[sl, :]
        kblk = _stack_heads(k, HEAD_DIM)
        vblk = _stack_heads(v, HEAD_DIM)
        scores = _dot_nt(q, kblk) * scale * dcat_ref[...]
        inner = _dot(scores.astype(BF16), vblk)
        r_prev = r_ref[...]
        q_dec = (q.astype(F32) * qd_ref[...]).astype(BF16)
        cross = _dot(q_dec, r_prev.astype(BF16))
        k_dec = (k.astype(F32) * scale * kd_ref[...]).astype(BF16)
        u = _dot_tn(k_dec, v)
        r_ref[...] = r_prev * chunk_decay + jnp.where(same_head, u, 0.0)
        o = _head_norm(inner + cross, avg) * gn_ref[...]
        g = g_ref[sl, :]
        o_ref[sl, :] = (o * (g * _sigmoid(g))).astype(BF16)


def _retention(pb, pf, gn, bsz, seq, tq=512):
    n = bsz * seq
    per_b = seq // tq
    w = GROUP_WIDTH

    def tok(col):
        return pl.BlockSpec((tq, w), lambda b, i: (b * per_b + i, col))

    return pl.pallas_call(
        functools.partial(_ret_kernel, nchunk=tq // RET_CHUNK),
        grid=(bsz, per_b),
        in_specs=[tok(3), tok(4), tok(5), tok(0), pl.BlockSpec((1, w), lambda b, i: (0, 0))],
        out_specs=tok(0),
        out_shape=jax.ShapeDtypeStruct((n, w), BF16),
        scratch_shapes=[pltpu.VMEM((w, w), F32),
                        pltpu.VMEM((RET_CHUNK, HEADS * RET_CHUNK), F32),
                        pltpu.VMEM((RET_CHUNK, w), F32),
                        pltpu.VMEM((RET_CHUNK, w), F32)],
        compiler_params=_cparams(2),
        name="retention",
    )(pb, pb, pb, pf, gn)


def _gla_kernel(q_ref, k_ref, v_ref, gl_ref, og_ref, wg_ref, bg_ref, gn_ref, o_ref, rt_ref, *, nchunk):
    c = GLA_CHUNK
    kwid = HEADS * GLA_DK
    vwid = GROUP_WIDTH

    @pl.when(pl.program_id(1) == 0)
    def _reset_state():
        rt_ref[...] = jnp.zeros_like(rt_ref)

    ti = lax.broadcasted_iota(jnp.int32, (c, c), 0)
    tj = lax.broadcasted_iota(jnp.int32, (c, c), 1)
    tri = jnp.where(ti >= tj, 1.0, 0.0).astype(F32)
    causal = lax.broadcasted_iota(jnp.int32, (c, HEADS * c), 0) >= (
        lax.broadcasted_iota(jnp.int32, (c, HEADS * c), 1) % c)
    rr = lax.broadcasted_iota(jnp.int32, (vwid, kwid), 0) // HEAD_DIM
    cc = lax.broadcasted_iota(jnp.int32, (vwid, kwid), 1) // GLA_DK
    same_head = rr == cc
    avg = _head_avg_matrix(vwid, HEAD_DIM)
    scale = GLA_DK ** -0.5

    for ci in range(nchunk):
        sl = slice(ci * c, (ci + 1) * c)
        z = _dot(gl_ref[sl, :].astype(BF16), wg_ref[...]) + bg_ref[...]
        log_a = (jnp.minimum(z, 0.0) - jnp.log(1.0 + jnp.exp(-jnp.abs(z)))) * (1.0 / GLA_TAU)
        bcum = jnp.dot(tri, log_a, precision=HIGHEST, preferred_element_type=F32)
        b_last = bcum[c - 1:c, :]
        q = q_ref[sl, :].astype(F32)
        k = k_ref[sl, :].astype(F32)
        v = v_ref[sl, :]
        q_in = (q * scale * jnp.exp(bcum)).astype(BF16)
        k_out = (k * jnp.exp(-bcum)).astype(BF16)
        k_st = (k * jnp.exp(b_last - bcum)).astype(BF16)
        att = _dot_nt(q_in, _stack_heads(k_out, GLA_DK))
        att = jnp.where(causal, att, 0.0)
        intra = _dot(att.astype(BF16), _stack_heads(v, HEAD_DIM))
        rt_prev = rt_ref[...]
        cross = _dot_nt(q_in, rt_prev.astype(BF16))
        ut = _dot_tn(v, k_st)
        rt_ref[...] = rt_prev * jnp.exp(b_last) + jnp.where(same_head, ut, 0.0)
        o = _head_norm(intra + cross, avg) * gn_ref[...]
        g = og_ref[sl, :]
        o_ref[sl, :] = (o * (g * _sigmoid(g))).astype(BF16)


def _gla(pb, pf, w_gate_p, b_gate, gn, bsz, seq, tq=512):
    n = bsz * seq
    per_b = seq // tq
    kwid = HEADS * GLA_DK

    def tok(width, col):
        return pl.BlockSpec((tq, width), lambda b, i: (b * per_b + i, col))

    def const(shape):
        return pl.BlockSpec(shape, lambda b, i: (0, 0))

    return pl.pallas_call(
        functools.partial(_gla_kernel, nchunk=tq // GLA_CHUNK),
        grid=(bsz, per_b),
        in_specs=[tok(kwid, 12), tok(kwid, 13), tok(GROUP_WIDTH, 7),
                  tok(kwid, 4), tok(GROUP_WIDTH, 1),
                  const((kwid, kwid)), const((1, kwid)), const((1, GROUP_WIDTH))],
        out_specs=tok(GROUP_WIDTH, 0),
        out_shape=jax.ShapeDtypeStruct((n, GROUP_WIDTH), BF16),
        scratch_shapes=[pltpu.VMEM((GROUP_WIDTH, kwid), F32)],
        compiler_params=_cparams(2),
        name="gla",
    )(pb, pb, pb, pf, pf, w_gate_p, b_gate, gn)


S5_ROW = S5_T * GROUP_WIDTH
S5_NSTATE = S5_GROUPS * S5_STATE
S5_SW = 2 * S5_NSTATE


def _s5prep_kernel(ar_ref, ai_ref, ld_ref, arc_ref, aic_ref, ldc_ref, bre_ref, bim_ref, cre_ref, cim_ref,
                   bst_ref, bigk_ref, ccr_ref, a8_ref):
    w = GROUP_WIDTH

    def discretise(ar, ai, ld):
        dt = jnp.exp(ld)
        ea = jnp.exp(ar * dt)
        return ea * jnp.cos(ai * dt), ea * jnp.sin(ai * dt)

    def powers(a_re, a_im, n):
        pw = [(jnp.ones_like(a_re), jnp.zeros_like(a_re)), (a_re, a_im)]
        for _ in range(2, n + 1):
            pr, pi = pw[-1]
            pw.append((pr * a_re - pi * a_im, pr * a_im + pi * a_re))
        return pw

    ar, ai = ar_ref[...], ai_ref[...]
    a_re, a_im = discretise(ar, ai, ld_ref[...])
    pw = powers(a_re, a_im, S5_T)
    den = ar * ar + ai * ai
    nr = a_re - 1.0
    f_re = (nr * ar + a_im * ai) / den
    f_im = (a_im * ar - nr * ai) / den
    b_re, b_im = bre_ref[...], bim_ref[...]
    bb_re = f_re * b_re - f_im * b_im
    bb_im = f_re * b_im + f_im * b_re
    c_re, c_im = cre_ref[...], cim_ref[...]
    cmat = jnp.concatenate([c_re, -c_im], axis=0)

    bigk_ref[...] = jnp.zeros_like(bigk_ref)
    for lag in range(S5_T):
        pr, pi = pw[lag]
        lmat = jnp.concatenate([pr * bb_re - pi * bb_im, pr * bb_im + pi * bb_re], axis=1)
        s = S5_T - 1 - lag
        bst_ref[s * w:(s + 1) * w, :] = lmat.astype(BF16)
        m = jnp.dot(lmat, cmat, precision=HIGHEST, preferred_element_type=F32).astype(BF16)
        for s in range(S5_T - lag):
            t = s + lag
            bigk_ref[s * w:(s + 1) * w, t * w:(t + 1) * w] = m

    ac_re, ac_im = discretise(arc_ref[...], aic_ref[...], ldc_ref[...])
    pwc = powers(ac_re, ac_im, S5_T)
    for t in range(S5_T):
        pr, pi = pwc[t + 1]
        ccr_ref[:S5_NSTATE, t * w:(t + 1) * w] = (c_re * pr - c_im * pi).astype(BF16)
        ccr_ref[S5_NSTATE:, t * w:(t + 1) * w] = (-(c_re * pi + c_im * pr)).astype(BF16)

    a8_ref[...] = jnp.concatenate(pw[S5_T], axis=1)


def _s5_prep(a_re, a_im, log_dt, b_re, b_im, c_re, c_im):
    g, p, ch = S5_GROUPS, S5_STATE, S5_CH
    eye = jnp.eye(g, dtype=bool)
    row = lambda t: t.reshape(1, g * p)
    col = lambda t: t.reshape(g * p, 1)
    ld = jnp.broadcast_to(log_dt[:, None], (g, p))

    def expand_b(b):
        t = jnp.transpose(b, (0, 2, 1))[:, :, None, :]
        return jnp.where(eye[:, None, :, None], t, 0.0).reshape(g * ch, g * p)

    def expand_c(c):
        t = jnp.transpose(c, (0, 2, 1))[:, :, None, :]
        return jnp.where(eye[:, None, :, None], t, 0.0).reshape(g * p, g * ch)

    sd = lambda shape, dt: jax.ShapeDtypeStruct(shape, dt)
    return pl.pallas_call(
        _s5prep_kernel,
        out_shape=[sd((S5_ROW, S5_SW), BF16), sd((S5_ROW, S5_ROW), BF16),
                   sd((S5_SW, S5_ROW), BF16), sd((1, S5_SW), F32)],
        compiler_params=pltpu.CompilerParams(vmem_limit_bytes=VMEM_LIMIT),
        name="s5_prep",
    )(row(a_re), row(a_im), row(ld), col(a_re), col(a_im), col(ld),
      expand_b(b_re), expand_b(b_im), expand_c(c_re), expand_c(c_im))


def _s5mm_kernel(u_ref, w_ref, o_ref):
    o_ref[...] = _dot(u_ref[...].astype(BF16), w_ref[...])


def _s5_chunk_matmul(uv, w_cat, tr=512, tn=1024):
    rows = uv.shape[0]
    cols = w_cat.shape[1]
    return pl.pallas_call(
        _s5mm_kernel,
        grid=(cols // tn, rows // tr),
        in_specs=[pl.BlockSpec((tr, S5_ROW), lambda j, i: (i, 0)),
                  pl.BlockSpec((S5_ROW, tn), lambda j, i: (0, j))],
        out_specs=pl.BlockSpec((tr, tn), lambda j, i: (i, j)),
        out_shape=jax.ShapeDtypeStruct((rows, cols), F32),
        compiler_params=_cparams(2),
        name="s5_chunk_matmul",
    )(uv, w_cat)


def _s5scan_kernel(v_ref, y0_ref, u_ref, a8_ref, ccr_ref, d_ref, o_ref, xs_ref, st_ref, *, tr):
    ns = S5_NSTATE

    @pl.when(pl.program_id(1) == 0)
    def _reset_state():
        st_ref[...] = jnp.zeros_like(st_ref)

    a_re = a8_ref[:, :ns]
    a_im = a8_ref[:, ns:]

    def body(r, carry):
        xr, xi = carry
        xs_ref[pl.ds(r, 1), :ns] = xr
        xs_ref[pl.ds(r, 1), ns:] = xi
        vr = v_ref[pl.ds(r, 1), :ns]
        vi = v_ref[pl.ds(r, 1), ns:]
        return (a_re * xr - a_im * xi + vr, a_re * xi + a_im * xr + vi)

    xr, xi = lax.fori_loop(0, tr, body, (st_ref[0:1, :ns], st_ref[0:1, ns:]), unroll=8)
    st_ref[0:1, :ns] = xr
    st_ref[0:1, ns:] = xi

    y = y0_ref[...] + _dot(xs_ref[...].astype(BF16), ccr_ref[...]) + d_ref[...] * u_ref[...]
    o_ref[...] = 0.5 * y * (1.0 + jnp.tanh(math.sqrt(2.0 / math.pi) * (y + 0.044715 * (y * y * y))))


def _s5_scan(vy, uv, a8, ccr, d_row, bsz, tr=256):
    rows = uv.shape[0]
    per_b = rows // bsz // tr
    blk = lambda col: pl.BlockSpec((tr, S5_ROW), lambda b, i: (b * per_b + i, col))
    const = lambda shape: pl.BlockSpec(shape, lambda b, i: (0, 0))
    return pl.pallas_call(
        functools.partial(_s5scan_kernel, tr=tr),
        grid=(bsz, per_b),
        in_specs=[blk(0), blk(1), blk(0), const((1, S5_SW)), const((S5_SW, S5_ROW)), const((1, S5_ROW))],
        out_specs=blk(0),
        out_shape=jax.ShapeDtypeStruct((rows, S5_ROW), F32),
        scratch_shapes=[pltpu.VMEM((tr, S5_SW), F32), pltpu.VMEM((8, S5_SW), F32)],
        compiler_params=_cparams(2),
        name="s5_scan",
    )(vy, vy, uv, a8, ccr, d_row)


def _s5(su, a_re, a_im, log_dt, b_re, b_im, c_re, c_im, d, bsz):
    n = su.shape[0]
    bst, bigk, ccr, a8 = _s5_prep(a_re, a_im, log_dt, b_re, b_im, c_re, c_im)
    uv = su.reshape(n // S5_T, S5_ROW)
    vy = _s5_chunk_matmul(uv, jnp.concatenate([bst, bigk], axis=1))
    d_row = jnp.tile(d.reshape(1, GROUP_WIDTH), (1, S5_T))
    return _s5_scan(vy, uv, a8, ccr, d_row, bsz).reshape(n, GROUP_WIDTH)


def _mixout_kernel(x_ref, o1_ref, l1_ref, o4_ref, l4_ref, o16_ref, l16_ref, yb_ref, yc_ref, yd_ref,
                   wglu_ref, bglu_ref, w_ref, g_ref, b_ref, out_ref):
    l1, l4, l16 = l1_ref[...], l4_ref[...], l16_ref[...]
    mx = jnp.maximum(jnp.maximum(l1, l4), l16)
    e1, e4, e16 = jnp.exp(l1 - mx), jnp.exp(l4 - mx), jnp.exp(l16 - mx)
    ya = (e1 * o1_ref[...] + e4 * o4_ref[...] + e16 * o16_ref[...]) / (e1 + e4 + e16)
    yd = yd_ref[...]
    yd = yd * _sigmoid(_dot(yd.astype(BF16), wglu_ref[...]) + bglu_ref[...])
    y = jnp.concatenate([ya.astype(BF16), yb_ref[...], yc_ref[...], yd.astype(BF16)], axis=1)
    h = _dot(y, w_ref[...])
    out_ref[...] = _layer_norm(DEEPNORM_ALPHA * x_ref[...] + h, g_ref[...], b_ref[...])


def _mixout(x2, dil, yb, yc, yd, wglu, bglu, w, g, b, tm=512):
    n = x2.shape[0]
    w4 = GROUP_WIDTH
    tok = lambda width: pl.BlockSpec((tm, width), lambda i: (i, 0))
    const = lambda shape: pl.BlockSpec(shape, lambda i: (0, 0))
    return pl.pallas_call(
        _mixout_kernel,
        grid=(n // tm,),
        in_specs=[tok(D_MODEL)] + [tok(w4)] * 9 + [const((w4, w4)), const((1, w4)),
                                                    const((D_MODEL, D_MODEL)), const((1, D_MODEL)),
                                                    const((1, D_MODEL))],
        out_specs=tok(D_MODEL),
        out_shape=jax.ShapeDtypeStruct((n, D_MODEL), F32),
        compiler_params=_cparams(1),
        name="mix_out",
    )(x2, *dil, yb, yc, yd, wglu, bglu, w, g, b)


def _kvproj_kernel(m_ref, w_ref, o_ref):
    mb = m_ref[...].astype(BF16)
    step = 512
    for j in range(0, 2 * D_MODEL, step):
        o_ref[:, j:j + step] = _dot(mb, w_ref[:, j:j + step]).astype(BF16)


def _kvproj(mem2, w_kv, tm=512):
    n = mem2.shape[0]
    return pl.pallas_call(
        _kvproj_kernel,
        grid=(n // tm,),
        in_specs=[pl.BlockSpec((tm, D_MODEL), lambda i: (i, 0)),
                  pl.BlockSpec((D_MODEL, 2 * D_MODEL), lambda i: (0, 0))],
        out_specs=pl.BlockSpec((tm, 2 * D_MODEL), lambda i: (i, 0)),
        out_shape=jax.ShapeDtypeStruct((n, 2 * D_MODEL), BF16),
        compiler_params=_cparams(1),
        name="mem_kv_proj",
    )(mem2, w_kv)


def _memattn_kernel(x_ref, wq_ref, kv_ref, wo_ref, g_ref, b_ref, out_ref):
    x = x_ref[...]
    hd = MEM_HEAD_DIM
    q = _dot(x.astype(BF16), wq_ref[...]).astype(BF16)
    outs = []
    for h in range(MEM_HEADS):
        k = kv_ref[:, h * hd:(h + 1) * hd]
        v = kv_ref[:, D_MODEL + h * hd:D_MODEL + (h + 1) * hd]
        s = _dot_nt(q[:, h * hd:(h + 1) * hd], k) * (hd ** -0.5)
        m = jnp.max(s, axis=-1, keepdims=True)
        p = jnp.exp(s - m)
        p = p / jnp.sum(p, axis=-1, keepdims=True)
        outs.append(_dot(p.astype(BF16), v).astype(BF16))
    h_out = _dot(jnp.concatenate(outs, axis=1), wo_ref[...])
    out_ref[...] = _layer_norm(DEEPNORM_ALPHA * x + h_out, g_ref[...], b_ref[...])


def _memattn(x2, wq, kv, wo, g, b, bsz, seq, tm=512):
    per_b = seq // tm
    const = lambda shape: pl.BlockSpec(shape, lambda bi, i: (0, 0))
    tok = pl.BlockSpec((tm, D_MODEL), lambda bi, i: (bi * per_b + i, 0))
    return pl.pallas_call(
        _memattn_kernel,
        grid=(bsz, per_b),
        in_specs=[tok, const((D_MODEL, D_MODEL)),
                  pl.BlockSpec((MEM_LEN, 2 * D_MODEL), lambda bi, i: (bi, 0)),
                  const((D_MODEL, D_MODEL)), const((1, D_MODEL)), const((1, D_MODEL))],
        out_specs=tok,
        out_shape=jax.ShapeDtypeStruct(x2.shape, F32),
        compiler_params=_cparams(2),
        name="mem_attn",
    )(x2, wq, kv, wo, g, b)


def _ffn_kernel(x_ref, wg_ref, wu_ref, wd_ref, g_ref, b_ref, out_ref, acc_ref, xb_ref):
    j = pl.program_id(1)

    @pl.when(j == 0)
    def _start():
        xb_ref[...] = x_ref[...].astype(BF16)
        acc_ref[...] = jnp.zeros_like(acc_ref)

    xb = xb_ref[...]
    gate = _dot(xb, wg_ref[...])
    up = _dot(xb, wu_ref[...])
    acc_ref[...] += _dot((gate * _sigmoid(gate) * up).astype(BF16), wd_ref[...])

    @pl.when(j == pl.num_programs(1) - 1)
    def _finish():
        out_ref[...] = _layer_norm(DEEPNORM_ALPHA * x_ref[...] + acc_ref[...], g_ref[...], b_ref[...])


def _ffn(x2, wg, wu, wd, g, b, tm=512, tf=1408):
    n = x2.shape[0]
    tok = pl.BlockSpec((tm, D_MODEL), lambda i, j: (i, 0))
    const = lambda shape: pl.BlockSpec(shape, lambda i, j: (0, 0))
    return pl.pallas_call(
        _ffn_kernel,
        grid=(n // tm, D_FF // tf),
        in_specs=[tok, pl.BlockSpec((D_MODEL, tf), lambda i, j: (0, j)),
                  pl.BlockSpec((D_MODEL, tf), lambda i, j: (0, j)),
                  pl.BlockSpec((tf, D_MODEL), lambda i, j: (j, 0)),
                  const((1, D_MODEL)), const((1, D_MODEL))],
        out_specs=tok,
        out_shape=jax.ShapeDtypeStruct(x2.shape, F32),
        scratch_shapes=[pltpu.VMEM((tm, D_MODEL), F32), pltpu.VMEM((tm, D_MODEL), BF16)],
        compiler_params=_cparams(2),
        name="ffn",
    )(x2, wg, wu, wd, g, b)


def _permute_w_in(w):
    c = w.shape[0]
    return jnp.concatenate([
        w[:, 0:1536], w[:, 1792:2304],
        w[:, 2576:2832],
        w[:, 1536:1792], w[:, 2320:2576],
        w[:, 2304:2320], jnp.zeros((c, 128 - GLA_GATE_RANK), w.dtype),
    ], axis=1).astype(BF16)


def _hybrid_mixer(x2, bsz, seq, w_in_p, ret_gn_g, gla_w_gate, gla_b_gate, gla_gn_g, s5_a_re, s5_a_im,
                  s5_log_dt, s5_b_re, s5_b_im, s5_c_re, s5_c_im, s5_d, s5_w_glu, s5_b_glu,
                  w_mix_out, ln_g, ln_b):
    pb, su, pf = _inproj(x2, w_in_p)
    dil = []
    for _, dilation in DIL_BRANCHES:
        dil.extend(_dilated_branch(pb, bsz, seq, dilation))
    yb = _retention(pb, pf, ret_gn_g.reshape(1, -1), bsz, seq)
    kwid = HEADS * GLA_DK
    w_gate_p = jnp.concatenate(
        [gla_w_gate, jnp.zeros((kwid - GLA_GATE_RANK, kwid), gla_w_gate.dtype)], axis=0).astype(BF16)
    yc = _gla(pb, pf, w_gate_p, gla_b_gate.reshape(1, -1), gla_gn_g.reshape(1, -1), bsz, seq)
    yd = _s5(su, s5_a_re, s5_a_im, s5_log_dt, s5_b_re, s5_b_im, s5_c_re, s5_c_im, s5_d, bsz)
    return _mixout(x2, dil, yb, yc, yd, s5_w_glu.astype(BF16), s5_b_glu.reshape(1, -1),
                   w_mix_out.astype(BF16), ln_g.reshape(1, -1), ln_b.reshape(1, -1))


def kernel(x, mem, w_in, ret_gn_g, gla_w_gate, gla_b_gate, gla_gn_g, s5_a_re, s5_a_im, s5_log_dt, s5_b_re,
           s5_b_im, s5_c_re, s5_c_im, s5_d, s5_w_glu, s5_b_glu, w_mix_out, ln_mix_g, ln_mix_b, w_mem_q,
           w_mem_kv, w_mem_o, ln_mem_g, ln_mem_b, w_ff_gate, w_ff_up, w_ff_down, ln_ff_g, ln_ff_b):
    bsz, seq, dm = x.shape
    assert dm == D_MODEL and mem.shape == (bsz, MEM_LEN, D_MODEL)
    assert seq % (ATTN_BLOCK * DIL_BRANCHES[-1][1]) == 0
    x2 = x.reshape(bsz * seq, dm)
    mem2 = mem.reshape(bsz * MEM_LEN, dm)
    row = lambda t: t.reshape(1, -1)
    for l in range(DEPTH):
        x2 = _hybrid_mixer(x2, bsz, seq, _permute_w_in(w_in[l]), ret_gn_g[l], gla_w_gate[l], gla_b_gate[l],
                           gla_gn_g[l], s5_a_re[l], s5_a_im[l], s5_log_dt[l], s5_b_re[l], s5_b_im[l],
                           s5_c_re[l], s5_c_im[l], s5_d[l], s5_w_glu[l], s5_b_glu[l], w_mix_out[l],
                           ln_mix_g[l], ln_mix_b[l])
        kv = _kvproj(mem2, w_mem_kv[l].astype(BF16))
        x2 = _memattn(x2, w_mem_q[l].astype(BF16), kv, w_mem_o[l].astype(BF16),
                      row(ln_mem_g[l]), row(ln_mem_b[l]), bsz, seq)
        x2 = _ffn(x2, w_ff_gate[l].astype(BF16), w_ff_up[l].astype(BF16), w_ff_down[l].astype(BF16),
                  row(ln_ff_g[l]), row(ln_ff_b[l]))
    return x2.reshape(bsz, seq, dm)
```

```python
import functools
import math

import jax
import jax.numpy as jnp
from jax import lax
from jax.experimental import pallas as pl
from jax.experimental.pallas import tpu as pltpu

F32 = jnp.float32
BF16 = jnp.bfloat16
HIGHEST = lax.Precision.HIGHEST
LANES = 128

D_MODEL = 1024
DEPTH = 2
GROUP_WIDTH = 256
HEADS = 4
HEAD_DIM = 64
DIL_BRANCHES = ((128, 1), (512, 4), (2048, 16))
ATTN_BLOCK = 128
RET_CHUNK = 128
GLA_DK = 32
GLA_GATE_RANK = 16
GLA_TAU = 16.0
GLA_CHUNK = 64
S5_CH = 16
S5_GROUPS = 16
S5_STATE = 64
S5_T = 8
MEM_LEN = 256
MEM_HEADS = 4
MEM_HEAD_DIM = 256
D_FF = 2816
DEEPNORM_ALPHA = (2 * DEPTH) ** 0.25
LN_EPS = 1e-5
NEG_INF = -1e30

PA_COLS = 768
PB_COLS = 1280
PF_COLS = 640
W_IN_COLS = PA_COLS + PB_COLS + GROUP_WIDTH + PF_COLS

VMEM_LIMIT = 56 * 1024 * 1024


def _cparams(n_axes):
    return pltpu.CompilerParams(dimension_semantics=("arbitrary",) * n_axes,
                                vmem_limit_bytes=VMEM_LIMIT)


def _dot(a, b):
    return jnp.dot(a, b, preferred_element_type=F32)


def _dot_nt(a, b):
    return lax.dot_general(a, b, (((1,), (1,)), ((), ())), preferred_element_type=F32)


def _dot_tn(a, b):
    return lax.dot_general(a, b, (((0,), (0,)), ((), ())), preferred_element_type=F32)


def _store_pages(ref, page, val, rows=slice(None)):
    for k in range(val.shape[1] // LANES):
        ref[page + k, rows, :] = val[:, k * LANES:(k + 1) * LANES]


def _load_pages(ref, page, rows, npages=2):
    return jnp.concatenate([ref[page + k, rows, :] for k in range(npages)], axis=1)


def _layer_norm(z, g, b):
    mu = jnp.mean(z, axis=-1, keepdims=True)
    d = z - mu
    var = jnp.mean(d * d, axis=-1, keepdims=True)
    return d * lax.rsqrt(var + LN_EPS) * g + b


def _sigmoid(x):
    return 1.0 / (1.0 + jnp.exp(-x))


def _head_of_lane(shape, head_width):
    return lax.broadcasted_iota(jnp.int32, shape, len(shape) - 1) // head_width


def _stack_heads(t, head_width):
    lane_head = _head_of_lane(t.shape, head_width)
    zero = jnp.zeros_like(t)
    return jnp.concatenate([jnp.where(lane_head == h, t, zero) for h in range(HEADS)], axis=0)


def _select_heads(t4, lane_head):
    n = t4.shape[0] // HEADS
    out = t4[(HEADS - 1) * n:]
    for h in range(HEADS - 2, -1, -1):
        out = jnp.where(lane_head == h, t4[h * n:(h + 1) * n], out)
    return out


def _split_dot(x, w_bf16, terms):
    acc = None
    for _ in range(terms):
        piece = x.astype(BF16)
        part = _dot(piece, w_bf16)
        acc = part if acc is None else acc + part
        x = x - piece.astype(F32)
    return acc


def _head_norm(o, avg_bf16):
    d = o - _split_dot(o, avg_bf16, 2)
    var = _split_dot(d * d, avg_bf16, 2)
    return d * lax.rsqrt(var + LN_EPS)


def _head_avg_matrix(width, head_width):
    r = lax.broadcasted_iota(jnp.int32, (width, width), 0) // head_width
    c = lax.broadcasted_iota(jnp.int32, (width, width), 1) // head_width
    return jnp.where(r == c, 1.0 / head_width, 0.0).astype(BF16)


def _inproj_kernel(x_ref, w_ref, pa_ref, pb_ref, suv_ref, pf_ref, su_scr):
    xb = x_ref[...].astype(BF16)
    step = 256
    for j in range(0, PA_COLS, step):
        _store_pages(pa_ref, j // LANES, _dot(xb, w_ref[:, j:j + step]))
    for j in range(0, PB_COLS, step):
        pb_ref[:, j:j + step] = _dot(xb, w_ref[:, PA_COLS + j:PA_COLS + j + step]).astype(BF16)
    off = PA_COLS + PB_COLS
    _store_pages(su_scr, 0, _dot(xb, w_ref[:, off:off + GROUP_WIDTH]))
    nrow = su_scr.shape[1] // S5_T
    for s in range(S5_T):
        suv_ref[:, s * GROUP_WIDTH:(s + 1) * GROUP_WIDTH] = _load_pages(su_scr, 0, pl.ds(s, nrow, stride=S5_T))
    pf_ref[...] = _dot(xb, w_ref[:, off + GROUP_WIDTH:])


def _inproj(x2, w_p, tm=512):
    n = x2.shape[0]
    return pl.pallas_call(
        _inproj_kernel,
        grid=(n // tm,),
        in_specs=[pl.BlockSpec((tm, D_MODEL), lambda i: (i, 0)),
                  pl.BlockSpec((D_MODEL, W_IN_COLS), lambda i: (0, 0))],
        out_specs=[pl.BlockSpec((PA_COLS // LANES, tm, LANES), lambda i: (0, i, 0)),
                   pl.BlockSpec((tm, PB_COLS), lambda i: (i, 0)),
                   pl.BlockSpec((tm // S5_T, S5_T * GROUP_WIDTH), lambda i: (i, 0)),
                   pl.BlockSpec((tm, PF_COLS), lambda i: (i, 0))],
        out_shape=[jax.ShapeDtypeStruct((PA_COLS // LANES, n, LANES), F32),
                   jax.ShapeDtypeStruct((n, PB_COLS), BF16),
                   jax.ShapeDtypeStruct((n // S5_T, S5_T * GROUP_WIDTH), F32),
                   jax.ShapeDtypeStruct((n, PF_COLS), F32)],
        scratch_shapes=[pltpu.VMEM((GROUP_WIDTH // LANES, tm, LANES), F32)],
        compiler_params=_cparams(1),
        name="inproj",
    )(x2, w_p)


DIL_SPAN = ATTN_BLOCK * max(r for _, r in DIL_BRANCHES)


def _dil_tables(dilation):
    blk = ATTN_BLOCK
    shape = (HEADS * blk, 2 * blk)
    row = lax.broadcasted_iota(jnp.int32, shape, 0)
    ki = lax.broadcasted_iota(jnp.int32, shape, 1)
    head = row // blk
    dist = row % blk + blk - ki
    valid = (dist >= 0) & (dist <= blk)
    slope = jnp.where(head == 0, 2.0 ** -2, jnp.where(head == 1, 2.0 ** -4,
                      jnp.where(head == 2, 2.0 ** -6, 2.0 ** -8))).astype(F32)
    alibi = -slope * (dist * dilation).astype(F32)
    return jnp.where(valid, alibi, NEG_INF), jnp.where(valid & (ki >= blk), alibi, NEG_INF)


def _dil_attend(q, kwin, vwin, tb):
    q4 = _stack_heads(q.astype(BF16), HEAD_DIM)
    s = _dot_nt(q4, kwin.astype(BF16)) * (HEAD_DIM ** -0.5) + tb
    m = jnp.max(s, axis=-1, keepdims=True)
    p = jnp.exp(s - m)
    l = jnp.sum(p, axis=-1, keepdims=True)
    o4 = _dot(p.astype(BF16), vwin.astype(BF16)) * (1.0 / l)
    lse4 = m + jnp.log(l)
    lane_head = _head_of_lane((ATTN_BLOCK, GROUP_WIDTH), HEAD_DIM)
    return _select_heads(o4, lane_head), _select_heads(jnp.broadcast_to(lse4, o4.shape), lane_head)


def _dil_kernel(q_ref, kp_ref, kc_ref, vp_ref, vc_ref, o_ref, tb_ref, ob_ref, lb_ref):
    blk = ATTN_BLOCK
    span = DIL_SPAN

    @pl.when((pl.program_id(0) == 0) & (pl.program_id(1) == 0))
    def _init_tables():
        for bi, (_, r) in enumerate(DIL_BRANCHES):
            t_any, t_first = _dil_tables(r)
            tb_ref[bi, 0] = t_any
            tb_ref[bi, 1] = t_first

    first = jnp.where(pl.program_id(1) == 0, 1, 0)

    def rows(start, size, r):
        if r == 1:
            return pl.ds(pl.multiple_of(start, blk), size)
        return pl.ds(start, size, stride=r)

    for bi, (_, r) in enumerate(DIL_BRANCHES):
        per_sub = span // (blk * r)

        def boundary(c, carry, bi=bi, r=r):
            rq = rows(c, blk, r)
            rp = rows(c + span - blk * r, blk, r)
            kwin = jnp.concatenate([_load_pages(kp_ref, 0, rp), _load_pages(kc_ref, 0, rq)], axis=0)
            vwin = jnp.concatenate([_load_pages(vp_ref, 0, rp), _load_pages(vc_ref, 0, rq)], axis=0)
            o, lse = _dil_attend(_load_pages(q_ref, 0, rq), kwin, vwin, tb_ref[bi, first])
            _store_pages(ob_ref, 2 * bi, o, rq)
            _store_pages(lb_ref, 2 * bi, lse, rq)
            return carry

        lax.fori_loop(0, r, boundary, 0)

        if per_sub > 1:
            def interior(idx, carry, bi=bi, r=r):
                c = lax.rem(idx, r)
                j = 1 + lax.div(idx, r)
                rq = rows(c + blk * r * j, blk, r)
                rk = rows(c + blk * r * (j - 1), 2 * blk, r)
                o, lse = _dil_attend(_load_pages(q_ref, 0, rq), _load_pages(kc_ref, 0, rk),
                                     _load_pages(vc_ref, 0, rk), tb_ref[bi, 0])
                _store_pages(ob_ref, 2 * bi, o, rq)
                _store_pages(lb_ref, 2 * bi, lse, rq)
                return carry

            lax.fori_loop(0, r * (per_sub - 1), interior, 0)

    def combine(i, carry):
        rr = pl.ds(pl.multiple_of(i * blk, blk), blk)
        l1, l4, l16 = (_load_pages(lb_ref, 2 * bi, rr) for bi in range(3))
        mx = jnp.maximum(jnp.maximum(l1, l4), l16)
        e1, e4, e16 = jnp.exp(l1 - mx), jnp.exp(l4 - mx), jnp.exp(l16 - mx)
        o1, o4, o16 = (_load_pages(ob_ref, 2 * bi, rr) for bi in range(3))
        ya = (e1 * o1 + e4 * o4 + e16 * o16) / (e1 + e4 + e16)
        o_ref[rr, :] = ya.astype(BF16)
        return carry

    lax.fori_loop(0, span // blk, combine, 0)


def _dilated_attention(pa, bsz, seq):
    n = bsz * seq
    span = DIL_SPAN
    per_b = seq // span
    w = GROUP_WIDTH
    pages = w // LANES

    def cur(col):
        return pl.BlockSpec((pages, span, LANES), lambda b, i: (col, b * per_b + i, 0))

    def prev(col):
        return pl.BlockSpec((pages, span, LANES), lambda b, i: (col, b * per_b + jnp.maximum(i - 1, 0), 0))

    nbr = len(DIL_BRANCHES)
    return pl.pallas_call(
        _dil_kernel,
        grid=(bsz, per_b),
        in_specs=[cur(0), prev(1), cur(1), prev(2), cur(2)],
        out_specs=pl.BlockSpec((span, w), lambda b, i: (b * per_b + i, 0)),
        out_shape=jax.ShapeDtypeStruct((n, w), BF16),
        scratch_shapes=[pltpu.VMEM((nbr, 2, HEADS * ATTN_BLOCK, 2 * ATTN_BLOCK), F32),
                        pltpu.VMEM((nbr * pages, span, LANES), F32),
                        pltpu.VMEM((nbr * pages, span, LANES), F32)],
        compiler_params=_cparams(2),
        name="dilated_attention",
    )(pa, pa, pa, pa, pa)


def _ret_kernel(q_ref, k_ref, v_ref, g_ref, gn_ref, o_ref, r_ref, dtab_ref, qd_ref, kd_ref, o_scr, u_scr,
                *, nchunk):
    c = RET_CHUNK
    width = GROUP_WIDTH
    scale = HEAD_DIM ** -0.5
    log_g = [math.log(1.0 - 2.0 ** (-5.0 - h)) for h in range(HEADS)]

    def head_log_g(head):
        return jnp.where(head == 0, log_g[0], jnp.where(head == 1, log_g[1],
                         jnp.where(head == 2, log_g[2], log_g[3]))).astype(F32)

    @pl.when((pl.program_id(0) == 0) & (pl.program_id(1) == 0))
    def _init_tables():
        row = lax.broadcasted_iota(jnp.int32, (HEADS * c, c), 0)
        key = lax.broadcasted_iota(jnp.int32, (HEADS * c, c), 1)
        rel = (row % c - key).astype(F32)
        decay = jnp.exp(jnp.maximum(rel, 0.0) * head_log_g(row // c))
        dtab_ref[...] = jnp.where(rel >= 0, decay, 0.0) * scale
        pos = (lax.broadcasted_iota(jnp.int32, (nchunk * c, width), 0) % c).astype(F32)
        lgl = head_log_g(_head_of_lane((nchunk * c, width), HEAD_DIM))
        qd_ref[...] = jnp.exp((pos + 1.0) * lgl)
        kd_ref[...] = jnp.exp((c - 1.0 - pos) * lgl) * scale

    @pl.when(pl.program_id(1) == 0)
    def _reset_state():
        r_ref[...] = jnp.zeros_like(r_ref)

    rr = lax.broadcasted_iota(jnp.int32, (width, width), 0) // HEAD_DIM
    cc = lax.broadcasted_iota(jnp.int32, (width, width), 1) // HEAD_DIM
    same_head = rr == cc
    chunk_decay = jnp.exp(c * head_log_g(_head_of_lane((1, width), HEAD_DIM)))
    lane_head = _head_of_lane((c, width), HEAD_DIM)

    q_dec = (q_ref[...].astype(F32) * qd_ref[...]).astype(BF16)
    k_dec = (k_ref[...].astype(F32) * kd_ref[...]).astype(BF16)

    for ci in range(nchunk):
        sl = slice(ci * c, (ci + 1) * c)
        k = k_ref[sl, :]
        v = v_ref[sl, :]
        q4 = _stack_heads(q_ref[sl, :], HEAD_DIM)
        scores = _dot_nt(q4, k) * dtab_ref[...]
        o_scr[sl, :] = _select_heads(_dot(scores.astype(BF16), v), lane_head)
        u_scr[ci] = jnp.where(same_head, _dot_tn(k_dec[sl, :], v), 0.0)

    for ci in range(nchunk):
        sl = slice(ci * c, (ci + 1) * c)
        r_prev = r_ref[...]
        o_scr[sl, :] += _dot(q_dec[sl, :], r_prev.astype(BF16))
        r_ref[...] = r_prev * chunk_decay + u_scr[ci]

    o = _head_norm(o_scr[...], _head_avg_matrix(width, HEAD_DIM)) * gn_ref[...]
    g = g_ref[...]
    o_ref[...] = (o * (g * _sigmoid(g))).astype(BF16)


def _retention(pb, pf, gn, bsz, seq, tq=512):
    n = bsz * seq
    per_b = seq // tq
    w = GROUP_WIDTH
    nchunk = tq // RET_CHUNK

    def tok(col):
        return pl.BlockSpec((tq, w), lambda b, i: (b * per_b + i, col))

    return pl.pallas_call(
        functools.partial(_ret_kernel, nchunk=nchunk),
        grid=(bsz, per_b),
        in_specs=[tok(0), tok(1), tok(2), tok(0), pl.BlockSpec((1, w), lambda b, i: (0, 0))],
        out_specs=tok(0),
        out_shape=jax.ShapeDtypeStruct((n, w), BF16),
        scratch_shapes=[pltpu.VMEM((w, w), F32),
                        pltpu.VMEM((HEADS * RET_CHUNK, RET_CHUNK), F32),
                        pltpu.VMEM((tq, w), F32),
                        pltpu.VMEM((tq, w), F32),
                        pltpu.VMEM((tq, w), F32),
                        pltpu.VMEM((nchunk, w, w), F32)],
        compiler_params=_cparams(2),
        name="retention",
    )(pb, pb, pb, pf, gn)


def _gla_kernel(q_ref, k_ref, v_ref, gl_ref, og_ref, wg_ref, bg_ref, gn_ref, o_ref, rt_ref, cum_ref, o_scr,
                ut_scr, *, nchunk):
    c = GLA_CHUNK
    tq = nchunk * c
    kwid = HEADS * GLA_DK
    vwid = GROUP_WIDTH
    scale = GLA_DK ** -0.5

    @pl.when((pl.program_id(0) == 0) & (pl.program_id(1) == 0))
    def _init_tables():
        i = lax.broadcasted_iota(jnp.int32, (2 * tq, tq), 0)
        j = lax.broadcasted_iota(jnp.int32, (2 * tq, tq), 1)
        ii = i % tq
        same_chunk = (ii // c) == (j // c)
        keep = same_chunk & ((i >= tq) | (ii >= j))
        cum_ref[...] = jnp.where(keep, 1.0, 0.0).astype(BF16)

    @pl.when(pl.program_id(1) == 0)
    def _reset_state():
        rt_ref[...] = jnp.zeros_like(rt_ref)

    causal = lax.broadcasted_iota(jnp.int32, (c, HEADS * c), 0) >= (
        lax.broadcasted_iota(jnp.int32, (c, HEADS * c), 1) % c)
    rr = lax.broadcasted_iota(jnp.int32, (vwid, kwid), 0) // HEAD_DIM
    cc = lax.broadcasted_iota(jnp.int32, (vwid, kwid), 1) // GLA_DK
    same_head = rr == cc

    z = _dot(gl_ref[...].astype(BF16), wg_ref[...]) + bg_ref[...]
    log_a = (jnp.minimum(z, 0.0) - jnp.log(1.0 + jnp.exp(-jnp.abs(z)))) * (1.0 / GLA_TAU)
    hi = log_a.astype(BF16)
    rest = log_a - hi.astype(F32)
    mid = rest.astype(BF16)
    lo = (rest - mid.astype(F32)).astype(BF16)
    sums = _dot(cum_ref[...], jnp.concatenate([hi, mid, lo], axis=1))
    sums = sums[:, :kwid] + sums[:, kwid:2 * kwid] + sums[:, 2 * kwid:]
    bcum = sums[:tq]
    blast = sums[tq:]
    q = q_ref[...].astype(F32)
    k = k_ref[...].astype(F32)
    q_in = (q * scale * jnp.exp(bcum)).astype(BF16)
    k_out = (k * jnp.exp(-bcum)).astype(BF16)
    k_st = (k * jnp.exp(blast - bcum)).astype(BF16)
    chunk_decay = jnp.exp(blast)

    for ci in range(nchunk):
        sl = slice(ci * c, (ci + 1) * c)
        v = v_ref[sl, :]
        att = _dot_nt(q_in[sl, :], _stack_heads(k_out[sl, :], GLA_DK))
        att = jnp.where(causal, att, 0.0)
        o_scr[sl, :] = _dot(att.astype(BF16), _stack_heads(v, HEAD_DIM))
        ut_scr[ci] = jnp.where(same_head, _dot_tn(v, k_st[sl, :]), 0.0)

    for ci in range(nchunk):
        sl = slice(ci * c, (ci + 1) * c)
        rt_prev = rt_ref[...]
        o_scr[sl, :] += _dot_nt(q_in[sl, :], rt_prev.astype(BF16))
        rt_ref[...] = rt_prev * chunk_decay[ci * c:ci * c + 1, :] + ut_scr[ci]

    o = _head_norm(o_scr[...], _head_avg_matrix(vwid, HEAD_DIM)) * gn_ref[...]
    g = og_ref[...]
    o_ref[...] = (o * (g * _sigmoid(g))).astype(BF16)


def _gla(pb, pf, w_gate_p, b_gate, gn, bsz, seq, tq=512):
    n = bsz * seq
    per_b = seq // tq
    kwid = HEADS * GLA_DK
    nchunk = tq // GLA_CHUNK

    def tok(width, col):
        return pl.BlockSpec((tq, width), lambda b, i: (b * per_b + i, col))

    def const(shape):
        return pl.BlockSpec(shape, lambda b, i: (0, 0))

    return pl.pallas_call(
        functools.partial(_gla_kernel, nchunk=nchunk),
        grid=(bsz, per_b),
        in_specs=[tok(kwid, 6), tok(kwid, 7), tok(GROUP_WIDTH, 4),
                  tok(kwid, 4), tok(GROUP_WIDTH, 1),
                  const((kwid, kwid)), const((1, kwid)), const((1, GROUP_WIDTH))],
        out_specs=tok(GROUP_WIDTH, 0),
        out_shape=jax.ShapeDtypeStruct((n, GROUP_WIDTH), BF16),
        scratch_shapes=[pltpu.VMEM((GROUP_WIDTH, kwid), F32),
                        pltpu.VMEM((2 * tq, tq), BF16),
                        pltpu.VMEM((tq, GROUP_WIDTH), F32),
                        pltpu.VMEM((nchunk, GROUP_WIDTH, kwid), F32)],
        compiler_params=_cparams(2),
        name="gla",
    )(pb, pb, pb, pf, pf, w_gate_p, b_gate, gn)


S5_ROW = S5_T * GROUP_WIDTH
S5_NSTATE = S5_GROUPS * S5_STATE
S5_SW = 2 * S5_NSTATE


def _s5prep_kernel(ar_ref, ai_ref, ld_ref, arc_ref, aic_ref, ldc_ref, bre_ref, bim_ref, cre_ref, cim_ref,
                   bst_ref, bigk_ref, ccr_ref, a8_ref):
    w = GROUP_WIDTH

    def discretise(ar, ai, ld):
        dt = jnp.exp(ld)
        ea = jnp.exp(ar * dt)
        return ea * jnp.cos(ai * dt), ea * jnp.sin(ai * dt)

    def powers(a_re, a_im, n):
        pw = [(jnp.ones_like(a_re), jnp.zeros_like(a_re)), (a_re, a_im)]
        for _ in range(2, n + 1):
            pr, pi = pw[-1]
            pw.append((pr * a_re - pi * a_im, pr * a_im + pi * a_re))
        return pw

    ar, ai = ar_ref[...], ai_ref[...]
    a_re, a_im = discretise(ar, ai, ld_ref[...])
    pw = powers(a_re, a_im, S5_T)
    den = ar * ar + ai * ai
    nr = a_re - 1.0
    f_re = (nr * ar + a_im * ai) / den
    f_im = (a_im * ar - nr * ai) / den
    b_re, b_im = bre_ref[...], bim_ref[...]
    bb_re = f_re * b_re - f_im * b_im
    bb_im = f_re * b_im + f_im * b_re
    c_re, c_im = cre_ref[...], cim_ref[...]
    cmat = jnp.concatenate([c_re, -c_im], axis=0)

    bigk_ref[...] = jnp.zeros_like(bigk_ref)
    for lag in range(S5_T):
        pr, pi = pw[lag]
        lmat = jnp.concatenate([pr * bb_re - pi * bb_im, pr * bb_im + pi * bb_re], axis=1)
        s = S5_T - 1 - lag
        bst_ref[s * w:(s + 1) * w, :] = lmat.astype(BF16)
        m = jnp.dot(lmat, cmat, precision=HIGHEST, preferred_element_type=F32).astype(BF16)
        for s in range(S5_T - lag):
            t = s + lag
            bigk_ref[s * w:(s + 1) * w, t * w:(t + 1) * w] = m

    ac_re, ac_im = discretise(arc_ref[...], aic_ref[...], ldc_ref[...])
    pwc = powers(ac_re, ac_im, S5_T)
    for t in range(S5_T):
        pr, pi = pwc[t + 1]
        ccr_ref[:S5_NSTATE, t * w:(t + 1) * w] = (c_re * pr - c_im * pi).astype(BF16)
        ccr_ref[S5_NSTATE:, t * w:(t + 1) * w] = (-(c_re * pi + c_im * pr)).astype(BF16)

    a8_ref[...] = jnp.concatenate(pw[S5_T], axis=1)


def _s5_prep(a_re, a_im, log_dt, b_re, b_im, c_re, c_im):
    g, p, ch = S5_GROUPS, S5_STATE, S5_CH
    eye = jnp.eye(g, dtype=bool)
    row = lambda t: t.reshape(1, g * p)
    col = lambda t: t.reshape(g * p, 1)
    ld = jnp.broadcast_to(log_dt[:, None], (g, p))

    def expand_b(b):
        t = jnp.transpose(b, (0, 2, 1))[:, :, None, :]
        return jnp.where(eye[:, None, :, None], t, 0.0).reshape(g * ch, g * p)

    def expand_c(c):
        t = jnp.transpose(c, (0, 2, 1))[:, :, None, :]
        return jnp.where(eye[:, None, :, None], t, 0.0).reshape(g * p, g * ch)

    sd = lambda shape, dt: jax.ShapeDtypeStruct(shape, dt)
    return pl.pallas_call(
        _s5prep_kernel,
        out_shape=[sd((S5_ROW, S5_SW), BF16), sd((S5_ROW, S5_ROW), BF16),
                   sd((S5_SW, S5_ROW), BF16), sd((1, S5_SW), F32)],
        compiler_params=pltpu.CompilerParams(vmem_limit_bytes=VMEM_LIMIT),
        name="s5_prep",
    )(row(a_re), row(a_im), row(ld), col(a_re), col(a_im), col(ld),
      expand_b(b_re), expand_b(b_im), expand_c(c_re), expand_c(c_im))


def _s5mm_kernel(u_ref, w_ref, o_ref):
    o_ref[...] = _dot(u_ref[...].astype(BF16), w_ref[...])


def _s5_chunk_matmul(uv, w_cat, tr=512, tn=1024):
    rows = uv.shape[0]
    cols = w_cat.shape[1]
    return pl.pallas_call(
        _s5mm_kernel,
        grid=(cols // tn, rows // tr),
        in_specs=[pl.BlockSpec((tr, S5_ROW), lambda j, i: (i, 0)),
                  pl.BlockSpec((S5_ROW, tn), lambda j, i: (0, j))],
        out_specs=pl.BlockSpec((tr, tn), lambda j, i: (i, j)),
        out_shape=jax.ShapeDtypeStruct((rows, cols), F32),
        compiler_params=_cparams(2),
        name="s5_chunk_matmul",
    )(uv, w_cat)


def _s5scan_kernel(v_ref, y0_ref, u_ref, a8_ref, ccr_ref, d_ref, o_ref, xs_ref, st_ref, *, tr):
    ns = S5_NSTATE

    @pl.when(pl.program_id(1) == 0)
    def _reset_state():
        st_ref[...] = jnp.zeros_like(st_ref)

    a_re = a8_ref[:, :ns]
    a_im = a8_ref[:, ns:]

    def body(r, carry):
        xr, xi = carry
        xs_ref[pl.ds(r, 1), :ns] = xr
        xs_ref[pl.ds(r, 1), ns:] = xi
        vr = v_ref[pl.ds(r, 1), :ns]
        vi = v_ref[pl.ds(r, 1), ns:]
        return (a_re * xr - a_im * xi + vr, a_re * xi + a_im * xr + vi)

    xr, xi = lax.fori_loop(0, tr, body, (st_ref[0:1, :ns], st_ref[0:1, ns:]), unroll=8)
    st_ref[0:1, :ns] = xr
    st_ref[0:1, ns:] = xi

    y = y0_ref[...] + _dot(xs_ref[...].astype(BF16), ccr_ref[...]) + d_ref[...] * u_ref[...]
    y = 0.5 * y * (1.0 + jnp.tanh(math.sqrt(2.0 / math.pi) * (y + 0.044715 * (y * y * y))))
    for s in range(S5_T):
        _store_pages(o_ref, 0, y[:, s * GROUP_WIDTH:(s + 1) * GROUP_WIDTH], pl.ds(s, tr, stride=S5_T))


def _s5_scan(vy, uv, a8, ccr, d_row, bsz, tr=256):
    rows = uv.shape[0]
    per_b = rows // bsz // tr
    pages = GROUP_WIDTH // LANES
    blk = lambda col: pl.BlockSpec((tr, S5_ROW), lambda b, i: (b * per_b + i, col))
    const = lambda shape: pl.BlockSpec(shape, lambda b, i: (0, 0))
    return pl.pallas_call(
        functools.partial(_s5scan_kernel, tr=tr),
        grid=(bsz, per_b),
        in_specs=[blk(0), blk(1), blk(0), const((1, S5_SW)), const((S5_SW, S5_ROW)), const((1, S5_ROW))],
        out_specs=pl.BlockSpec((pages, tr * S5_T, LANES), lambda b, i: (0, b * per_b + i, 0)),
        out_shape=jax.ShapeDtypeStruct((pages, rows * S5_T, LANES), F32),
        scratch_shapes=[pltpu.VMEM((tr, S5_SW), F32), pltpu.VMEM((8, S5_SW), F32)],
        compiler_params=_cparams(2),
        name="s5_scan",
    )(vy, vy, uv, a8, ccr, d_row)


def _s5(uv, a_re, a_im, log_dt, b_re, b_im, c_re, c_im, d, bsz):
    bst, bigk, ccr, a8 = _s5_prep(a_re, a_im, log_dt, b_re, b_im, c_re, c_im)
    vy = _s5_chunk_matmul(uv, jnp.concatenate([bst, bigk], axis=1))
    d_row = jnp.tile(d.reshape(1, GROUP_WIDTH), (1, S5_T))
    return _s5_scan(vy, uv, a8, ccr, d_row, bsz)


def _mixout_kernel(x_ref, ya_ref, yb_ref, yc_ref, yd_ref, wglu_ref, bglu_ref, w_ref, g_ref, b_ref, out_ref):
    yd = _load_pages(yd_ref, 0, slice(None))
    yd = yd * _sigmoid(_dot(yd.astype(BF16), wglu_ref[...]) + bglu_ref[...])
    y = jnp.concatenate([ya_ref[...], yb_ref[...], yc_ref[...], yd.astype(BF16)], axis=1)
    h = _dot(y, w_ref[...])
    out_ref[...] = _layer_norm(DEEPNORM_ALPHA * x_ref[...] + h, g_ref[...], b_ref[...])


def _mixout(x2, ya, yb, yc, yd, wglu, bglu, w, g, b, tm=512):
    n = x2.shape[0]
    w4 = GROUP_WIDTH
    tok = lambda width: pl.BlockSpec((tm, width), lambda i: (i, 0))
    const = lambda shape: pl.BlockSpec(shape, lambda i: (0, 0))
    return pl.pallas_call(
        _mixout_kernel,
        grid=(n // tm,),
        in_specs=[tok(D_MODEL)] + [tok(w4)] * 3 + [pl.BlockSpec((w4 // LANES, tm, LANES), lambda i: (0, i, 0)),
                                                    const((w4, w4)), const((1, w4)),
                                                    const((D_MODEL, D_MODEL)), const((1, D_MODEL)),
                                                    const((1, D_MODEL))],
        out_specs=tok(D_MODEL),
        out_shape=jax.ShapeDtypeStruct((n, D_MODEL), F32),
        compiler_params=_cparams(1),
        name="mix_out",
    )(x2, ya, yb, yc, yd, wglu, bglu, w, g, b)


def _kvproj_kernel(m_ref, w_ref, o_ref):
    mb = m_ref[...].astype(BF16)
    step = 512
    for j in range(0, 2 * D_MODEL, step):
        o_ref[:, j:j + step] = _dot(mb, w_ref[:, j:j + step]).astype(BF16)


def _kvproj(mem2, w_kv, tm=512):
    n = mem2.shape[0]
    return pl.pallas_call(
        _kvproj_kernel,
        grid=(n // tm,),
        in_specs=[pl.BlockSpec((tm, D_MODEL), lambda i: (i, 0)),
                  pl.BlockSpec((D_MODEL, 2 * D_MODEL), lambda i: (0, 0))],
        out_specs=pl.BlockSpec((tm, 2 * D_MODEL), lambda i: (i, 0)),
        out_shape=jax.ShapeDtypeStruct((n, 2 * D_MODEL), BF16),
        compiler_params=_cparams(1),
        name="mem_kv_proj",
    )(mem2, w_kv)


def _memattn_kernel(x_ref, wq_ref, kv_ref, wo_ref, g_ref, b_ref, out_ref):
    x = x_ref[...]
    hd = MEM_HEAD_DIM
    q = _dot(x.astype(BF16), wq_ref[...]).astype(BF16)
    outs = []
    for h in range(MEM_HEADS):
        k = kv_ref[:, h * hd:(h + 1) * hd]
        v = kv_ref[:, D_MODEL + h * hd:D_MODEL + (h + 1) * hd]
        s = _dot_nt(q[:, h * hd:(h + 1) * hd], k) * (hd ** -0.5)
        m = jnp.max(s, axis=-1, keepdims=True)
        p = jnp.exp(s - m)
        p = p / jnp.sum(p, axis=-1, keepdims=True)
        outs.append(_dot(p.astype(BF16), v).astype(BF16))
    h_out = _dot(jnp.concatenate(outs, axis=1), wo_ref[...])
    out_ref[...] = _layer_norm(DEEPNORM_ALPHA * x + h_out, g_ref[...], b_ref[...])


def _memattn(x2, wq, kv, wo, g, b, bsz, seq, tm=512):
    per_b = seq // tm
    const = lambda shape: pl.BlockSpec(shape, lambda bi, i: (0, 0))
    tok = pl.BlockSpec((tm, D_MODEL), lambda bi, i: (bi * per_b + i, 0))
    return pl.pallas_call(
        _memattn_kernel,
        grid=(bsz, per_b),
        in_specs=[tok, const((D_MODEL, D_MODEL)),
                  pl.BlockSpec((MEM_LEN, 2 * D_MODEL), lambda bi, i: (bi, 0)),
                  const((D_MODEL, D_MODEL)), const((1, D_MODEL)), const((1, D_MODEL))],
        out_specs=tok,
        out_shape=jax.ShapeDtypeStruct(x2.shape, F32),
        compiler_params=_cparams(2),
        name="mem_attn",
    )(x2, wq, kv, wo, g, b)


def _ffn_kernel(x_ref, wg_ref, wu_ref, wd_ref, g_ref, b_ref, out_ref, acc_ref, xb_ref):
    j = pl.program_id(1)

    @pl.when(j == 0)
    def _start():
        xb_ref[...] = x_ref[...].astype(BF16)
        acc_ref[...] = jnp.zeros_like(acc_ref)

    xb = xb_ref[...]
    gate = _dot(xb, wg_ref[...])
    up = _dot(xb, wu_ref[...])
    acc_ref[...] += _dot((gate * _sigmoid(gate) * up).astype(BF16), wd_ref[...])

    @pl.when(j == pl.num_programs(1) - 1)
    def _finish():
        out_ref[...] = _layer_norm(DEEPNORM_ALPHA * x_ref[...] + acc_ref[...], g_ref[...], b_ref[...])


def _ffn(x2, wg, wu, wd, g, b, tm=512, tf=1408):
    n = x2.shape[0]
    tok = pl.BlockSpec((tm, D_MODEL), lambda i, j: (i, 0))
    const = lambda shape: pl.BlockSpec(shape, lambda i, j: (0, 0))
    return pl.pallas_call(
        _ffn_kernel,
        grid=(n // tm, D_FF // tf),
        in_specs=[tok, pl.BlockSpec((D_MODEL, tf), lambda i, j: (0, j)),
                  pl.BlockSpec((D_MODEL, tf), lambda i, j: (0, j)),
                  pl.BlockSpec((tf, D_MODEL), lambda i, j: (j, 0)),
                  const((1, D_MODEL)), const((1, D_MODEL))],
        out_specs=tok,
        out_shape=jax.ShapeDtypeStruct(x2.shape, F32),
        scratch_shapes=[pltpu.VMEM((tm, D_MODEL), F32), pltpu.VMEM((tm, D_MODEL), BF16)],
        compiler_params=_cparams(2),
        name="ffn",
    )(x2, wg, wu, wd, g, b)


def _permute_w_in(w):
    c = w.shape[0]
    return jnp.concatenate([
        w[:, 0:1536], w[:, 1792:2304],
        w[:, 2576:2832],
        w[:, 1536:1792], w[:, 2320:2576],
        w[:, 2304:2320], jnp.zeros((c, LANES - GLA_GATE_RANK), w.dtype),
    ], axis=1).astype(BF16)


def _hybrid_mixer(x2, bsz, seq, w_in_p, ret_gn_g, gla_w_gate, gla_b_gate, gla_gn_g, s5_a_re, s5_a_im,
                  s5_log_dt, s5_b_re, s5_b_im, s5_c_re, s5_c_im, s5_d, s5_w_glu, s5_b_glu,
                  w_mix_out, ln_g, ln_b):
    pa, pb, uv, pf = _inproj(x2, w_in_p)
    ya = _dilated_attention(pa, bsz, seq)
    yb = _retention(pb, pf, ret_gn_g.reshape(1, -1), bsz, seq)
    kwid = HEADS * GLA_DK
    w_gate_p = jnp.concatenate(
        [gla_w_gate, jnp.zeros((kwid - GLA_GATE_RANK, kwid), gla_w_gate.dtype)], axis=0).astype(BF16)
    yc = _gla(pb, pf, w_gate_p, gla_b_gate.reshape(1, -1), gla_gn_g.reshape(1, -1), bsz, seq)
    yd = _s5(uv, s5_a_re, s5_a_im, s5_log_dt, s5_b_re, s5_b_im, s5_c_re, s5_c_im, s5_d, bsz)
    return _mixout(x2, ya, yb, yc, yd, s5_w_glu.astype(BF16), s5_b_glu.reshape(1, -1),
                   w_mix_out.astype(BF16), ln_g.reshape(1, -1), ln_b.reshape(1, -1))


def kernel(x, mem, w_in, ret_gn_g, gla_w_gate, gla_b_gate, gla_gn_g, s5_a_re, s5_a_im, s5_log_dt, s5_b_re,
           s5_b_im, s5_c_re, s5_c_im, s5_d, s5_w_glu, s5_b_glu, w_mix_out, ln_mix_g, ln_mix_b, w_mem_q,
           w_mem_kv, w_mem_o, ln_mem_g, ln_mem_b, w_ff_gate, w_ff_up, w_ff_down, ln_ff_g, ln_ff_b):
    bsz, seq, dm = x.shape
    assert dm == D_MODEL and mem.shape == (bsz, MEM_LEN, D_MODEL)
    assert seq % DIL_SPAN == 0
    x2 = x.reshape(bsz * seq, dm)
    mem2 = mem.reshape(bsz * MEM_LEN, dm)
    row = lambda t: t.reshape(1, -1)
    for l in range(DEPTH):
        x2 = _hybrid_mixer(x2, bsz, seq, _permute_w_in(w_in[l]), ret_gn_g[l], gla_w_gate[l], gla_b_gate[l],
                           gla_gn_g[l], s5_a_re[l], s5_a_im[l], s5_log_dt[l], s5_b_re[l], s5_b_im[l],
                           s5_c_re[l], s5_c_im[l], s5_d[l], s5_w_glu[l], s5_b_glu[l], w_mix_out[l],
                           ln_mix_g[l], ln_mix_b[l])
        kv = _kvproj(mem2, w_mem_kv[l].astype(BF16))
        x2 = _memattn(x2, w_mem_q[l].astype(BF16), kv, w_mem_o[l].astype(BF16),
                      row(ln_mem_g[l]), row(ln_mem_b[l]), bsz, seq)
        x2 = _ffn(x2, w_ff_gate[l].astype(BF16), w_ff_up[l].astype(BF16), w_ff_down[l].astype(BF16),
                  row(ln_ff_g[l]), row(ln_ff_b[l]))
    return x2.reshape(bsz, seq, dm)
```

```python
import functools
import math

import jax
import jax.numpy as jnp
from jax import lax
from jax.experimental import pallas as pl
from jax.experimental.pallas import tpu as pltpu

F32 = jnp.float32
BF16 = jnp.bfloat16
HIGHEST = lax.Precision.HIGHEST
LANES = 128

D_MODEL = 1024
DEPTH = 2
GROUP_WIDTH = 256
HEADS = 4
HEAD_DIM = 64
DIL_BRANCHES = ((128, 1), (512, 4), (2048, 16))
ATTN_BLOCK = 128
RET_CHUNK = 128
GLA_DK = 32
GLA_GATE_RANK = 16
GLA_TAU = 16.0
GLA_CHUNK = 64
S5_CH = 16
S5_GROUPS = 16
S5_STATE = 64
S5_T = 8
MEM_LEN = 256
MEM_HEADS = 4
MEM_HEAD_DIM = 256
D_FF = 2816
DEEPNORM_ALPHA = (2 * DEPTH) ** 0.25
LN_EPS = 1e-5
NEG_INF = -1e30

PA_COLS = 768
PB_COLS = 1280
PF_COLS = 640
W_IN_COLS = PA_COLS + PB_COLS + GROUP_WIDTH + PF_COLS

VMEM_LIMIT = 56 * 1024 * 1024


def _cparams(n_axes):
    return pltpu.CompilerParams(dimension_semantics=("arbitrary",) * n_axes,
                                vmem_limit_bytes=VMEM_LIMIT)


def _dot(a, b):
    return jnp.dot(a, b, preferred_element_type=F32)


def _dot_nt(a, b):
    return lax.dot_general(a, b, (((1,), (1,)), ((), ())), preferred_element_type=F32)


def _dot_tn(a, b):
    return lax.dot_general(a, b, (((0,), (0,)), ((), ())), preferred_element_type=F32)


def _store_pages(ref, page, val, rows=slice(None)):
    for k in range(val.shape[1] // LANES):
        ref[page + k, rows, :] = val[:, k * LANES:(k + 1) * LANES]


def _load_pages(ref, page, rows, npages=2):
    return jnp.concatenate([ref[page + k, rows, :] for k in range(npages)], axis=1)


def _layer_norm(z, g, b):
    mu = jnp.mean(z, axis=-1, keepdims=True)
    d = z - mu
    var = jnp.mean(d * d, axis=-1, keepdims=True)
    return d * lax.rsqrt(var + LN_EPS) * g + b


def _sigmoid(x):
    return 1.0 / (1.0 + jnp.exp(-x))


def _head_of_lane(shape, head_width):
    return lax.broadcasted_iota(jnp.int32, shape, len(shape) - 1) // head_width


def _stack_heads(t, head_width):
    lane_head = _head_of_lane(t.shape, head_width)
    zero = jnp.zeros_like(t)
    return jnp.concatenate([jnp.where(lane_head == h, t, zero) for h in range(HEADS)], axis=0)


def _select_heads(t4, lane_head):
    n = t4.shape[0] // HEADS
    out = t4[(HEADS - 1) * n:]
    for h in range(HEADS - 2, -1, -1):
        out = jnp.where(lane_head == h, t4[h * n:(h + 1) * n], out)
    return out


def _split_dot(x, w_bf16, terms):
    acc = None
    for _ in range(terms):
        piece = x.astype(BF16)
        part = _dot(piece, w_bf16)
        acc = part if acc is None else acc + part
        x = x - piece.astype(F32)
    return acc


def _head_norm(o, avg_bf16):
    d = o - _split_dot(o, avg_bf16, 2)
    var = _split_dot(d * d, avg_bf16, 2)
    return d * lax.rsqrt(var + LN_EPS)


def _head_avg_matrix(width, head_width):
    r = lax.broadcasted_iota(jnp.int32, (width, width), 0) // head_width
    c = lax.broadcasted_iota(jnp.int32, (width, width), 1) // head_width
    return jnp.where(r == c, 1.0 / head_width, 0.0).astype(BF16)


def _inproj_kernel(x_ref, w_ref, pa_ref, pb_ref, suv_ref, pf_ref, su_scr):
    xb = x_ref[...].astype(BF16)
    step = 256
    for j in range(0, PA_COLS, step):
        _store_pages(pa_ref, j // LANES, _dot(xb, w_ref[:, j:j + step]))
    for j in range(0, PB_COLS, step):
        pb_ref[:, j:j + step] = _dot(xb, w_ref[:, PA_COLS + j:PA_COLS + j + step]).astype(BF16)
    off = PA_COLS + PB_COLS
    _store_pages(su_scr, 0, _dot(xb, w_ref[:, off:off + GROUP_WIDTH]))
    nrow = su_scr.shape[1] // S5_T
    for s in range(S5_T):
        suv_ref[:, s * GROUP_WIDTH:(s + 1) * GROUP_WIDTH] = _load_pages(su_scr, 0, pl.ds(s, nrow, stride=S5_T))
    pf_ref[...] = _dot(xb, w_ref[:, off + GROUP_WIDTH:])


def _inproj(x2, w_p, tm=512):
    n = x2.shape[0]
    return pl.pallas_call(
        _inproj_kernel,
        grid=(n // tm,),
        in_specs=[pl.BlockSpec((tm, D_MODEL), lambda i: (i, 0)),
                  pl.BlockSpec((D_MODEL, W_IN_COLS), lambda i: (0, 0))],
        out_specs=[pl.BlockSpec((PA_COLS // LANES, tm, LANES), lambda i: (0, i, 0)),
                   pl.BlockSpec((tm, PB_COLS), lambda i: (i, 0)),
                   pl.BlockSpec((tm // S5_T, S5_T * GROUP_WIDTH), lambda i: (i, 0)),
                   pl.BlockSpec((tm, PF_COLS), lambda i: (i, 0))],
        out_shape=[jax.ShapeDtypeStruct((PA_COLS // LANES, n, LANES), F32),
                   jax.ShapeDtypeStruct((n, PB_COLS), BF16),
                   jax.ShapeDtypeStruct((n // S5_T, S5_T * GROUP_WIDTH), F32),
                   jax.ShapeDtypeStruct((n, PF_COLS), F32)],
        scratch_shapes=[pltpu.VMEM((GROUP_WIDTH // LANES, tm, LANES), F32)],
        compiler_params=_cparams(1),
        name="inproj",
    )(x2, w_p)


DIL_SPAN = ATTN_BLOCK * max(r for _, r in DIL_BRANCHES)
DIL_UNROLL_BOUNDARY = 4
DIL_UNROLL_INTERIOR = {1: 5, 4: 4}


def _dil_tables(dilation):
    blk = ATTN_BLOCK
    shape = (HEADS * blk, 2 * blk)
    row = lax.broadcasted_iota(jnp.int32, shape, 0)
    ki = lax.broadcasted_iota(jnp.int32, shape, 1)
    head = row // blk
    dist = row % blk + blk - ki
    valid = (dist >= 0) & (dist <= blk)
    slope = jnp.where(head == 0, 2.0 ** -2, jnp.where(head == 1, 2.0 ** -4,
                      jnp.where(head == 2, 2.0 ** -6, 2.0 ** -8))).astype(F32)
    alibi = -slope * (dist * dilation).astype(F32)
    return jnp.where(valid, alibi, NEG_INF), jnp.where(valid & (ki >= blk), alibi, NEG_INF)


def _dil_attend(q, kwin, vwin, tb):
    q4 = _stack_heads(q.astype(BF16), HEAD_DIM)
    s = _dot_nt(q4, kwin.astype(BF16)) * (HEAD_DIM ** -0.5) + tb
    m = jnp.max(s, axis=-1, keepdims=True)
    p = jnp.exp(s - m)
    l = jnp.sum(p, axis=-1, keepdims=True)
    o4 = _dot(p.astype(BF16), vwin.astype(BF16)) * (1.0 / l)
    lse4 = m + jnp.log(l)
    lane_head = _head_of_lane((ATTN_BLOCK, GROUP_WIDTH), HEAD_DIM)
    return _select_heads(o4, lane_head), _select_heads(jnp.broadcast_to(lse4, o4.shape), lane_head)


def _dil_kernel(q_ref, kp_ref, kc_ref, vp_ref, vc_ref, o_ref, tb_ref, ob_ref, lb_ref):
    blk = ATTN_BLOCK
    span = DIL_SPAN

    @pl.when((pl.program_id(0) == 0) & (pl.program_id(1) == 0))
    def _init_tables():
        for bi, (_, r) in enumerate(DIL_BRANCHES):
            t_any, t_first = _dil_tables(r)
            tb_ref[bi, 0] = t_any
            tb_ref[bi, 1] = t_first

    first = jnp.where(pl.program_id(1) == 0, 1, 0)

    def rows(start, size, r):
        if r == 1:
            return pl.ds(pl.multiple_of(start, blk), size)
        return pl.ds(start, size, stride=r)

    for bi, (_, r) in enumerate(DIL_BRANCHES):
        per_sub = span // (blk * r)

        def boundary(c, carry, bi=bi, r=r):
            rq = rows(c, blk, r)
            rp = rows(c + span - blk * r, blk, r)
            kwin = jnp.concatenate([_load_pages(kp_ref, 0, rp), _load_pages(kc_ref, 0, rq)], axis=0)
            vwin = jnp.concatenate([_load_pages(vp_ref, 0, rp), _load_pages(vc_ref, 0, rq)], axis=0)
            o, lse = _dil_attend(_load_pages(q_ref, 0, rq), kwin, vwin, tb_ref[bi, first])
            _store_pages(ob_ref, 2 * bi, o, rq)
            _store_pages(lb_ref, 2 * bi, lse, rq)
            return carry

        lax.fori_loop(0, r, boundary, 0, unroll=min(r, DIL_UNROLL_BOUNDARY))

        if per_sub > 1:
            def interior(idx, carry, bi=bi, r=r):
                c = lax.rem(idx, r)
                j = 1 + lax.div(idx, r)
                rq = rows(c + blk * r * j, blk, r)
                rk = rows(c + blk * r * (j - 1), 2 * blk, r)
                o, lse = _dil_attend(_load_pages(q_ref, 0, rq), _load_pages(kc_ref, 0, rk),
                                     _load_pages(vc_ref, 0, rk), tb_ref[bi, 0])
                _store_pages(ob_ref, 2 * bi, o, rq)
                _store_pages(lb_ref, 2 * bi, lse, rq)
                return carry

            lax.fori_loop(0, r * (per_sub - 1), interior, 0, unroll=DIL_UNROLL_INTERIOR[r])

    def combine(i, carry):
        rr = pl.ds(pl.multiple_of(i * blk, blk), blk)
        l1, l4, l16 = (_load_pages(lb_ref, 2 * bi, rr) for bi in range(3))
        mx = jnp.maximum(jnp.maximum(l1, l4), l16)
        e1, e4, e16 = jnp.exp(l1 - mx), jnp.exp(l4 - mx), jnp.exp(l16 - mx)
        o1, o4, o16 = (_load_pages(ob_ref, 2 * bi, rr) for bi in range(3))
        ya = (e1 * o1 + e4 * o4 + e16 * o16) / (e1 + e4 + e16)
        o_ref[rr, :] = ya.astype(BF16)
        return carry

    lax.fori_loop(0, span // blk, combine, 0)


def _dilated_attention(pa, bsz, seq):
    n = bsz * seq
    span = DIL_SPAN
    per_b = seq // span
    w = GROUP_WIDTH
    pages = w // LANES

    def cur(col):
        return pl.BlockSpec((pages, span, LANES), lambda b, i: (col, b * per_b + i, 0))

    def prev(col):
        return pl.BlockSpec((pages, span, LANES), lambda b, i: (col, b * per_b + jnp.maximum(i - 1, 0), 0))

    nbr = len(DIL_BRANCHES)
    return pl.pallas_call(
        _dil_kernel,
        grid=(bsz, per_b),
        in_specs=[cur(0), prev(1), cur(1), prev(2), cur(2)],
        out_specs=pl.BlockSpec((span, w), lambda b, i: (b * per_b + i, 0)),
        out_shape=jax.ShapeDtypeStruct((n, w), BF16),
        scratch_shapes=[pltpu.VMEM((nbr, 2, HEADS * ATTN_BLOCK, 2 * ATTN_BLOCK), F32),
                        pltpu.VMEM((nbr * pages, span, LANES), F32),
                        pltpu.VMEM((nbr * pages, span, LANES), F32)],
        compiler_params=_cparams(2),
        name="dilated_attention",
    )(pa, pa, pa, pa, pa)


def _ret_kernel(q_ref, k_ref, v_ref, g_ref, gn_ref, o_ref, r_ref, dtab_ref, qd_ref, kd_ref, o_scr, u_scr,
                *, nchunk):
    c = RET_CHUNK
    width = GROUP_WIDTH
    scale = HEAD_DIM ** -0.5
    log_g = [math.log(1.0 - 2.0 ** (-5.0 - h)) for h in range(HEADS)]

    def head_log_g(head):
        return jnp.where(head == 0, log_g[0], jnp.where(head == 1, log_g[1],
                         jnp.where(head == 2, log_g[2], log_g[3]))).astype(F32)

    @pl.when((pl.program_id(0) == 0) & (pl.program_id(1) == 0))
    def _init_tables():
        qry = lax.broadcasted_iota(jnp.int32, (c, HEADS * c), 0)
        col = lax.broadcasted_iota(jnp.int32, (c, HEADS * c), 1)
        rel = (qry - col % c).astype(F32)
        decay = jnp.exp(jnp.maximum(rel, 0.0) * head_log_g(col // c))
        dtab_ref[...] = jnp.where(rel >= 0, decay, 0.0) * scale
        pos = (lax.broadcasted_iota(jnp.int32, (nchunk * c, width), 0) % c).astype(F32)
        lgl = head_log_g(_head_of_lane((nchunk * c, width), HEAD_DIM))
        qd_ref[...] = jnp.exp((pos + 1.0) * lgl)
        kd_ref[...] = jnp.exp((c - 1.0 - pos) * lgl) * scale

    @pl.when(pl.program_id(1) == 0)
    def _reset_state():
        r_ref[...] = jnp.zeros_like(r_ref)

    rr = lax.broadcasted_iota(jnp.int32, (width, width), 0) // HEAD_DIM
    cc = lax.broadcasted_iota(jnp.int32, (width, width), 1) // HEAD_DIM
    same_head = rr == cc
    chunk_decay = jnp.exp(c * head_log_g(_head_of_lane((1, width), HEAD_DIM)))
    q_dec = (q_ref[...].astype(F32) * qd_ref[...]).astype(BF16)
    k_dec = (k_ref[...].astype(F32) * kd_ref[...]).astype(BF16)

    for ci in range(nchunk):
        sl = slice(ci * c, (ci + 1) * c)
        v = v_ref[sl, :]
        scores = _dot_nt(q_ref[sl, :], _stack_heads(k_ref[sl, :], HEAD_DIM)) * dtab_ref[...]
        o_scr[sl, :] = _dot(scores.astype(BF16), _stack_heads(v, HEAD_DIM))
        u_scr[ci] = jnp.where(same_head, _dot_tn(k_dec[sl, :], v), 0.0)

    for ci in range(nchunk):
        sl = slice(ci * c, (ci + 1) * c)
        r_prev = r_ref[...]
        o_scr[sl, :] += _dot(q_dec[sl, :], r_prev.astype(BF16))
        r_ref[...] = r_prev * chunk_decay + u_scr[ci]

    o = _head_norm(o_scr[...], _head_avg_matrix(width, HEAD_DIM)) * gn_ref[...]
    g = g_ref[...]
    o_ref[...] = (o * (g * _sigmoid(g))).astype(BF16)


def _retention(pb, pf, gn, bsz, seq, tq=512):
    n = bsz * seq
    per_b = seq // tq
    w = GROUP_WIDTH
    nchunk = tq // RET_CHUNK

    def tok(col):
        return pl.BlockSpec((tq, w), lambda b, i: (b * per_b + i, col))

    return pl.pallas_call(
        functools.partial(_ret_kernel, nchunk=nchunk),
        grid=(bsz, per_b),
        in_specs=[tok(0), tok(1), tok(2), tok(0), pl.BlockSpec((1, w), lambda b, i: (0, 0))],
        out_specs=tok(0),
        out_shape=jax.ShapeDtypeStruct((n, w), BF16),
        scratch_shapes=[pltpu.VMEM((w, w), F32),
                        pltpu.VMEM((RET_CHUNK, HEADS * RET_CHUNK), F32),
                        pltpu.VMEM((tq, w), F32),
                        pltpu.VMEM((tq, w), F32),
                        pltpu.VMEM((tq, w), F32),
                        pltpu.VMEM((nchunk, w, w), F32)],
        compiler_params=_cparams(2),
        name="retention",
    )(pb, pb, pb, pf, gn)


def _gla_kernel(q_ref, k_ref, v_ref, gl_ref, og_ref, wg_ref, bg_ref, gn_ref, o_ref, rt_ref, cum_ref, o_scr,
                ut_scr, *, nchunk):
    c = GLA_CHUNK
    tq = nchunk * c
    kwid = HEADS * GLA_DK
    vwid = GROUP_WIDTH
    scale = GLA_DK ** -0.5

    @pl.when((pl.program_id(0) == 0) & (pl.program_id(1) == 0))
    def _init_tables():
        i = lax.broadcasted_iota(jnp.int32, (2 * c, 2 * c), 0)
        j = lax.broadcasted_iota(jnp.int32, (2 * c, 2 * c), 1)
        cum_ref[...] = jnp.where((i // c == j // c) & (i >= j), 1.0, 0.0).astype(BF16)

    @pl.when(pl.program_id(1) == 0)
    def _reset_state():
        rt_ref[...] = jnp.zeros_like(rt_ref)

    causal = lax.broadcasted_iota(jnp.int32, (c, HEADS * c), 0) >= (
        lax.broadcasted_iota(jnp.int32, (c, HEADS * c), 1) % c)
    rr = lax.broadcasted_iota(jnp.int32, (vwid, kwid), 0) // HEAD_DIM
    cc = lax.broadcasted_iota(jnp.int32, (vwid, kwid), 1) // GLA_DK
    same_head = rr == cc

    z = _dot(gl_ref[...].astype(BF16), wg_ref[...]) + bg_ref[...]
    log_a = (jnp.minimum(z, 0.0) - jnp.log(1.0 + jnp.exp(-jnp.abs(z)))) * (1.0 / GLA_TAU)
    hi = log_a.astype(BF16)
    rest = log_a - hi.astype(F32)
    mid = rest.astype(BF16)
    lo = (rest - mid.astype(F32)).astype(BF16)
    la3 = jnp.concatenate([hi, mid, lo], axis=1)
    sums = jnp.concatenate([_dot(cum_ref[...], la3[p:p + 2 * c, :]) for p in range(0, tq, 2 * c)], axis=0)
    bcum = sums[:, :kwid] + sums[:, kwid:2 * kwid] + sums[:, 2 * kwid:]
    blast = jnp.concatenate([jnp.broadcast_to(bcum[e - 1:e, :], (c, kwid)) for e in range(c, tq + c, c)],
                            axis=0)
    q = q_ref[...].astype(F32)
    k = k_ref[...].astype(F32)
    q_in = (q * scale * jnp.exp(bcum)).astype(BF16)
    k_out = (k * jnp.exp(-bcum)).astype(BF16)
    k_st = (k * jnp.exp(blast - bcum)).astype(BF16)
    chunk_decay = jnp.exp(blast)

    for ci in range(nchunk):
        sl = slice(ci * c, (ci + 1) * c)
        v = v_ref[sl, :]
        att = _dot_nt(q_in[sl, :], _stack_heads(k_out[sl, :], GLA_DK))
        att = jnp.where(causal, att, 0.0)
        o_scr[sl, :] = _dot(att.astype(BF16), _stack_heads(v, HEAD_DIM))
        ut_scr[ci] = jnp.where(same_head, _dot_tn(v, k_st[sl, :]), 0.0)

    for ci in range(nchunk):
        sl = slice(ci * c, (ci + 1) * c)
        rt_prev = rt_ref[...]
        o_scr[sl, :] += _dot_nt(q_in[sl, :], rt_prev.astype(BF16))
        rt_ref[...] = rt_prev * chunk_decay[ci * c:ci * c + 1, :] + ut_scr[ci]

    o = _head_norm(o_scr[...], _head_avg_matrix(vwid, HEAD_DIM)) * gn_ref[...]
    g = og_ref[...]
    o_ref[...] = (o * (g * _sigmoid(g))).astype(BF16)


def _gla(pb, pf, w_gate_p, b_gate, gn, bsz, seq, tq=512):
    n = bsz * seq
    per_b = seq // tq
    kwid = HEADS * GLA_DK
    nchunk = tq // GLA_CHUNK

    def tok(width, col):
        return pl.BlockSpec((tq, width), lambda b, i: (b * per_b + i, col))

    def const(shape):
        return pl.BlockSpec(shape, lambda b, i: (0, 0))

    return pl.pallas_call(
        functools.partial(_gla_kernel, nchunk=nchunk),
        grid=(bsz, per_b),
        in_specs=[tok(kwid, 6), tok(kwid, 7), tok(GROUP_WIDTH, 4),
                  tok(kwid, 4), tok(GROUP_WIDTH, 1),
                  const((kwid, kwid)), const((1, kwid)), const((1, GROUP_WIDTH))],
        out_specs=tok(GROUP_WIDTH, 0),
        out_shape=jax.ShapeDtypeStruct((n, GROUP_WIDTH), BF16),
        scratch_shapes=[pltpu.VMEM((GROUP_WIDTH, kwid), F32),
                        pltpu.VMEM((2 * GLA_CHUNK, 2 * GLA_CHUNK), BF16),
                        pltpu.VMEM((tq, GROUP_WIDTH), F32),
                        pltpu.VMEM((nchunk, GROUP_WIDTH, kwid), F32)],
        compiler_params=_cparams(2),
        name="gla",
    )(pb, pb, pb, pf, pf, w_gate_p, b_gate, gn)


S5_ROW = S5_T * GROUP_WIDTH
S5_NSTATE = S5_GROUPS * S5_STATE
S5_SW = 2 * S5_NSTATE


def _s5prep_kernel(ar_ref, ai_ref, ld_ref, arc_ref, aic_ref, ldc_ref, bre_ref, bim_ref, cre_ref, cim_ref,
                   bst_ref, bigk_ref, ccr_ref, a8_ref):
    w = GROUP_WIDTH

    def discretise(ar, ai, ld):
        dt = jnp.exp(ld)
        ea = jnp.exp(ar * dt)
        return ea * jnp.cos(ai * dt), ea * jnp.sin(ai * dt)

    def powers(a_re, a_im, n):
        pw = [(jnp.ones_like(a_re), jnp.zeros_like(a_re)), (a_re, a_im)]
        for _ in range(2, n + 1):
            pr, pi = pw[-1]
            pw.append((pr * a_re - pi * a_im, pr * a_im + pi * a_re))
        return pw

    ar, ai = ar_ref[...], ai_ref[...]
    a_re, a_im = discretise(ar, ai, ld_ref[...])
    pw = powers(a_re, a_im, S5_T)
    den = ar * ar + ai * ai
    nr = a_re - 1.0
    f_re = (nr * ar + a_im * ai) / den
    f_im = (a_im * ar - nr * ai) / den
    b_re, b_im = bre_ref[...], bim_ref[...]
    bb_re = f_re * b_re - f_im * b_im
    bb_im = f_re * b_im + f_im * b_re
    c_re, c_im = cre_ref[...], cim_ref[...]
    cmat = jnp.concatenate([c_re, -c_im], axis=0)

    bigk_ref[...] = jnp.zeros_like(bigk_ref)
    for lag in range(S5_T):
        pr, pi = pw[lag]
        lmat = jnp.concatenate([pr * bb_re - pi * bb_im, pr * bb_im + pi * bb_re], axis=1)
        s = S5_T - 1 - lag
        bst_ref[s * w:(s + 1) * w, :] = lmat.astype(BF16)
        m = jnp.dot(lmat, cmat, precision=HIGHEST, preferred_element_type=F32).astype(BF16)
        for s in range(S5_T - lag):
            t = s + lag
            bigk_ref[s * w:(s + 1) * w, t * w:(t + 1) * w] = m

    ac_re, ac_im = discretise(arc_ref[...], aic_ref[...], ldc_ref[...])
    pwc = powers(ac_re, ac_im, S5_T)
    for t in range(S5_T):
        pr, pi = pwc[t + 1]
        ccr_ref[:S5_NSTATE, t * w:(t + 1) * w] = (c_re * pr - c_im * pi).astype(BF16)
        ccr_ref[S5_NSTATE:, t * w:(t + 1) * w] = (-(c_re * pi + c_im * pr)).astype(BF16)

    a8_ref[...] = jnp.concatenate(pw[S5_T], axis=1)


def _s5_prep(a_re, a_im, log_dt, b_re, b_im, c_re, c_im):
    g, p, ch = S5_GROUPS, S5_STATE, S5_CH
    eye = jnp.eye(g, dtype=bool)
    row = lambda t: t.reshape(1, g * p)
    col = lambda t: t.reshape(g * p, 1)
    ld = jnp.broadcast_to(log_dt[:, None], (g, p))

    def expand_b(b):
        t = jnp.transpose(b, (0, 2, 1))[:, :, None, :]
        return jnp.where(eye[:, None, :, None], t, 0.0).reshape(g * ch, g * p)

    def expand_c(c):
        t = jnp.transpose(c, (0, 2, 1))[:, :, None, :]
        return jnp.where(eye[:, None, :, None], t, 0.0).reshape(g * p, g * ch)

    sd = lambda shape, dt: jax.ShapeDtypeStruct(shape, dt)
    return pl.pallas_call(
        _s5prep_kernel,
        out_shape=[sd((S5_ROW, S5_SW), BF16), sd((S5_ROW, S5_ROW), BF16),
                   sd((S5_SW, S5_ROW), BF16), sd((1, S5_SW), F32)],
        compiler_params=pltpu.CompilerParams(vmem_limit_bytes=VMEM_LIMIT),
        name="s5_prep",
    )(row(a_re), row(a_im), row(ld), col(a_re), col(a_im), col(ld),
      expand_b(b_re), expand_b(b_im), expand_c(c_re), expand_c(c_im))


def _s5mm_kernel(u_ref, w_ref, o_ref):
    o_ref[...] = _dot(u_ref[...].astype(BF16), w_ref[...])


def _s5_chunk_matmul(uv, w_cat, tr=512, tn=1024):
    rows = uv.shape[0]
    cols = w_cat.shape[1]
    return pl.pallas_call(
        _s5mm_kernel,
        grid=(cols // tn, rows // tr),
        in_specs=[pl.BlockSpec((tr, S5_ROW), lambda j, i: (i, 0)),
                  pl.BlockSpec((S5_ROW, tn), lambda j, i: (0, j))],
        out_specs=pl.BlockSpec((tr, tn), lambda j, i: (i, j)),
        out_shape=jax.ShapeDtypeStruct((rows, cols), F32),
        compiler_params=_cparams(2),
        name="s5_chunk_matmul",
    )(uv, w_cat)


def _s5scan_kernel(v_ref, y0_ref, u_ref, a8_ref, ccr_ref, d_ref, o_ref, xs_ref, st_ref, *, tr):
    ns = S5_NSTATE

    @pl.when(pl.program_id(1) == 0)
    def _reset_state():
        st_ref[...] = jnp.zeros_like(st_ref)

    a_re = a8_ref[:, :ns]
    a_im = a8_ref[:, ns:]

    def body(r, carry):
        xr, xi = carry
        xs_ref[pl.ds(r, 1), :ns] = xr
        xs_ref[pl.ds(r, 1), ns:] = xi
        vr = v_ref[pl.ds(r, 1), :ns]
        vi = v_ref[pl.ds(r, 1), ns:]
        return (a_re * xr - a_im * xi + vr, a_re * xi + a_im * xr + vi)

    xr, xi = lax.fori_loop(0, tr, body, (st_ref[0:1, :ns], st_ref[0:1, ns:]), unroll=8)
    st_ref[0:1, :ns] = xr
    st_ref[0:1, ns:] = xi

    y = y0_ref[...] + _dot(xs_ref[...].astype(BF16), ccr_ref[...]) + d_ref[...] * u_ref[...]
    y = 0.5 * y * (1.0 + jnp.tanh(math.sqrt(2.0 / math.pi) * (y + 0.044715 * (y * y * y))))
    for s in range(S5_T):
        _store_pages(o_ref, 0, y[:, s * GROUP_WIDTH:(s + 1) * GROUP_WIDTH], pl.ds(s, tr, stride=S5_T))


def _s5_scan(vy, uv, a8, ccr, d_row, bsz, tr=256):
    rows = uv.shape[0]
    per_b = rows // bsz // tr
    pages = GROUP_WIDTH // LANES
    blk = lambda col: pl.BlockSpec((tr, S5_ROW), lambda b, i: (b * per_b + i, col))
    const = lambda shape: pl.BlockSpec(shape, lambda b, i: (0, 0))
    return pl.pallas_call(
        functools.partial(_s5scan_kernel, tr=tr),
        grid=(bsz, per_b),
        in_specs=[blk(0), blk(1), blk(0), const((1, S5_SW)), const((S5_SW, S5_ROW)), const((1, S5_ROW))],
        out_specs=pl.BlockSpec((pages, tr * S5_T, LANES), lambda b, i: (0, b * per_b + i, 0)),
        out_shape=jax.ShapeDtypeStruct((pages, rows * S5_T, LANES), F32),
        scratch_shapes=[pltpu.VMEM((tr, S5_SW), F32), pltpu.VMEM((8, S5_SW), F32)],
        compiler_params=_cparams(2),
        name="s5_scan",
    )(vy, vy, uv, a8, ccr, d_row)


def _s5(uv, a_re, a_im, log_dt, b_re, b_im, c_re, c_im, d, bsz):
    bst, bigk, ccr, a8 = _s5_prep(a_re, a_im, log_dt, b_re, b_im, c_re, c_im)
    vy = _s5_chunk_matmul(uv, jnp.concatenate([bst, bigk], axis=1))
    d_row = jnp.tile(d.reshape(1, GROUP_WIDTH), (1, S5_T))
    return _s5_scan(vy, uv, a8, ccr, d_row, bsz)


def _mixout_kernel(x_ref, ya_ref, yb_ref, yc_ref, yd_ref, wglu_ref, bglu_ref, w_ref, g_ref, b_ref, out_ref):
    yd = _load_pages(yd_ref, 0, slice(None))
    yd = yd * _sigmoid(_dot(yd.astype(BF16), wglu_ref[...]) + bglu_ref[...])
    y = jnp.concatenate([ya_ref[...], yb_ref[...], yc_ref[...], yd.astype(BF16)], axis=1)
    h = _dot(y, w_ref[...])
    out_ref[...] = _layer_norm(DEEPNORM_ALPHA * x_ref[...] + h, g_ref[...], b_ref[...])


def _mixout(x2, ya, yb, yc, yd, wglu, bglu, w, g, b, tm=512):
    n = x2.shape[0]
    w4 = GROUP_WIDTH
    tok = lambda width: pl.BlockSpec((tm, width), lambda i: (i, 0))
    const = lambda shape: pl.BlockSpec(shape, lambda i: (0, 0))
    return pl.pallas_call(
        _mixout_kernel,
        grid=(n // tm,),
        in_specs=[tok(D_MODEL)] + [tok(w4)] * 3 + [pl.BlockSpec((w4 // LANES, tm, LANES), lambda i: (0, i, 0)),
                                                    const((w4, w4)), const((1, w4)),
                                                    const((D_MODEL, D_MODEL)), const((1, D_MODEL)),
                                                    const((1, D_MODEL))],
        out_specs=tok(D_MODEL),
        out_shape=jax.ShapeDtypeStruct((n, D_MODEL), F32),
        compiler_params=_cparams(1),
        name="mix_out",
    )(x2, ya, yb, yc, yd, wglu, bglu, w, g, b)


def _kvproj_kernel(m_ref, w_ref, o_ref):
    mb = m_ref[...].astype(BF16)
    step = 512
    for j in range(0, 2 * D_MODEL, step):
        o_ref[:, j:j + step] = _dot(mb, w_ref[:, j:j + step]).astype(BF16)


def _kvproj(mem2, w_kv, tm=512):
    n = mem2.shape[0]
    return pl.pallas_call(
        _kvproj_kernel,
        grid=(n // tm,),
        in_specs=[pl.BlockSpec((tm, D_MODEL), lambda i: (i, 0)),
                  pl.BlockSpec((D_MODEL, 2 * D_MODEL), lambda i: (0, 0))],
        out_specs=pl.BlockSpec((tm, 2 * D_MODEL), lambda i: (i, 0)),
        out_shape=jax.ShapeDtypeStruct((n, 2 * D_MODEL), BF16),
        compiler_params=_cparams(1),
        name="mem_kv_proj",
    )(mem2, w_kv)


def _memattn_kernel(x_ref, wq_ref, kv_ref, wo_ref, g_ref, b_ref, out_ref, *, parts):
    hd = MEM_HEAD_DIM
    rows = x_ref.shape[0] // parts
    for part in range(parts):
        sl = slice(part * rows, (part + 1) * rows)
        x = x_ref[sl, :]
        q = _dot(x.astype(BF16), wq_ref[...]).astype(BF16)
        outs = []
        for h in range(MEM_HEADS):
            k = kv_ref[:, h * hd:(h + 1) * hd]
            v = kv_ref[:, D_MODEL + h * hd:D_MODEL + (h + 1) * hd]
            s = _dot_nt(q[:, h * hd:(h + 1) * hd], k) * (hd ** -0.5)
            m = jnp.max(s, axis=-1, keepdims=True)
            p = jnp.exp(s - m)
            p = p / jnp.sum(p, axis=-1, keepdims=True)
            outs.append(_dot(p.astype(BF16), v).astype(BF16))
        h_out = _dot(jnp.concatenate(outs, axis=1), wo_ref[...])
        out_ref[sl, :] = _layer_norm(DEEPNORM_ALPHA * x + h_out, g_ref[...], b_ref[...])


def _memattn(x2, wq, kv, wo, g, b, bsz, seq, tm=1024, parts=2):
    per_b = seq // tm
    const = lambda shape: pl.BlockSpec(shape, lambda bi, i: (0, 0), pipeline_mode=pl.Buffered(1))
    tok = pl.BlockSpec((tm, D_MODEL), lambda bi, i: (bi * per_b + i, 0))
    return pl.pallas_call(
        functools.partial(_memattn_kernel, parts=parts),
        grid=(bsz, per_b),
        in_specs=[tok, const((D_MODEL, D_MODEL)),
                  pl.BlockSpec((MEM_LEN, 2 * D_MODEL), lambda bi, i: (bi, 0)),
                  const((D_MODEL, D_MODEL)), const((1, D_MODEL)), const((1, D_MODEL))],
        out_specs=tok,
        out_shape=jax.ShapeDtypeStruct(x2.shape, F32),
        compiler_params=_cparams(2),
        name="mem_attn",
    )(x2, wq, kv, wo, g, b)


FFN_CHUNK = 256


def _ffn_kernel(x_ref, wg_ref, wu_ref, wd_ref, g_ref, b_ref, out_ref, h_ref):
    x = x_ref[...]
    xb = x.astype(BF16)
    for j in range(0, D_FF, FFN_CHUNK):
        gate = _dot(xb, wg_ref[:, j:j + FFN_CHUNK])
        up = _dot(xb, wu_ref[:, j:j + FFN_CHUNK])
        h_ref[:, j:j + FFN_CHUNK] = (gate * _sigmoid(gate) * up).astype(BF16)
    y = _dot(h_ref[...], wd_ref[...])
    out_ref[...] = _layer_norm(DEEPNORM_ALPHA * x + y, g_ref[...], b_ref[...])


def _ffn(x2, wg, wu, wd, g, b, tm=1024):
    n = x2.shape[0]
    tok = pl.BlockSpec((tm, D_MODEL), lambda i: (i, 0))
    const = lambda shape: pl.BlockSpec(shape, lambda i: (0, 0), pipeline_mode=pl.Buffered(1))
    return pl.pallas_call(
        _ffn_kernel,
        grid=(n // tm,),
        in_specs=[tok, const((D_MODEL, D_FF)), const((D_MODEL, D_FF)), const((D_FF, D_MODEL)),
                  const((1, D_MODEL)), const((1, D_MODEL))],
        out_specs=tok,
        out_shape=jax.ShapeDtypeStruct(x2.shape, F32),
        scratch_shapes=[pltpu.VMEM((tm, D_FF), BF16)],
        compiler_params=_cparams(1),
        name="ffn",
    )(x2, wg, wu, wd, g, b)


def _permute_w_in(w):
    c = w.shape[0]
    return jnp.concatenate([
        w[:, 0:1536], w[:, 1792:2304],
        w[:, 2576:2832],
        w[:, 1536:1792], w[:, 2320:2576],
        w[:, 2304:2320], jnp.zeros((c, LANES - GLA_GATE_RANK), w.dtype),
    ], axis=1).astype(BF16)


def _hybrid_mixer(x2, bsz, seq, w_in_p, ret_gn_g, gla_w_gate, gla_b_gate, gla_gn_g, s5_a_re, s5_a_im,
                  s5_log_dt, s5_b_re, s5_b_im, s5_c_re, s5_c_im, s5_d, s5_w_glu, s5_b_glu,
                  w_mix_out, ln_g, ln_b):
    pa, pb, uv, pf = _inproj(x2, w_in_p)
    ya = _dilated_attention(pa, bsz, seq)
    yb = _retention(pb, pf, ret_gn_g.reshape(1, -1), bsz, seq)
    kwid = HEADS * GLA_DK
    w_gate_p = jnp.concatenate(
        [gla_w_gate, jnp.zeros((kwid - GLA_GATE_RANK, kwid), gla_w_gate.dtype)], axis=0).astype(BF16)
    yc = _gla(pb, pf, w_gate_p, gla_b_gate.reshape(1, -1), gla_gn_g.reshape(1, -1), bsz, seq)
    yd = _s5(uv, s5_a_re, s5_a_im, s5_log_dt, s5_b_re, s5_b_im, s5_c_re, s5_c_im, s5_d, bsz)
    return _mixout(x2, ya, yb, yc, yd, s5_w_glu.astype(BF16), s5_b_glu.reshape(1, -1),
                   w_mix_out.astype(BF16), ln_g.reshape(1, -1), ln_b.reshape(1, -1))


def kernel(x, mem, w_in, ret_gn_g, gla_w_gate, gla_b_gate, gla_gn_g, s5_a_re, s5_a_im, s5_log_dt, s5_b_re,
           s5_b_im, s5_c_re, s5_c_im, s5_d, s5_w_glu, s5_b_glu, w_mix_out, ln_mix_g, ln_mix_b, w_mem_q,
           w_mem_kv, w_mem_o, ln_mem_g, ln_mem_b, w_ff_gate, w_ff_up, w_ff_down, ln_ff_g, ln_ff_b):
    bsz, seq, dm = x.shape
    assert dm == D_MODEL and mem.shape == (bsz, MEM_LEN, D_MODEL)
    assert seq % DIL_SPAN == 0
    x2 = x.reshape(bsz * seq, dm)
    mem2 = mem.reshape(bsz * MEM_LEN, dm)
    row = lambda t: t.reshape(1, -1)
    for l in range(DEPTH):
        x2 = _hybrid_mixer(x2, bsz, seq, _permute_w_in(w_in[l]), ret_gn_g[l], gla_w_gate[l], gla_b_gate[l],
                           gla_gn_g[l], s5_a_re[l], s5_a_im[l], s5_log_dt[l], s5_b_re[l], s5_b_im[l],
                           s5_c_re[l], s5_c_im[l], s5_d[l], s5_w_glu[l], s5_b_glu[l], w_mix_out[l],
                           ln_mix_g[l], ln_mix_b[l])
        kv = _kvproj(mem2, w_mem_kv[l].astype(BF16))
        x2 = _memattn(x2, w_mem_q[l].astype(BF16), kv, w_mem_o[l].astype(BF16),
                      row(ln_mem_g[l]), row(ln_mem_b[l]), bsz, seq)
        x2 = _ffn(x2, w_ff_gate[l].astype(BF16), w_ff_up[l].astype(BF16), w_ff_down[l].astype(BF16),
                  row(ln_ff_g[l]), row(ln_ff_b[l]))
    return x2.reshape(bsz, seq, dm)
```

```python
import functools
import math

import jax
import jax.numpy as jnp
from jax import lax
from jax.experimental import pallas as pl
from jax.experimental.pallas import tpu as pltpu

F32 = jnp.float32
BF16 = jnp.bfloat16
HIGHEST = lax.Precision.HIGHEST
LANES = 128

D_MODEL = 1024
DEPTH = 2
GROUP_WIDTH = 256
HEADS = 4
HEAD_DIM = 64
DIL_BRANCHES = ((128, 1), (512, 4), (2048, 16))
ATTN_BLOCK = 128
RET_CHUNK = 128
GLA_DK = 32
GLA_GATE_RANK = 16
GLA_TAU = 16.0
GLA_CHUNK = 64
S5_CH = 16
S5_GROUPS = 16
S5_STATE = 64
S5_T = 8
MEM_LEN = 256
MEM_HEADS = 4
MEM_HEAD_DIM = 256
D_FF = 2816
DEEPNORM_ALPHA = (2 * DEPTH) ** 0.25
LN_EPS = 1e-5
NEG_INF = -1e30
LOG2_E = math.log2(math.e)
LN_2 = math.log(2.0)

PA_COLS = 768
PB_COLS = 1280
PF_COLS = 640
W_IN_COLS = PA_COLS + PB_COLS + GROUP_WIDTH + PF_COLS

VMEM_LIMIT = 56 * 1024 * 1024


def _cparams(n_axes):
    return pltpu.CompilerParams(dimension_semantics=("arbitrary",) * n_axes,
                                vmem_limit_bytes=VMEM_LIMIT)


def _dot(a, b):
    return jnp.dot(a, b, preferred_element_type=F32)


def _dot_nt(a, b):
    return lax.dot_general(a, b, (((1,), (1,)), ((), ())), preferred_element_type=F32)


def _dot_tn(a, b):
    return lax.dot_general(a, b, (((0,), (0,)), ((), ())), preferred_element_type=F32)


def _store_pages(ref, page, val, rows=slice(None)):
    for k in range(val.shape[1] // LANES):
        ref[page + k, rows, :] = val[:, k * LANES:(k + 1) * LANES]


def _load_pages(ref, page, rows, npages=2):
    return jnp.concatenate([ref[page + k, rows, :] for k in range(npages)], axis=1)


def _layer_norm(z, g, b):
    mu = jnp.mean(z, axis=-1, keepdims=True)
    d = z - mu
    var = jnp.mean(d * d, axis=-1, keepdims=True)
    return d * lax.rsqrt(var + LN_EPS) * g + b


def _sigmoid(x):
    return 1.0 / (1.0 + jnp.exp(-x))


def _head_of_lane(shape, head_width):
    return lax.broadcasted_iota(jnp.int32, shape, len(shape) - 1) // head_width


def _stack_heads(t, head_width):
    lane_head = _head_of_lane(t.shape, head_width)
    zero = jnp.zeros_like(t)
    return jnp.concatenate([jnp.where(lane_head == h, t, zero) for h in range(HEADS)], axis=0)


def _select_heads(t4, lane_head):
    n = t4.shape[0] // HEADS
    out = t4[(HEADS - 1) * n:]
    for h in range(HEADS - 2, -1, -1):
        out = jnp.where(lane_head == h, t4[h * n:(h + 1) * n], out)
    return out


def _split_dot(x, w_bf16, terms):
    acc = None
    for _ in range(terms):
        piece = x.astype(BF16)
        part = _dot(piece, w_bf16)
        acc = part if acc is None else acc + part
        x = x - piece.astype(F32)
    return acc


def _head_norm(o, avg_bf16):
    d = o - _split_dot(o, avg_bf16, 2)
    var = _split_dot(d * d, avg_bf16, 2)
    return d * lax.rsqrt(var + LN_EPS)


def _head_avg_matrix(width, head_width):
    r = lax.broadcasted_iota(jnp.int32, (width, width), 0) // head_width
    c = lax.broadcasted_iota(jnp.int32, (width, width), 1) // head_width
    return jnp.where(r == c, 1.0 / head_width, 0.0).astype(BF16)


def _inproj_kernel(x_ref, w_ref, pa_ref, pb_ref, suv_ref, pf_ref, su_scr):
    xb = x_ref[...].astype(BF16)
    step = 256
    for j in range(0, PA_COLS, step):
        _store_pages(pa_ref, j // LANES, _dot(xb, w_ref[:, j:j + step]))
    for j in range(0, PB_COLS, step):
        pb_ref[:, j:j + step] = _dot(xb, w_ref[:, PA_COLS + j:PA_COLS + j + step]).astype(BF16)
    off = PA_COLS + PB_COLS
    _store_pages(su_scr, 0, _dot(xb, w_ref[:, off:off + GROUP_WIDTH]))
    nrow = su_scr.shape[1] // S5_T
    for s in range(S5_T):
        suv_ref[:, s * GROUP_WIDTH:(s + 1) * GROUP_WIDTH] = _load_pages(su_scr, 0, pl.ds(s, nrow, stride=S5_T))
    pf_ref[...] = _dot(xb, w_ref[:, off + GROUP_WIDTH:])


def _inproj(x2, w_p, tm=512):
    n = x2.shape[0]
    return pl.pallas_call(
        _inproj_kernel,
        grid=(n // tm,),
        in_specs=[pl.BlockSpec((tm, D_MODEL), lambda i: (i, 0)),
                  pl.BlockSpec((D_MODEL, W_IN_COLS), lambda i: (0, 0))],
        out_specs=[pl.BlockSpec((PA_COLS // LANES, tm, LANES), lambda i: (0, i, 0)),
                   pl.BlockSpec((tm, PB_COLS), lambda i: (i, 0)),
                   pl.BlockSpec((tm // S5_T, S5_T * GROUP_WIDTH), lambda i: (i, 0)),
                   pl.BlockSpec((tm, PF_COLS), lambda i: (i, 0))],
        out_shape=[jax.ShapeDtypeStruct((PA_COLS // LANES, n, LANES), F32),
                   jax.ShapeDtypeStruct((n, PB_COLS), BF16),
                   jax.ShapeDtypeStruct((n // S5_T, S5_T * GROUP_WIDTH), F32),
                   jax.ShapeDtypeStruct((n, PF_COLS), F32)],
        scratch_shapes=[pltpu.VMEM((GROUP_WIDTH // LANES, tm, LANES), F32)],
        compiler_params=_cparams(1),
        name="inproj",
    )(x2, w_p)


DIL_SPAN = ATTN_BLOCK * max(r for _, r in DIL_BRANCHES)
DIL_GROUP = 2
DIL_GROUP_INTERIOR = {1: 5, 4: 3}


def _dil_tables(dilation):
    blk = ATTN_BLOCK
    shape = (HEADS * blk, 2 * blk)
    row = lax.broadcasted_iota(jnp.int32, shape, 0)
    ki = lax.broadcasted_iota(jnp.int32, shape, 1)
    head = row // blk
    dist = row % blk + blk - ki
    valid = (dist >= 0) & (dist <= blk)
    slope = jnp.where(head == 0, 2.0 ** -2, jnp.where(head == 1, 2.0 ** -4,
                      jnp.where(head == 2, 2.0 ** -6, 2.0 ** -8))).astype(F32)
    alibi = -slope * (dist * dilation).astype(F32) * LOG2_E
    return jnp.where(valid, alibi, NEG_INF), jnp.where(valid & (ki >= blk), alibi, NEG_INF)


def _dil_attend(qs, kwins, vwins, tbs):
    scale = HEAD_DIM ** -0.5 * LOG2_E
    ss = [_dot_nt(_stack_heads((q * scale).astype(BF16), HEAD_DIM), kwin.astype(BF16))
          for q, kwin in zip(qs, kwins)]
    ms, ps, ls = [], [], []
    for s, tb in zip(ss, tbs):
        s = s + tb
        m = jnp.max(s, axis=-1, keepdims=True)
        p = jnp.exp2(s - m)
        ms.append(m)
        ls.append(jnp.sum(p, axis=-1, keepdims=True))
        ps.append(p.astype(BF16))
    o4s = [_dot(p, vwin.astype(BF16)) for p, vwin in zip(ps, vwins)]
    lane_head = _head_of_lane((ATTN_BLOCK, GROUP_WIDTH), HEAD_DIM)
    outs = []
    for o4, m, l in zip(o4s, ms, ls):
        lse4 = jnp.broadcast_to((m + jnp.log2(l)) * LN_2, o4.shape)
        outs.append((_select_heads(o4 * (1.0 / l), lane_head), _select_heads(lse4, lane_head)))
    return outs


def _dil_kernel(q_ref, kp_ref, kc_ref, vp_ref, vc_ref, o_ref, tb_ref, ob_ref, lb_ref):
    blk = ATTN_BLOCK
    span = DIL_SPAN

    @pl.when((pl.program_id(0) == 0) & (pl.program_id(1) == 0))
    def _init_tables():
        for bi, (_, r) in enumerate(DIL_BRANCHES):
            t_any, t_first = _dil_tables(r)
            tb_ref[bi, 0] = t_any
            tb_ref[bi, 1] = t_first

    first = jnp.where(pl.program_id(1) == 0, 1, 0)

    def rows(start, size, r):
        if r == 1:
            return pl.ds(pl.multiple_of(start, blk), size)
        return pl.ds(start, size, stride=r)

    def store(bi, rqs, outs):
        for rq, (o, lse) in zip(rqs, outs):
            _store_pages(ob_ref, 2 * bi, o, rq)
            _store_pages(lb_ref, 2 * bi, lse, rq)

    for bi, (_, r) in enumerate(DIL_BRANCHES):
        per_sub = span // (blk * r)
        group = min(r, DIL_GROUP)

        def boundary(g, carry, bi=bi, r=r, group=group):
            rqs = [rows(g * group + u, blk, r) for u in range(group)]
            rps = [rows(g * group + u + span - blk * r, blk, r) for u in range(group)]
            kwins = [jnp.concatenate([_load_pages(kp_ref, 0, rp), _load_pages(kc_ref, 0, rq)], axis=0)
                     for rp, rq in zip(rps, rqs)]
            vwins = [jnp.concatenate([_load_pages(vp_ref, 0, rp), _load_pages(vc_ref, 0, rq)], axis=0)
                     for rp, rq in zip(rps, rqs)]
            tb = tb_ref[bi, first]
            store(bi, rqs, _dil_attend([_load_pages(q_ref, 0, rq) for rq in rqs], kwins, vwins, [tb] * group))
            return carry

        lax.fori_loop(0, r // group, boundary, 0)

        if per_sub > 1:
            group = DIL_GROUP_INTERIOR[r]

            def interior(g, carry, bi=bi, r=r, group=group):
                rqs, rks = [], []
                for u in range(group):
                    idx = g * group + u
                    c = lax.rem(idx, r)
                    j = 1 + lax.div(idx, r)
                    rqs.append(rows(c + blk * r * j, blk, r))
                    rks.append(rows(c + blk * r * (j - 1), 2 * blk, r))
                tb = tb_ref[bi, 0]
                store(bi, rqs, _dil_attend([_load_pages(q_ref, 0, rq) for rq in rqs],
                                           [_load_pages(kc_ref, 0, rk) for rk in rks],
                                           [_load_pages(vc_ref, 0, rk) for rk in rks], [tb] * group))
                return carry

            lax.fori_loop(0, r * (per_sub - 1) // group, interior, 0)

    def combine(i, carry):
        rr = pl.ds(pl.multiple_of(i * blk, blk), blk)
        l1, l4, l16 = (_load_pages(lb_ref, 2 * bi, rr) for bi in range(3))
        mx = jnp.maximum(jnp.maximum(l1, l4), l16)
        e1, e4, e16 = jnp.exp(l1 - mx), jnp.exp(l4 - mx), jnp.exp(l16 - mx)
        o1, o4, o16 = (_load_pages(ob_ref, 2 * bi, rr) for bi in range(3))
        ya = (e1 * o1 + e4 * o4 + e16 * o16) / (e1 + e4 + e16)
        o_ref[rr, :] = ya.astype(BF16)
        return carry

    lax.fori_loop(0, span // blk, combine, 0)


def _dilated_attention(pa, bsz, seq):
    n = bsz * seq
    span = DIL_SPAN
    per_b = seq // span
    w = GROUP_WIDTH
    pages = w // LANES

    def cur(col):
        return pl.BlockSpec((pages, span, LANES), lambda b, i: (col, b * per_b + i, 0))

    def prev(col):
        return pl.BlockSpec((pages, span, LANES), lambda b, i: (col, b * per_b + jnp.maximum(i - 1, 0), 0))

    nbr = len(DIL_BRANCHES)
    return pl.pallas_call(
        _dil_kernel,
        grid=(bsz, per_b),
        in_specs=[cur(0), prev(1), cur(1), prev(2), cur(2)],
        out_specs=pl.BlockSpec((span, w), lambda b, i: (b * per_b + i, 0)),
        out_shape=jax.ShapeDtypeStruct((n, w), BF16),
        scratch_shapes=[pltpu.VMEM((nbr, 2, HEADS * ATTN_BLOCK, 2 * ATTN_BLOCK), F32),
                        pltpu.VMEM((nbr * pages, span, LANES), F32),
                        pltpu.VMEM((nbr * pages, span, LANES), F32)],
        compiler_params=_cparams(2),
        name="dilated_attention",
    )(pa, pa, pa, pa, pa)


def _ret_kernel(q_ref, k_ref, v_ref, g_ref, gn_ref, o_ref, r_ref, dtab_ref, qd_ref, kd_ref, o_scr, *, nchunk):
    c = RET_CHUNK
    width = GROUP_WIDTH
    scale = HEAD_DIM ** -0.5
    log_g = [math.log(1.0 - 2.0 ** (-5.0 - h)) for h in range(HEADS)]

    def head_log_g(head):
        return jnp.where(head == 0, log_g[0], jnp.where(head == 1, log_g[1],
                         jnp.where(head == 2, log_g[2], log_g[3]))).astype(F32)

    @pl.when((pl.program_id(0) == 0) & (pl.program_id(1) == 0))
    def _init_tables():
        qry = lax.broadcasted_iota(jnp.int32, (c, HEADS * c), 0)
        col = lax.broadcasted_iota(jnp.int32, (c, HEADS * c), 1)
        rel = (qry - col % c).astype(F32)
        decay = jnp.exp(jnp.maximum(rel, 0.0) * head_log_g(col // c))
        dtab_ref[...] = jnp.where(rel >= 0, decay, 0.0) * scale
        pos = (lax.broadcasted_iota(jnp.int32, (nchunk * c, width), 0) % c).astype(F32)
        lgl = head_log_g(_head_of_lane((nchunk * c, width), HEAD_DIM))
        qd_ref[...] = jnp.exp((pos + 1.0) * lgl)
        kd_ref[...] = jnp.exp((c - 1.0 - pos) * lgl) * scale

    @pl.when(pl.program_id(1) == 0)
    def _reset_state():
        r_ref[...] = jnp.zeros_like(r_ref)

    rr = lax.broadcasted_iota(jnp.int32, (width, width), 0) // HEAD_DIM
    cc = lax.broadcasted_iota(jnp.int32, (width, width), 1) // HEAD_DIM
    same_head = rr == cc
    chunk_decay = jnp.exp(c * head_log_g(_head_of_lane((1, width), HEAD_DIM)))
    q_dec = (q_ref[...].astype(F32) * qd_ref[...]).astype(BF16)
    k_dec = (k_ref[...].astype(F32) * kd_ref[...]).astype(BF16)

    sls = [slice(ci * c, (ci + 1) * c) for ci in range(nchunk)]
    scores = [_dot_nt(q_ref[sl, :], _stack_heads(k_ref[sl, :], HEAD_DIM)) for sl in sls]
    probs = [(sc * dtab_ref[...]).astype(BF16) for sc in scores]
    inner = [_dot(pr, _stack_heads(v_ref[sl, :], HEAD_DIM)) for pr, sl in zip(probs, sls)]
    upd = [_dot_tn(k_dec[sl, :], v_ref[sl, :]) for sl in sls]
    states = [r_ref[...]]
    for ci in range(nchunk):
        states.append(states[-1] * chunk_decay + jnp.where(same_head, upd[ci], 0.0))
    r_ref[...] = states[nchunk]
    cross = [_dot(q_dec[sl, :], states[ci].astype(BF16)) for ci, sl in enumerate(sls)]
    for ci, sl in enumerate(sls):
        o_scr[sl, :] = inner[ci] + cross[ci]

    o = _head_norm(o_scr[...], _head_avg_matrix(width, HEAD_DIM)) * gn_ref[...]
    g = g_ref[...]
    o_ref[...] = (o * (g * _sigmoid(g))).astype(BF16)


def _retention(pb, pf, gn, bsz, seq, tq=512):
    n = bsz * seq
    per_b = seq // tq
    w = GROUP_WIDTH
    nchunk = tq // RET_CHUNK

    def tok(col):
        return pl.BlockSpec((tq, w), lambda b, i: (b * per_b + i, col))

    return pl.pallas_call(
        functools.partial(_ret_kernel, nchunk=nchunk),
        grid=(bsz, per_b),
        in_specs=[tok(0), tok(1), tok(2), tok(0), pl.BlockSpec((1, w), lambda b, i: (0, 0))],
        out_specs=tok(0),
        out_shape=jax.ShapeDtypeStruct((n, w), BF16),
        scratch_shapes=[pltpu.VMEM((w, w), F32),
                        pltpu.VMEM((RET_CHUNK, HEADS * RET_CHUNK), F32),
                        pltpu.VMEM((tq, w), F32),
                        pltpu.VMEM((tq, w), F32),
                        pltpu.VMEM((tq, w), F32)],
        compiler_params=_cparams(2),
        name="retention",
    )(pb, pb, pb, pf, gn)


def _gla_kernel(q_ref, k_ref, v_ref, gl_ref, og_ref, wg_ref, bg_ref, gn_ref, o_ref, rt_ref, cum_ref, o_scr,
                *, nchunk):
    c = GLA_CHUNK
    tq = nchunk * c
    kwid = HEADS * GLA_DK
    vwid = GROUP_WIDTH
    scale = GLA_DK ** -0.5

    @pl.when((pl.program_id(0) == 0) & (pl.program_id(1) == 0))
    def _init_tables():
        i = lax.broadcasted_iota(jnp.int32, (2 * c, 2 * c), 0)
        j = lax.broadcasted_iota(jnp.int32, (2 * c, 2 * c), 1)
        cum_ref[...] = jnp.where((i // c == j // c) & (i >= j), 1.0, 0.0).astype(BF16)

    @pl.when(pl.program_id(1) == 0)
    def _reset_state():
        rt_ref[...] = jnp.zeros_like(rt_ref)

    causal = lax.broadcasted_iota(jnp.int32, (c, HEADS * c), 0) >= (
        lax.broadcasted_iota(jnp.int32, (c, HEADS * c), 1) % c)
    rr = lax.broadcasted_iota(jnp.int32, (vwid, kwid), 0) // HEAD_DIM
    cc = lax.broadcasted_iota(jnp.int32, (vwid, kwid), 1) // GLA_DK
    same_head = rr == cc

    z = _dot(gl_ref[...].astype(BF16), wg_ref[...]) + bg_ref[...]
    log_a = (jnp.minimum(z, 0.0) - jnp.log(1.0 + jnp.exp(-jnp.abs(z)))) * (1.0 / GLA_TAU)
    hi = log_a.astype(BF16)
    rest = log_a - hi.astype(F32)
    mid = rest.astype(BF16)
    lo = (rest - mid.astype(F32)).astype(BF16)
    la3 = jnp.concatenate([hi, mid, lo], axis=1)
    sums = jnp.concatenate([_dot(cum_ref[...], la3[p:p + 2 * c, :]) for p in range(0, tq, 2 * c)], axis=0)
    bcum = sums[:, :kwid] + sums[:, kwid:2 * kwid] + sums[:, 2 * kwid:]
    blast = jnp.concatenate([jnp.broadcast_to(bcum[e - 1:e, :], (c, kwid)) for e in range(c, tq + c, c)],
                            axis=0)
    q = q_ref[...].astype(F32)
    k = k_ref[...].astype(F32)
    q_in = (q * scale * jnp.exp(bcum)).astype(BF16)
    k_out = (k * jnp.exp(-bcum)).astype(BF16)
    k_st = (k * jnp.exp(blast - bcum)).astype(BF16)
    chunk_decay = jnp.exp(blast)

    sls = [slice(ci * c, (ci + 1) * c) for ci in range(nchunk)]
    att = [_dot_nt(q_in[sl, :], _stack_heads(k_out[sl, :], GLA_DK)) for sl in sls]
    att = [jnp.where(causal, a, 0.0).astype(BF16) for a in att]
    intra = [_dot(a, _stack_heads(v_ref[sl, :], HEAD_DIM)) for a, sl in zip(att, sls)]
    upd = [_dot_tn(v_ref[sl, :], k_st[sl, :]) for sl in sls]
    states = [rt_ref[...]]
    for ci in range(nchunk):
        states.append(states[-1] * chunk_decay[ci * c:ci * c + 1, :] + jnp.where(same_head, upd[ci], 0.0))
    rt_ref[...] = states[nchunk]
    cross = [_dot_nt(q_in[sl, :], states[ci].astype(BF16)) for ci, sl in enumerate(sls)]
    for ci, sl in enumerate(sls):
        o_scr[sl, :] = intra[ci] + cross[ci]

    o = _head_norm(o_scr[...], _head_avg_matrix(vwid, HEAD_DIM)) * gn_ref[...]
    g = og_ref[...]
    o_ref[...] = (o * (g * _sigmoid(g))).astype(BF16)


def _gla(pb, pf, w_gate_p, b_gate, gn, bsz, seq, tq=512):
    n = bsz * seq
    per_b = seq // tq
    kwid = HEADS * GLA_DK
    nchunk = tq // GLA_CHUNK

    def tok(width, col):
        return pl.BlockSpec((tq, width), lambda b, i: (b * per_b + i, col))

    def const(shape):
        return pl.BlockSpec(shape, lambda b, i: (0, 0))

    return pl.pallas_call(
        functools.partial(_gla_kernel, nchunk=nchunk),
        grid=(bsz, per_b),
        in_specs=[tok(kwid, 6), tok(kwid, 7), tok(GROUP_WIDTH, 4),
                  tok(kwid, 4), tok(GROUP_WIDTH, 1),
                  const((kwid, kwid)), const((1, kwid)), const((1, GROUP_WIDTH))],
        out_specs=tok(GROUP_WIDTH, 0),
        out_shape=jax.ShapeDtypeStruct((n, GROUP_WIDTH), BF16),
        scratch_shapes=[pltpu.VMEM((GROUP_WIDTH, kwid), F32),
                        pltpu.VMEM((2 * GLA_CHUNK, 2 * GLA_CHUNK), BF16),
                        pltpu.VMEM((tq, GROUP_WIDTH), F32)],
        compiler_params=_cparams(2),
        name="gla",
    )(pb, pb, pb, pf, pf, w_gate_p, b_gate, gn)


S5_ROW = S5_T * GROUP_WIDTH
S5_NSTATE = S5_GROUPS * S5_STATE
S5_SW = 2 * S5_NSTATE


def _s5prep_kernel(ar_ref, ai_ref, ld_ref, arc_ref, aic_ref, ldc_ref, bre_ref, bim_ref, cre_ref, cim_ref,
                   bst_ref, bigk_ref, ccr_ref, a8_ref):
    w = GROUP_WIDTH

    def discretise(ar, ai, ld):
        dt = jnp.exp(ld)
        ea = jnp.exp(ar * dt)
        return ea * jnp.cos(ai * dt), ea * jnp.sin(ai * dt)

    def powers(a_re, a_im, n):
        pw = [(jnp.ones_like(a_re), jnp.zeros_like(a_re)), (a_re, a_im)]
        for _ in range(2, n + 1):
            pr, pi = pw[-1]
            pw.append((pr * a_re - pi * a_im, pr * a_im + pi * a_re))
        return pw

    ar, ai = ar_ref[...], ai_ref[...]
    a_re, a_im = discretise(ar, ai, ld_ref[...])
    pw = powers(a_re, a_im, S5_T)
    den = ar * ar + ai * ai
    nr = a_re - 1.0
    f_re = (nr * ar + a_im * ai) / den
    f_im = (a_im * ar - nr * ai) / den
    b_re, b_im = bre_ref[...], bim_ref[...]
    bb_re = f_re * b_re - f_im * b_im
    bb_im = f_re * b_im + f_im * b_re
    c_re, c_im = cre_ref[...], cim_ref[...]
    cmat = jnp.concatenate([c_re, -c_im], axis=0)

    bigk_ref[...] = jnp.zeros_like(bigk_ref)
    for lag in range(S5_T):
        pr, pi = pw[lag]
        lmat = jnp.concatenate([pr * bb_re - pi * bb_im, pr * bb_im + pi * bb_re], axis=1)
        s = S5_T - 1 - lag
        bst_ref[s * w:(s + 1) * w, :] = lmat.astype(BF16)
        m = jnp.dot(lmat, cmat, precision=HIGHEST, preferred_element_type=F32).astype(BF16)
        for s in range(S5_T - lag):
            t = s + lag
            bigk_ref[s * w:(s + 1) * w, t * w:(t + 1) * w] = m

    ac_re, ac_im = discretise(arc_ref[...], aic_ref[...], ldc_ref[...])
    pwc = powers(ac_re, ac_im, S5_T)
    for t in range(S5_T):
        pr, pi = pwc[t + 1]
        ccr_ref[:S5_NSTATE, t * w:(t + 1) * w] = (c_re * pr - c_im * pi).astype(BF16)
        ccr_ref[S5_NSTATE:, t * w:(t + 1) * w] = (-(c_re * pi + c_im * pr)).astype(BF16)

    a8_ref[...] = jnp.concatenate(pw[S5_T], axis=1)


def _s5_prep(a_re, a_im, log_dt, b_re, b_im, c_re, c_im):
    g, p, ch = S5_GROUPS, S5_STATE, S5_CH
    eye = jnp.eye(g, dtype=bool)
    row = lambda t: t.reshape(1, g * p)
    col = lambda t: t.reshape(g * p, 1)
    ld = jnp.broadcast_to(log_dt[:, None], (g, p))

    def expand_b(b):
        t = jnp.transpose(b, (0, 2, 1))[:, :, None, :]
        return jnp.where(eye[:, None, :, None], t, 0.0).reshape(g * ch, g * p)

    def expand_c(c):
        t = jnp.transpose(c, (0, 2, 1))[:, :, None, :]
        return jnp.where(eye[:, None, :, None], t, 0.0).reshape(g * p, g * ch)

    sd = lambda shape, dt: jax.ShapeDtypeStruct(shape, dt)
    return pl.pallas_call(
        _s5prep_kernel,
        out_shape=[sd((S5_ROW, S5_SW), BF16), sd((S5_ROW, S5_ROW), BF16),
                   sd((S5_SW, S5_ROW), BF16), sd((1, S5_SW), F32)],
        compiler_params=pltpu.CompilerParams(vmem_limit_bytes=VMEM_LIMIT),
        name="s5_prep",
    )(row(a_re), row(a_im), row(ld), col(a_re), col(a_im), col(ld),
      expand_b(b_re), expand_b(b_im), expand_c(c_re), expand_c(c_im))


def _s5_kernel(u_ref, bst_ref, bigk_ref, ccr_ref, a8_ref, d_ref, o_ref, v_scr, xs_scr, st_ref, *, tr):
    ns = S5_NSTATE
    w = GROUP_WIDTH

    @pl.when(pl.program_id(1) == 0)
    def _reset_state():
        st_ref[...] = jnp.zeros_like(st_ref)

    u = u_ref[...]
    ub = u.astype(BF16)
    v_scr[...] = _dot(ub, bst_ref[...])

    a_re = a8_ref[:, :ns]
    a_im = a8_ref[:, ns:]

    def body(r, carry):
        xr, xi = carry
        xs_scr[pl.ds(r, 1), :ns] = xr
        xs_scr[pl.ds(r, 1), ns:] = xi
        vr = v_scr[pl.ds(r, 1), :ns]
        vi = v_scr[pl.ds(r, 1), ns:]
        return (a_re * xr - a_im * xi + vr, a_re * xi + a_im * xr + vi)

    xr, xi = lax.fori_loop(0, tr, body, (st_ref[0:1, :ns], st_ref[0:1, ns:]), unroll=8)
    st_ref[0:1, :ns] = xr
    st_ref[0:1, ns:] = xi

    xb = xs_scr[...].astype(BF16)
    d_row = d_ref[...]
    for t in range(S5_T):
        cols = slice(t * w, (t + 1) * w)
        y = (_dot(ub[:, :(t + 1) * w], bigk_ref[:(t + 1) * w, cols]) + _dot(xb, ccr_ref[:, cols])
             + d_row * u[:, cols])
        y = 0.5 * y * (1.0 + jnp.tanh(math.sqrt(2.0 / math.pi) * (y + 0.044715 * (y * y * y))))
        _store_pages(o_ref, 0, y, pl.ds(t, tr, stride=S5_T))


def _s5(uv, a_re, a_im, log_dt, b_re, b_im, c_re, c_im, d, bsz, tr=256):
    bst, bigk, ccr, a8 = _s5_prep(a_re, a_im, log_dt, b_re, b_im, c_re, c_im)
    rows = uv.shape[0]
    per_b = rows // bsz // tr
    pages = GROUP_WIDTH // LANES
    const = lambda shape: pl.BlockSpec(shape, lambda b, i: (0, 0), pipeline_mode=pl.Buffered(1))
    return pl.pallas_call(
        functools.partial(_s5_kernel, tr=tr),
        grid=(bsz, per_b),
        in_specs=[pl.BlockSpec((tr, S5_ROW), lambda b, i: (b * per_b + i, 0)),
                  const((S5_ROW, S5_SW)), const((S5_ROW, S5_ROW)), const((S5_SW, S5_ROW)),
                  const((1, S5_SW)), const((1, GROUP_WIDTH))],
        out_specs=pl.BlockSpec((pages, tr * S5_T, LANES), lambda b, i: (0, b * per_b + i, 0)),
        out_shape=jax.ShapeDtypeStruct((pages, rows * S5_T, LANES), F32),
        scratch_shapes=[pltpu.VMEM((tr, S5_SW), F32), pltpu.VMEM((tr, S5_SW), F32), pltpu.VMEM((8, S5_SW), F32)],
        compiler_params=_cparams(2),
        name="s5",
    )(uv, bst, bigk, ccr, a8, d.reshape(1, GROUP_WIDTH))


def _kvproj_kernel(m_ref, w_ref, o_ref):
    mb = m_ref[...].astype(BF16)
    step = 512
    for j in range(0, 2 * D_MODEL, step):
        o_ref[:, j:j + step] = _dot(mb, w_ref[:, j:j + step]).astype(BF16)


def _kvproj(mem2, w_kv, tm=512):
    n = mem2.shape[0]
    return pl.pallas_call(
        _kvproj_kernel,
        grid=(n // tm,),
        in_specs=[pl.BlockSpec((tm, D_MODEL), lambda i: (i, 0)),
                  pl.BlockSpec((D_MODEL, 2 * D_MODEL), lambda i: (0, 0))],
        out_specs=pl.BlockSpec((tm, 2 * D_MODEL), lambda i: (i, 0)),
        out_shape=jax.ShapeDtypeStruct((n, 2 * D_MODEL), BF16),
        compiler_params=_cparams(1),
        name="mem_kv_proj",
    )(mem2, w_kv)


def _mixmem_kernel(x_ref, ya_ref, yb_ref, yc_ref, yd_ref, wglu_ref, bglu_ref, wmix_ref, gmix_ref, bmix_ref,
                   wq_ref, kv_ref, wo_ref, gmem_ref, bmem_ref, out_ref, *, parts):
    hd = MEM_HEAD_DIM
    rows = x_ref.shape[0] // parts
    sls = [slice(part * rows, (part + 1) * rows) for part in range(parts)]
    yds = [_load_pages(yd_ref, 0, sl) for sl in sls]
    gates = [_dot(yd.astype(BF16), wglu_ref[...]) for yd in yds]
    yds = [yd * _sigmoid(g + bglu_ref[...]) for yd, g in zip(yds, gates)]
    ys = [jnp.concatenate([ya_ref[sl, :], yb_ref[sl, :], yc_ref[sl, :], yd.astype(BF16)], axis=1)
          for sl, yd in zip(sls, yds)]
    hs = [_dot(y, wmix_ref[...]) for y in ys]
    xs = [_layer_norm(DEEPNORM_ALPHA * x_ref[sl, :] + h, gmix_ref[...], bmix_ref[...]) for sl, h in zip(sls, hs)]
    qs = [_dot(x.astype(BF16), wq_ref[...]).astype(BF16) for x in xs]
    outs = [[] for _ in sls]
    for h in range(MEM_HEADS):
        k = kv_ref[:, h * hd:(h + 1) * hd]
        v = kv_ref[:, D_MODEL + h * hd:D_MODEL + (h + 1) * hd]
        ss = [_dot_nt(q[:, h * hd:(h + 1) * hd], k) * (hd ** -0.5) for q in qs]
        ps = []
        for s in ss:
            p = jnp.exp(s - jnp.max(s, axis=-1, keepdims=True))
            ps.append((p / jnp.sum(p, axis=-1, keepdims=True)).astype(BF16))
        for o, p in zip(outs, ps):
            o.append(_dot(p, v).astype(BF16))
    hs = [_dot(jnp.concatenate(o, axis=1), wo_ref[...]) for o in outs]
    for sl, x, h in zip(sls, xs, hs):
        out_ref[sl, :] = _layer_norm(DEEPNORM_ALPHA * x + h, gmem_ref[...], bmem_ref[...])


def _mixmem(x2, ya, yb, yc, yd, wglu, bglu, wmix, gmix, bmix, wq, kv, wo, gmem, bmem, bsz, seq, tm=1024, parts=4):
    per_b = seq // tm
    w4 = GROUP_WIDTH
    const = lambda shape: pl.BlockSpec(shape, lambda bi, i: (0, 0), pipeline_mode=pl.Buffered(1))
    tok = lambda width: pl.BlockSpec((tm, width), lambda bi, i: (bi * per_b + i, 0))
    return pl.pallas_call(
        functools.partial(_mixmem_kernel, parts=parts),
        grid=(bsz, per_b),
        in_specs=[tok(D_MODEL), tok(w4), tok(w4), tok(w4),
                  pl.BlockSpec((w4 // LANES, tm, LANES), lambda bi, i: (0, bi * per_b + i, 0)),
                  const((w4, w4)), const((1, w4)), const((D_MODEL, D_MODEL)), const((1, D_MODEL)),
                  const((1, D_MODEL)), const((D_MODEL, D_MODEL)),
                  pl.BlockSpec((MEM_LEN, 2 * D_MODEL), lambda bi, i: (bi, 0)),
                  const((D_MODEL, D_MODEL)), const((1, D_MODEL)), const((1, D_MODEL))],
        out_specs=tok(D_MODEL),
        out_shape=jax.ShapeDtypeStruct(x2.shape, F32),
        compiler_params=_cparams(2),
        name="mix_out_mem_attn",
    )(x2, ya, yb, yc, yd, wglu, bglu, wmix, gmix, bmix, wq, kv, wo, gmem, bmem)


FFN_CHUNK = 256


def _ffn_kernel(x_ref, wg_ref, wu_ref, wd_ref, g_ref, b_ref, out_ref, h_ref):
    x = x_ref[...]
    xb = x.astype(BF16)
    for j in range(0, D_FF, FFN_CHUNK):
        gate = _dot(xb, wg_ref[:, j:j + FFN_CHUNK])
        up = _dot(xb, wu_ref[:, j:j + FFN_CHUNK])
        h_ref[:, j:j + FFN_CHUNK] = (gate * _sigmoid(gate) * up).astype(BF16)
    y = _dot(h_ref[...], wd_ref[...])
    out_ref[...] = _layer_norm(DEEPNORM_ALPHA * x + y, g_ref[...], b_ref[...])


def _ffn(x2, wg, wu, wd, g, b, tm=1024):
    n = x2.shape[0]
    tok = pl.BlockSpec((tm, D_MODEL), lambda i: (i, 0))
    const = lambda shape: pl.BlockSpec(shape, lambda i: (0, 0), pipeline_mode=pl.Buffered(1))
    return pl.pallas_call(
        _ffn_kernel,
        grid=(n // tm,),
        in_specs=[tok, const((D_MODEL, D_FF)), const((D_MODEL, D_FF)), const((D_FF, D_MODEL)),
                  const((1, D_MODEL)), const((1, D_MODEL))],
        out_specs=tok,
        out_shape=jax.ShapeDtypeStruct(x2.shape, F32),
        scratch_shapes=[pltpu.VMEM((tm, D_FF), BF16)],
        compiler_params=_cparams(1),
        name="ffn",
    )(x2, wg, wu, wd, g, b)


def _permute_w_in(w):
    c = w.shape[0]
    return jnp.concatenate([
        w[:, 0:1536], w[:, 1792:2304],
        w[:, 2576:2832],
        w[:, 1536:1792], w[:, 2320:2576],
        w[:, 2304:2320], jnp.zeros((c, LANES - GLA_GATE_RANK), w.dtype),
    ], axis=1).astype(BF16)


def _mixers(x2, bsz, seq, w_in_p, ret_gn_g, gla_w_gate, gla_b_gate, gla_gn_g, s5_a_re, s5_a_im,
            s5_log_dt, s5_b_re, s5_b_im, s5_c_re, s5_c_im, s5_d):
    pa, pb, uv, pf = _inproj(x2, w_in_p)
    ya = _dilated_attention(pa, bsz, seq)
    yb = _retention(pb, pf, ret_gn_g.reshape(1, -1), bsz, seq)
    kwid = HEADS * GLA_DK
    w_gate_p = jnp.concatenate(
        [gla_w_gate, jnp.zeros((kwid - GLA_GATE_RANK, kwid), gla_w_gate.dtype)], axis=0).astype(BF16)
    yc = _gla(pb, pf, w_gate_p, gla_b_gate.reshape(1, -1), gla_gn_g.reshape(1, -1), bsz, seq)
    yd = _s5(uv, s5_a_re, s5_a_im, s5_log_dt, s5_b_re, s5_b_im, s5_c_re, s5_c_im, s5_d, bsz)
    return ya, yb, yc, yd


def kernel(x, mem, w_in, ret_gn_g, gla_w_gate, gla_b_gate, gla_gn_g, s5_a_re, s5_a_im, s5_log_dt, s5_b_re,
           s5_b_im, s5_c_re, s5_c_im, s5_d, s5_w_glu, s5_b_glu, w_mix_out, ln_mix_g, ln_mix_b, w_mem_q,
           w_mem_kv, w_mem_o, ln_mem_g, ln_mem_b, w_ff_gate, w_ff_up, w_ff_down, ln_ff_g, ln_ff_b):
    bsz, seq, dm = x.shape
    assert dm == D_MODEL and mem.shape == (bsz, MEM_LEN, D_MODEL)
    assert seq % DIL_SPAN == 0
    x2 = x.reshape(bsz * seq, dm)
    mem2 = mem.reshape(bsz * MEM_LEN, dm)
    row = lambda t: t.reshape(1, -1)
    for l in range(DEPTH):
        ys = _mixers(x2, bsz, seq, _permute_w_in(w_in[l]), ret_gn_g[l], gla_w_gate[l], gla_b_gate[l],
                     gla_gn_g[l], s5_a_re[l], s5_a_im[l], s5_log_dt[l], s5_b_re[l], s5_b_im[l],
                     s5_c_re[l], s5_c_im[l], s5_d[l])
        kv = _kvproj(mem2, w_mem_kv[l].astype(BF16))
        x2 = _mixmem(x2, *ys, s5_w_glu[l].astype(BF16), row(s5_b_glu[l]), w_mix_out[l].astype(BF16),
                     row(ln_mix_g[l]), row(ln_mix_b[l]), w_mem_q[l].astype(BF16), kv, w_mem_o[l].astype(BF16),
                     row(ln_mem_g[l]), row(ln_mem_b[l]), bsz, seq)
        x2 = _ffn(x2, w_ff_gate[l].astype(BF16), w_ff_up[l].astype(BF16), w_ff_down[l].astype(BF16),
                  row(ln_ff_g[l]), row(ln_ff_b[l]))
    return x2.reshape(bsz, seq, dm)
```

```python
import functools
import math

import jax
import jax.numpy as jnp
from jax import lax
from jax.experimental import pallas as pl
from jax.experimental.pallas import tpu as pltpu

F32 = jnp.float32
BF16 = jnp.bfloat16
HIGHEST = lax.Precision.HIGHEST
LANES = 128

D_MODEL = 1024
DEPTH = 2
GROUP_WIDTH = 256
HEADS = 4
HEAD_DIM = 64
DIL_BRANCHES = ((128, 1), (512, 4), (2048, 16))
ATTN_BLOCK = 128
RET_CHUNK = 128
GLA_DK = 32
GLA_GATE_RANK = 16
GLA_TAU = 16.0
GLA_CHUNK = 64
S5_CH = 16
S5_GROUPS = 16
S5_STATE = 64
S5_T = 8
MEM_LEN = 256
MEM_HEADS = 4
MEM_HEAD_DIM = 256
D_FF = 2816
DEEPNORM_ALPHA = (2 * DEPTH) ** 0.25
LN_EPS = 1e-5
NEG_INF = -1e30
LOG2_E = math.log2(math.e)
LN_2 = math.log(2.0)

PA_COLS = 768
PB_COLS = 1280
PF_COLS = 640
W_IN_COLS = PA_COLS + PB_COLS + GROUP_WIDTH + PF_COLS

VMEM_LIMIT = 56 * 1024 * 1024


def _cparams(n_axes):
    return pltpu.CompilerParams(dimension_semantics=("arbitrary",) * n_axes,
                                vmem_limit_bytes=VMEM_LIMIT)


def _dot(a, b):
    return jnp.dot(a, b, preferred_element_type=F32)


def _dot_nt(a, b):
    return lax.dot_general(a, b, (((1,), (1,)), ((), ())), preferred_element_type=F32)


def _dot_tn(a, b):
    return lax.dot_general(a, b, (((0,), (0,)), ((), ())), preferred_element_type=F32)


def _store_pages(ref, page, val, rows=slice(None)):
    for k in range(val.shape[1] // LANES):
        ref[page + k, rows, :] = val[:, k * LANES:(k + 1) * LANES]


def _load_pages(ref, page, rows, npages=2):
    return jnp.concatenate([ref[page + k, rows, :] for k in range(npages)], axis=1)


def _layer_norm(z, g, b):
    mu = jnp.mean(z, axis=-1, keepdims=True)
    d = z - mu
    var = jnp.mean(d * d, axis=-1, keepdims=True)
    return d * lax.rsqrt(var + LN_EPS) * g + b


def _sigmoid(x):
    return 1.0 / (1.0 + jnp.exp(-x))


def _head_of_lane(shape, head_width):
    return lax.broadcasted_iota(jnp.int32, shape, len(shape) - 1) // head_width


def _stack_heads(t, head_width):
    lane_head = _head_of_lane(t.shape, head_width)
    zero = jnp.zeros_like(t)
    return jnp.concatenate([jnp.where(lane_head == h, t, zero) for h in range(HEADS)], axis=0)


def _select_heads(t4, lane_head):
    n = t4.shape[0] // HEADS
    out = t4[(HEADS - 1) * n:]
    for h in range(HEADS - 2, -1, -1):
        out = jnp.where(lane_head == h, t4[h * n:(h + 1) * n], out)
    return out


def _split_dot(x, w_bf16, terms):
    acc = None
    for _ in range(terms):
        piece = x.astype(BF16)
        part = _dot(piece, w_bf16)
        acc = part if acc is None else acc + part
        x = x - piece.astype(F32)
    return acc


def _head_norm(o, avg_bf16):
    d = o - _split_dot(o, avg_bf16, 2)
    var = _split_dot(d * d, avg_bf16, 2)
    return d * lax.rsqrt(var + LN_EPS)


def _head_avg_matrix(width, head_width):
    r = lax.broadcasted_iota(jnp.int32, (width, width), 0) // head_width
    c = lax.broadcasted_iota(jnp.int32, (width, width), 1) // head_width
    return jnp.where(r == c, 1.0 / head_width, 0.0).astype(BF16)


CAST_BLOCK_BYTES = 4 * 1024 * 1024


def _cast_kernel(w_ref, o_ref):
    o_ref[...] = w_ref[...].astype(BF16)


def _to_bf16(w, layer=None):
    rows, cols = w.shape[-2:]
    tr = rows
    while tr * cols * 4 > CAST_BLOCK_BYTES and tr % 16 == 0:
        tr //= 2
    if layer is None:
        in_spec = pl.BlockSpec((tr, cols), lambda i: (i, 0))
    else:
        in_spec = pl.BlockSpec((None, tr, cols), lambda i: (layer, i, 0))
    return pl.pallas_call(
        _cast_kernel,
        grid=(rows // tr,),
        in_specs=[in_spec],
        out_specs=pl.BlockSpec((tr, cols), lambda i: (i, 0)),
        out_shape=jax.ShapeDtypeStruct((rows, cols), BF16),
        compiler_params=_cparams(1),
        name="weight_cast",
    )(w)


def _inproj_kernel(x_ref, w_ref, pa_ref, pb_ref, suv_ref, pf_ref, su_scr):
    xb = x_ref[...].astype(BF16)
    step = 256
    for j in range(0, PA_COLS, step):
        _store_pages(pa_ref, j // LANES, _dot(xb, w_ref[:, j:j + step]))
    for j in range(0, PB_COLS, step):
        pb_ref[:, j:j + step] = _dot(xb, w_ref[:, PA_COLS + j:PA_COLS + j + step]).astype(BF16)
    off = PA_COLS + PB_COLS
    _store_pages(su_scr, 0, _dot(xb, w_ref[:, off:off + GROUP_WIDTH]))
    nrow = su_scr.shape[1] // S5_T
    for page in range(GROUP_WIDTH // LANES):
        for s in range(S5_T):
            suv_ref[page, :, s * LANES:(s + 1) * LANES] = su_scr[page, pl.ds(s, nrow, stride=S5_T), :]
    pf_ref[...] = _dot(xb, w_ref[:, off + GROUP_WIDTH:])


def _inproj(x2, w_p, tm=512):
    n = x2.shape[0]
    return pl.pallas_call(
        _inproj_kernel,
        grid=(n // tm,),
        in_specs=[pl.BlockSpec((tm, D_MODEL), lambda i: (i, 0)),
                  pl.BlockSpec((D_MODEL, W_IN_COLS), lambda i: (0, 0))],
        out_specs=[pl.BlockSpec((PA_COLS // LANES, tm, LANES), lambda i: (0, i, 0)),
                   pl.BlockSpec((tm, PB_COLS), lambda i: (i, 0)),
                   pl.BlockSpec((GROUP_WIDTH // LANES, tm // S5_T, S5_T * LANES), lambda i: (0, i, 0)),
                   pl.BlockSpec((tm, PF_COLS), lambda i: (i, 0))],
        out_shape=[jax.ShapeDtypeStruct((PA_COLS // LANES, n, LANES), F32),
                   jax.ShapeDtypeStruct((n, PB_COLS), BF16),
                   jax.ShapeDtypeStruct((GROUP_WIDTH // LANES, n // S5_T, S5_T * LANES), F32),
                   jax.ShapeDtypeStruct((n, PF_COLS), F32)],
        scratch_shapes=[pltpu.VMEM((GROUP_WIDTH // LANES, tm, LANES), F32)],
        compiler_params=_cparams(1),
        name="inproj",
    )(x2, w_p)


DIL_SPAN = ATTN_BLOCK * max(r for _, r in DIL_BRANCHES)
DIL_GROUP = 2
DIL_GROUP_INTERIOR = {1: 5, 4: 3}


def _dil_tables(dilation):
    blk = ATTN_BLOCK
    shape = (HEADS * blk, 2 * blk)
    row = lax.broadcasted_iota(jnp.int32, shape, 0)
    ki = lax.broadcasted_iota(jnp.int32, shape, 1)
    head = row // blk
    dist = row % blk + blk - ki
    valid = (dist >= 0) & (dist <= blk)
    slope = jnp.where(head == 0, 2.0 ** -2, jnp.where(head == 1, 2.0 ** -4,
                      jnp.where(head == 2, 2.0 ** -6, 2.0 ** -8))).astype(F32)
    alibi = -slope * (dist * dilation).astype(F32) * LOG2_E
    return jnp.where(valid, alibi, NEG_INF), jnp.where(valid & (ki >= blk), alibi, NEG_INF)


def _dil_attend(qs, kwins, vwins, tbs):
    scale = HEAD_DIM ** -0.5 * LOG2_E
    ss = [_dot_nt(_stack_heads((q * scale).astype(BF16), HEAD_DIM), kwin.astype(BF16))
          for q, kwin in zip(qs, kwins)]
    ms, ps, ls = [], [], []
    for s, tb in zip(ss, tbs):
        s = s + tb
        m = jnp.max(s, axis=-1, keepdims=True)
        p = jnp.exp2(s - m)
        ms.append(m)
        ls.append(jnp.sum(p, axis=-1, keepdims=True))
        ps.append(p.astype(BF16))
    o4s = [_dot(p, vwin.astype(BF16)) for p, vwin in zip(ps, vwins)]
    lane_head = _head_of_lane((ATTN_BLOCK, GROUP_WIDTH), HEAD_DIM)
    outs = []
    for o4, m, l in zip(o4s, ms, ls):
        lse4 = jnp.broadcast_to((m + jnp.log2(l)) * LN_2, o4.shape)
        outs.append((_select_heads(o4 * (1.0 / l), lane_head), _select_heads(lse4, lane_head)))
    return outs


def _dil_kernel(q_ref, kp_ref, kc_ref, vp_ref, vc_ref, o_ref, tb_ref, ob_ref, lb_ref):
    blk = ATTN_BLOCK
    span = DIL_SPAN

    @pl.when((pl.program_id(0) == 0) & (pl.program_id(1) == 0))
    def _init_tables():
        for bi, (_, r) in enumerate(DIL_BRANCHES):
            t_any, t_first = _dil_tables(r)
            tb_ref[bi, 0] = t_any
            tb_ref[bi, 1] = t_first

    first = jnp.where(pl.program_id(1) == 0, 1, 0)

    def rows(start, size, r):
        if r == 1:
            return pl.ds(pl.multiple_of(start, blk), size)
        return pl.ds(start, size, stride=r)

    def store(bi, rqs, outs):
        for rq, (o, lse) in zip(rqs, outs):
            _store_pages(ob_ref, 2 * bi, o, rq)
            _store_pages(lb_ref, 2 * bi, lse, rq)

    for bi, (_, r) in enumerate(DIL_BRANCHES):
        per_sub = span // (blk * r)
        group = min(r, DIL_GROUP)

        def boundary(g, carry, bi=bi, r=r, group=group):
            rqs = [rows(g * group + u, blk, r) for u in range(group)]
            rps = [rows(g * group + u + span - blk * r, blk, r) for u in range(group)]
            kwins = [jnp.concatenate([_load_pages(kp_ref, 0, rp), _load_pages(kc_ref, 0, rq)], axis=0)
                     for rp, rq in zip(rps, rqs)]
            vwins = [jnp.concatenate([_load_pages(vp_ref, 0, rp), _load_pages(vc_ref, 0, rq)], axis=0)
                     for rp, rq in zip(rps, rqs)]
            tb = tb_ref[bi, first]
            store(bi, rqs, _dil_attend([_load_pages(q_ref, 0, rq) for rq in rqs], kwins, vwins, [tb] * group))
            return carry

        lax.fori_loop(0, r // group, boundary, 0)

        if per_sub > 1:
            group = DIL_GROUP_INTERIOR[r]

            def interior(g, carry, bi=bi, r=r, group=group):
                rqs, rks = [], []
                for u in range(group):
                    idx = g * group + u
                    c = lax.rem(idx, r)
                    j = 1 + lax.div(idx, r)
                    rqs.append(rows(c + blk * r * j, blk, r))
                    rks.append(rows(c + blk * r * (j - 1), 2 * blk, r))
                tb = tb_ref[bi, 0]
                store(bi, rqs, _dil_attend([_load_pages(q_ref, 0, rq) for rq in rqs],
                                           [_load_pages(kc_ref, 0, rk) for rk in rks],
                                           [_load_pages(vc_ref, 0, rk) for rk in rks], [tb] * group))
                return carry

            lax.fori_loop(0, r * (per_sub - 1) // group, interior, 0)

    def combine(i, carry):
        rr = pl.ds(pl.multiple_of(i * blk, blk), blk)
        l1, l4, l16 = (_load_pages(lb_ref, 2 * bi, rr) for bi in range(3))
        mx = jnp.maximum(jnp.maximum(l1, l4), l16)
        e1, e4, e16 = jnp.exp(l1 - mx), jnp.exp(l4 - mx), jnp.exp(l16 - mx)
        o1, o4, o16 = (_load_pages(ob_ref, 2 * bi, rr) for bi in range(3))
        ya = (e1 * o1 + e4 * o4 + e16 * o16) / (e1 + e4 + e16)
        o_ref[rr, :] = ya.astype(BF16)
        return carry

    lax.fori_loop(0, span // blk, combine, 0)


def _dilated_attention(pa, bsz, seq):
    n = bsz * seq
    span = DIL_SPAN
    per_b = seq // span
    w = GROUP_WIDTH
    pages = w // LANES

    def cur(col):
        return pl.BlockSpec((pages, span, LANES), lambda b, i: (col, b * per_b + i, 0))

    def prev(col):
        return pl.BlockSpec((pages, span, LANES), lambda b, i: (col, b * per_b + jnp.maximum(i - 1, 0), 0))

    nbr = len(DIL_BRANCHES)
    return pl.pallas_call(
        _dil_kernel,
        grid=(bsz, per_b),
        in_specs=[cur(0), prev(1), cur(1), prev(2), cur(2)],
        out_specs=pl.BlockSpec((span, w), lambda b, i: (b * per_b + i, 0)),
        out_shape=jax.ShapeDtypeStruct((n, w), BF16),
        scratch_shapes=[pltpu.VMEM((nbr, 2, HEADS * ATTN_BLOCK, 2 * ATTN_BLOCK), F32),
                        pltpu.VMEM((nbr * pages, span, LANES), F32),
                        pltpu.VMEM((nbr * pages, span, LANES), F32)],
        compiler_params=_cparams(2),
        name="dilated_attention",
    )(pa, pa, pa, pa, pa)


def _ret_kernel(q_ref, k_ref, v_ref, g_ref, gn_ref, o_ref, r_ref, dtab_ref, qd_ref, kd_ref, o_scr, *, nchunk):
    c = RET_CHUNK
    width = GROUP_WIDTH
    scale = HEAD_DIM ** -0.5
    log_g = [math.log(1.0 - 2.0 ** (-5.0 - h)) for h in range(HEADS)]

    def head_log_g(head):
        return jnp.where(head == 0, log_g[0], jnp.where(head == 1, log_g[1],
                         jnp.where(head == 2, log_g[2], log_g[3]))).astype(F32)

    @pl.when((pl.program_id(0) == 0) & (pl.program_id(1) == 0))
    def _init_tables():
        qry = lax.broadcasted_iota(jnp.int32, (c, HEADS * c), 0)
        col = lax.broadcasted_iota(jnp.int32, (c, HEADS * c), 1)
        rel = (qry - col % c).astype(F32)
        decay = jnp.exp(jnp.maximum(rel, 0.0) * head_log_g(col // c))
        dtab_ref[...] = jnp.where(rel >= 0, decay, 0.0) * scale
        pos = (lax.broadcasted_iota(jnp.int32, (nchunk * c, width), 0) % c).astype(F32)
        lgl = head_log_g(_head_of_lane((nchunk * c, width), HEAD_DIM))
        qd_ref[...] = jnp.exp((pos + 1.0) * lgl)
        kd_ref[...] = jnp.exp((c - 1.0 - pos) * lgl) * scale

    @pl.when(pl.program_id(1) == 0)
    def _reset_state():
        r_ref[...] = jnp.zeros_like(r_ref)

    rr = lax.broadcasted_iota(jnp.int32, (width, width), 0) // HEAD_DIM
    cc = lax.broadcasted_iota(jnp.int32, (width, width), 1) // HEAD_DIM
    same_head = rr == cc
    chunk_decay = jnp.exp(c * head_log_g(_head_of_lane((1, width), HEAD_DIM)))
    q_dec = (q_ref[...].astype(F32) * qd_ref[...]).astype(BF16)
    k_dec = (k_ref[...].astype(F32) * kd_ref[...]).astype(BF16)

    sls = [slice(ci * c, (ci + 1) * c) for ci in range(nchunk)]
    scores = [_dot_nt(q_ref[sl, :], _stack_heads(k_ref[sl, :], HEAD_DIM)) for sl in sls]
    probs = [(sc * dtab_ref[...]).astype(BF16) for sc in scores]
    inner = [_dot(pr, _stack_heads(v_ref[sl, :], HEAD_DIM)) for pr, sl in zip(probs, sls)]
    upd = [_dot_tn(k_dec[sl, :], v_ref[sl, :]) for sl in sls]
    states = [r_ref[...]]
    for ci in range(nchunk):
        states.append(states[-1] * chunk_decay + jnp.where(same_head, upd[ci], 0.0))
    r_ref[...] = states[nchunk]
    cross = [_dot(q_dec[sl, :], states[ci].astype(BF16)) for ci, sl in enumerate(sls)]
    for ci, sl in enumerate(sls):
        o_scr[sl, :] = inner[ci] + cross[ci]

    o = _head_norm(o_scr[...], _head_avg_matrix(width, HEAD_DIM)) * gn_ref[...]
    g = g_ref[...]
    o_ref[...] = (o * (g * _sigmoid(g))).astype(BF16)


def _retention(pb, pf, gn, bsz, seq, tq=2048):
    n = bsz * seq
    per_b = seq // tq
    w = GROUP_WIDTH
    nchunk = tq // RET_CHUNK

    def tok(col):
        return pl.BlockSpec((tq, w), lambda b, i: (b * per_b + i, col))

    return pl.pallas_call(
        functools.partial(_ret_kernel, nchunk=nchunk),
        grid=(bsz, per_b),
        in_specs=[tok(0), tok(1), tok(2), tok(0), pl.BlockSpec((1, w), lambda b, i: (0, 0))],
        out_specs=tok(0),
        out_shape=jax.ShapeDtypeStruct((n, w), BF16),
        scratch_shapes=[pltpu.VMEM((w, w), F32),
                        pltpu.VMEM((RET_CHUNK, HEADS * RET_CHUNK), F32),
                        pltpu.VMEM((tq, w), F32),
                        pltpu.VMEM((tq, w), F32),
                        pltpu.VMEM((tq, w), F32)],
        compiler_params=_cparams(2),
        name="retention",
    )(pb, pb, pb, pf, gn)


def _gla_kernel(q_ref, k_ref, v_ref, gl_ref, og_ref, wg_ref, bg_ref, gn_ref, o_ref, rt_ref, cum_ref, o_scr,
                *, nchunk):
    c = GLA_CHUNK
    tq = nchunk * c
    kwid = HEADS * GLA_DK
    vwid = GROUP_WIDTH
    scale = GLA_DK ** -0.5

    @pl.when((pl.program_id(0) == 0) & (pl.program_id(1) == 0))
    def _init_tables():
        i = lax.broadcasted_iota(jnp.int32, (2 * c, 2 * c), 0)
        j = lax.broadcasted_iota(jnp.int32, (2 * c, 2 * c), 1)
        cum_ref[...] = jnp.where((i // c == j // c) & (i >= j), 1.0, 0.0).astype(BF16)

    @pl.when(pl.program_id(1) == 0)
    def _reset_state():
        rt_ref[...] = jnp.zeros_like(rt_ref)

    causal = lax.broadcasted_iota(jnp.int32, (c, HEADS * c), 0) >= (
        lax.broadcasted_iota(jnp.int32, (c, HEADS * c), 1) % c)
    rr = lax.broadcasted_iota(jnp.int32, (vwid, kwid), 0) // HEAD_DIM
    cc = lax.broadcasted_iota(jnp.int32, (vwid, kwid), 1) // GLA_DK
    same_head = rr == cc

    z = _dot(gl_ref[...].astype(BF16), wg_ref[...]) + bg_ref[...]
    log_a = (jnp.minimum(z, 0.0) - jnp.log(1.0 + jnp.exp(-jnp.abs(z)))) * (1.0 / GLA_TAU)
    hi = log_a.astype(BF16)
    rest = log_a - hi.astype(F32)
    mid = rest.astype(BF16)
    lo = (rest - mid.astype(F32)).astype(BF16)
    la3 = jnp.concatenate([hi, mid, lo], axis=1)
    sums = jnp.concatenate([_dot(cum_ref[...], la3[p:p + 2 * c, :]) for p in range(0, tq, 2 * c)], axis=0)
    bcum = sums[:, :kwid] + sums[:, kwid:2 * kwid] + sums[:, 2 * kwid:]
    blast = jnp.concatenate([jnp.broadcast_to(bcum[e - 1:e, :], (c, kwid)) for e in range(c, tq + c, c)],
                            axis=0)
    q = q_ref[...].astype(F32)
    k = k_ref[...].astype(F32)
    q_in = (q * scale * jnp.exp(bcum)).astype(BF16)
    k_out = (k * jnp.exp(-bcum)).astype(BF16)
    k_st = (k * jnp.exp(blast - bcum)).astype(BF16)
    chunk_decay = jnp.exp(blast)

    sls = [slice(ci * c, (ci + 1) * c) for ci in range(nchunk)]
    att = [_dot_nt(q_in[sl, :], _stack_heads(k_out[sl, :], GLA_DK)) for sl in sls]
    att = [jnp.where(causal, a, 0.0).astype(BF16) for a in att]
    intra = [_dot(a, _stack_heads(v_ref[sl, :], HEAD_DIM)) for a, sl in zip(att, sls)]
    upd = [_dot_tn(v_ref[sl, :], k_st[sl, :]) for sl in sls]
    states = [rt_ref[...]]
    for ci in range(nchunk):
        states.append(states[-1] * chunk_decay[ci * c:ci * c + 1, :] + jnp.where(same_head, upd[ci], 0.0))
    rt_ref[...] = states[nchunk]
    cross = [_dot_nt(q_in[sl, :], states[ci].astype(BF16)) for ci, sl in enumerate(sls)]
    for ci, sl in enumerate(sls):
        o_scr[sl, :] = intra[ci] + cross[ci]

    o = _head_norm(o_scr[...], _head_avg_matrix(vwid, HEAD_DIM)) * gn_ref[...]
    g = og_ref[...]
    o_ref[...] = (o * (g * _sigmoid(g))).astype(BF16)


def _gla(pb, pf, w_gate_p, b_gate, gn, bsz, seq, tq=2048):
    n = bsz * seq
    per_b = seq // tq
    kwid = HEADS * GLA_DK
    nchunk = tq // GLA_CHUNK

    def tok(width, col):
        return pl.BlockSpec((tq, width), lambda b, i: (b * per_b + i, col))

    def const(shape):
        return pl.BlockSpec(shape, lambda b, i: (0, 0))

    return pl.pallas_call(
        functools.partial(_gla_kernel, nchunk=nchunk),
        grid=(bsz, per_b),
        in_specs=[tok(kwid, 6), tok(kwid, 7), tok(GROUP_WIDTH, 4),
                  tok(kwid, 4), tok(GROUP_WIDTH, 1),
                  const((kwid, kwid)), const((1, kwid)), const((1, GROUP_WIDTH))],
        out_specs=tok(GROUP_WIDTH, 0),
        out_shape=jax.ShapeDtypeStruct((n, GROUP_WIDTH), BF16),
        scratch_shapes=[pltpu.VMEM((GROUP_WIDTH, kwid), F32),
                        pltpu.VMEM((2 * GLA_CHUNK, 2 * GLA_CHUNK), BF16),
                        pltpu.VMEM((tq, GROUP_WIDTH), F32)],
        compiler_params=_cparams(2),
        name="gla",
    )(pb, pb, pb, pf, pf, w_gate_p, b_gate, gn)


S5_PAGES = GROUP_WIDTH // LANES
S5_ROW = S5_T * LANES
S5_NSTATE = S5_GROUPS * S5_STATE // S5_PAGES
S5_SW = 2 * S5_NSTATE


def _s5prep_kernel(ar_ref, ai_ref, ld_ref, arc_ref, aic_ref, ldc_ref, bre_ref, bim_ref, cre_ref, cim_ref,
                   bst_ref, bigk_ref, ccr_ref, a8_ref):
    w = LANES

    def discretise(ar, ai, ld):
        dt = jnp.exp(ld)
        ea = jnp.exp(ar * dt)
        return ea * jnp.cos(ai * dt), ea * jnp.sin(ai * dt)

    def powers(a_re, a_im, n):
        pw = [(jnp.ones_like(a_re), jnp.zeros_like(a_re)), (a_re, a_im)]
        for _ in range(2, n + 1):
            pr, pi = pw[-1]
            pw.append((pr * a_re - pi * a_im, pr * a_im + pi * a_re))
        return pw

    ar, ai = ar_ref[...], ai_ref[...]
    a_re, a_im = discretise(ar, ai, ld_ref[...])
    pw = powers(a_re, a_im, S5_T)
    den = ar * ar + ai * ai
    nr = a_re - 1.0
    f_re = (nr * ar + a_im * ai) / den
    f_im = (a_im * ar - nr * ai) / den
    b_re, b_im = bre_ref[...], bim_ref[...]
    bb_re = f_re * b_re - f_im * b_im
    bb_im = f_re * b_im + f_im * b_re
    c_re, c_im = cre_ref[...], cim_ref[...]
    cmat = jnp.concatenate([c_re, -c_im], axis=0)

    bigk_ref[...] = jnp.zeros_like(bigk_ref)
    for lag in range(S5_T):
        pr, pi = pw[lag]
        lmat = jnp.concatenate([pr * bb_re - pi * bb_im, pr * bb_im + pi * bb_re], axis=1)
        s = S5_T - 1 - lag
        bst_ref[s * w:(s + 1) * w, :] = lmat.astype(BF16)
        m = jnp.dot(lmat, cmat, precision=HIGHEST, preferred_element_type=F32).astype(BF16)
        for s in range(S5_T - lag):
            t = s + lag
            bigk_ref[s * w:(s + 1) * w, t * w:(t + 1) * w] = m

    ac_re, ac_im = discretise(arc_ref[...], aic_ref[...], ldc_ref[...])
    pwc = powers(ac_re, ac_im, S5_T)
    for t in range(S5_T):
        pr, pi = pwc[t + 1]
        ccr_ref[:S5_NSTATE, t * w:(t + 1) * w] = (c_re * pr - c_im * pi).astype(BF16)
        ccr_ref[S5_NSTATE:, t * w:(t + 1) * w] = (-(c_re * pi + c_im * pr)).astype(BF16)

    a8_ref[...] = jnp.concatenate(pw[S5_T], axis=1)


def _s5_prep(a_re, a_im, log_dt, b_re, b_im, c_re, c_im):
    pg, g, p, ch = S5_PAGES, S5_GROUPS // S5_PAGES, S5_STATE, S5_CH
    eye = jnp.eye(g, dtype=bool)
    row = lambda t: t.reshape(pg, 1, g * p)
    col = lambda t: t.reshape(pg, g * p, 1)
    ld = jnp.broadcast_to(log_dt[:, None], (pg * g, p))

    def expand_b(b):
        t = jnp.transpose(b.reshape(pg, g, p, ch), (0, 1, 3, 2))[:, :, :, None, :]
        return jnp.where(eye[None, :, None, :, None], t, 0.0).reshape(pg, g * ch, g * p)

    def expand_c(c):
        t = jnp.transpose(c.reshape(pg, g, ch, p), (0, 1, 3, 2))[:, :, :, None, :]
        return jnp.where(eye[None, :, None, :, None], t, 0.0).reshape(pg, g * p, g * ch)

    sd = lambda shape, dt: jax.ShapeDtypeStruct((pg,) + shape, dt)
    spec = lambda shape: pl.BlockSpec((None,) + shape, lambda i: (i, 0, 0))
    return pl.pallas_call(
        _s5prep_kernel,
        grid=(pg,),
        in_specs=[spec((1, g * p))] * 3 + [spec((g * p, 1))] * 3 + [spec((g * ch, g * p))] * 2
                 + [spec((g * p, g * ch))] * 2,
        out_specs=[spec((S5_ROW, S5_SW)), spec((S5_ROW, S5_ROW)), spec((S5_SW, S5_ROW)), spec((1, S5_SW))],
        out_shape=[sd((S5_ROW, S5_SW), BF16), sd((S5_ROW, S5_ROW), BF16),
                   sd((S5_SW, S5_ROW), BF16), sd((1, S5_SW), F32)],
        compiler_params=_cparams(1),
        name="s5_prep",
    )(row(a_re), row(a_im), row(ld), col(a_re), col(a_im), col(ld),
      expand_b(b_re), expand_b(b_im), expand_c(c_re), expand_c(c_im))


def _s5_kernel(u_ref, bst_ref, bigk_ref, ccr_ref, a8_ref, d_ref, o_ref, v_scr, xs_scr, st_ref, *, tr):
    ns = S5_NSTATE
    w = LANES
    pages = range(S5_PAGES)

    @pl.when(pl.program_id(1) == 0)
    def _reset_state():
        st_ref[...] = jnp.zeros_like(st_ref)

    us = [u_ref[pg] for pg in pages]
    ubs = [u.astype(BF16) for u in us]
    for pg in pages:
        v_scr[pg] = _dot(ubs[pg], bst_ref[pg])

    a_re = [a8_ref[pg, :, :ns] for pg in pages]
    a_im = [a8_ref[pg, :, ns:] for pg in pages]

    def body(r, carry):
        new = []
        for pg in pages:
            xr, xi = carry[2 * pg], carry[2 * pg + 1]
            xs_scr[pg, pl.ds(r, 1), :ns] = xr
            xs_scr[pg, pl.ds(r, 1), ns:] = xi
            vr = v_scr[pg, pl.ds(r, 1), :ns]
            vi = v_scr[pg, pl.ds(r, 1), ns:]
            new += [a_re[pg] * xr - a_im[pg] * xi + vr, a_re[pg] * xi + a_im[pg] * xr + vi]
        return tuple(new)

    init = []
    for pg in pages:
        init += [st_ref[pg, 0:1, :ns], st_ref[pg, 0:1, ns:]]
    fin = lax.fori_loop(0, tr, body, tuple(init), unroll=8)
    for pg in pages:
        st_ref[pg, 0:1, :ns] = fin[2 * pg]
        st_ref[pg, 0:1, ns:] = fin[2 * pg + 1]

    step = 2
    for pg in pages:
        xb = xs_scr[pg].astype(BF16)
        d_row = jnp.concatenate([d_ref[pg]] * step, axis=1)
        for t in range(0, S5_T, step):
            cols = slice(t * w, (t + step) * w)
            y = (_dot(ubs[pg][:, :(t + step) * w], bigk_ref[pg, :(t + step) * w, cols])
                 + _dot(xb, ccr_ref[pg, :, cols]) + d_row * us[pg][:, cols])
            y = 0.5 * y * (1.0 + jnp.tanh(math.sqrt(2.0 / math.pi) * (y + 0.044715 * (y * y * y))))
            for k in range(step):
                o_ref[pg, pl.ds(t + k, tr, stride=S5_T), :] = y[:, k * w:(k + 1) * w]


def _s5(uv, a_re, a_im, log_dt, b_re, b_im, c_re, c_im, d, bsz, tr=256):
    bst, bigk, ccr, a8 = _s5_prep(a_re, a_im, log_dt, b_re, b_im, c_re, c_im)
    pg, rows = uv.shape[:2]
    per_b = rows // bsz // tr
    const = lambda shape: pl.BlockSpec((pg,) + shape, lambda b, i: (0, 0, 0), pipeline_mode=pl.Buffered(1))
    return pl.pallas_call(
        functools.partial(_s5_kernel, tr=tr),
        grid=(bsz, per_b),
        in_specs=[pl.BlockSpec((pg, tr, S5_ROW), lambda b, i: (0, b * per_b + i, 0)),
                  const((S5_ROW, S5_SW)), const((S5_ROW, S5_ROW)), const((S5_SW, S5_ROW)),
                  const((1, S5_SW)), const((1, LANES))],
        out_specs=pl.BlockSpec((pg, tr * S5_T, LANES), lambda b, i: (0, b * per_b + i, 0)),
        out_shape=jax.ShapeDtypeStruct((pg, rows * S5_T, LANES), F32),
        scratch_shapes=[pltpu.VMEM((pg, tr, S5_SW), F32), pltpu.VMEM((pg, tr, S5_SW), F32),
                        pltpu.VMEM((pg, 8, S5_SW), F32)],
        compiler_params=_cparams(2),
        name="s5",
    )(uv, bst, bigk, ccr, a8, d.reshape(pg, 1, LANES))


def _kvproj_kernel(m_ref, w_ref, o_ref):
    mb = m_ref[...].astype(BF16)
    step = 512
    for j in range(0, 2 * D_MODEL, step):
        o_ref[:, j:j + step] = _dot(mb, w_ref[:, j:j + step]).astype(BF16)


def _kvproj(mem2, w_kv, tm=512):
    n = mem2.shape[0]
    return pl.pallas_call(
        _kvproj_kernel,
        grid=(n // tm,),
        in_specs=[pl.BlockSpec((tm, D_MODEL), lambda i: (i, 0)),
                  pl.BlockSpec((D_MODEL, 2 * D_MODEL), lambda i: (0, 0))],
        out_specs=pl.BlockSpec((tm, 2 * D_MODEL), lambda i: (i, 0)),
        out_shape=jax.ShapeDtypeStruct((n, 2 * D_MODEL), BF16),
        compiler_params=_cparams(1),
        name="mem_kv_proj",
    )(mem2, w_kv)


def _mixmem_kernel(x_ref, ya_ref, yb_ref, yc_ref, yd_ref, wglu_ref, bglu_ref, wmix_ref, gmix_ref, bmix_ref,
                   wq_ref, kv_ref, wo_ref, gmem_ref, bmem_ref, out_ref, *, parts):
    hd = MEM_HEAD_DIM
    rows = x_ref.shape[0] // parts
    sls = [slice(part * rows, (part + 1) * rows) for part in range(parts)]
    yds = [_load_pages(yd_ref, 0, sl) for sl in sls]
    gates = [_dot(yd.astype(BF16), wglu_ref[...]) for yd in yds]
    yds = [yd * _sigmoid(g + bglu_ref[...]) for yd, g in zip(yds, gates)]
    ys = [jnp.concatenate([ya_ref[sl, :], yb_ref[sl, :], yc_ref[sl, :], yd.astype(BF16)], axis=1)
          for sl, yd in zip(sls, yds)]
    hs = [_dot(y, wmix_ref[...]) for y in ys]
    xs = [_layer_norm(DEEPNORM_ALPHA * x_ref[sl, :] + h, gmix_ref[...], bmix_ref[...]) for sl, h in zip(sls, hs)]
    qs = [_dot(x.astype(BF16), wq_ref[...]).astype(BF16) for x in xs]
    outs = [[] for _ in sls]
    for h in range(MEM_HEADS):
        k = kv_ref[:, h * hd:(h + 1) * hd]
        v = kv_ref[:, D_MODEL + h * hd:D_MODEL + (h + 1) * hd]
        ss = [_dot_nt(q[:, h * hd:(h + 1) * hd], k) * (hd ** -0.5) for q in qs]
        ps = []
        for s in ss:
            p = jnp.exp(s - jnp.max(s, axis=-1, keepdims=True))
            ps.append((p / jnp.sum(p, axis=-1, keepdims=True)).astype(BF16))
        for o, p in zip(outs, ps):
            o.append(_dot(p, v).astype(BF16))
    hs = [_dot(jnp.concatenate(o, axis=1), wo_ref[...]) for o in outs]
    for sl, x, h in zip(sls, xs, hs):
        out_ref[sl, :] = _layer_norm(DEEPNORM_ALPHA * x + h, gmem_ref[...], bmem_ref[...])


def _mixmem(x2, ya, yb, yc, yd, wglu, bglu, wmix, gmix, bmix, wq, kv, wo, gmem, bmem, bsz, seq, tm=1024, parts=4):
    per_b = seq // tm
    w4 = GROUP_WIDTH
    const = lambda shape: pl.BlockSpec(shape, lambda bi, i: (0, 0), pipeline_mode=pl.Buffered(1))
    tok = lambda width: pl.BlockSpec((tm, width), lambda bi, i: (bi * per_b + i, 0))
    return pl.pallas_call(
        functools.partial(_mixmem_kernel, parts=parts),
        grid=(bsz, per_b),
        in_specs=[tok(D_MODEL), tok(w4), tok(w4), tok(w4),
                  pl.BlockSpec((w4 // LANES, tm, LANES), lambda bi, i: (0, bi * per_b + i, 0)),
                  const((w4, w4)), const((1, w4)), const((D_MODEL, D_MODEL)), const((1, D_MODEL)),
                  const((1, D_MODEL)), const((D_MODEL, D_MODEL)),
                  pl.BlockSpec((MEM_LEN, 2 * D_MODEL), lambda bi, i: (bi, 0)),
                  const((D_MODEL, D_MODEL)), const((1, D_MODEL)), const((1, D_MODEL))],
        out_specs=tok(D_MODEL),
        out_shape=jax.ShapeDtypeStruct(x2.shape, F32),
        compiler_params=_cparams(2),
        name="mix_out_mem_attn",
    )(x2, ya, yb, yc, yd, wglu, bglu, wmix, gmix, bmix, wq, kv, wo, gmem, bmem)


FFN_CHUNK = 256


def _ffn_kernel(x_ref, wg_ref, wu_ref, wd_ref, g_ref, b_ref, out_ref, h_ref, *, parts):
    rows = x_ref.shape[0] // parts
    sls = [slice(p * rows, (p + 1) * rows) for p in range(parts)]
    xbs = [x_ref[sl, :].astype(BF16) for sl in sls]
    for j in range(0, D_FF, FFN_CHUNK):
        gates = [_dot(xb, wg_ref[:, j:j + FFN_CHUNK]) for xb in xbs]
        ups = [_dot(xb, wu_ref[:, j:j + FFN_CHUNK]) for xb in xbs]
        for sl, gate, up in zip(sls, gates, ups):
            h_ref[sl, j:j + FFN_CHUNK] = (gate * _sigmoid(gate) * up).astype(BF16)
    ys = [_dot(h_ref[sl, :], wd_ref[...]) for sl in sls]
    for sl, y in zip(sls, ys):
        out_ref[sl, :] = _layer_norm(DEEPNORM_ALPHA * x_ref[sl, :] + y, g_ref[...], b_ref[...])


def _ffn(x2, wg, wu, wd, g, b, tm=1024, parts=2):
    n = x2.shape[0]
    tok = pl.BlockSpec((tm, D_MODEL), lambda i: (i, 0))
    const = lambda shape: pl.BlockSpec(shape, lambda i: (0, 0), pipeline_mode=pl.Buffered(1))
    return pl.pallas_call(
        functools.partial(_ffn_kernel, parts=parts),
        grid=(n // tm,),
        in_specs=[tok, const((D_MODEL, D_FF)), const((D_MODEL, D_FF)), const((D_FF, D_MODEL)),
                  const((1, D_MODEL)), const((1, D_MODEL))],
        out_specs=tok,
        out_shape=jax.ShapeDtypeStruct(x2.shape, F32),
        scratch_shapes=[pltpu.VMEM((tm, D_FF), BF16)],
        compiler_params=_cparams(1),
        name="ffn",
    )(x2, wg, wu, wd, g, b)


def _permute_w_in(w):
    c = w.shape[0]
    return jnp.concatenate([
        w[:, 0:1536], w[:, 1792:2304],
        w[:, 2576:2832],
        w[:, 1536:1792], w[:, 2320:2576],
        w[:, 2304:2320], jnp.zeros((c, LANES - GLA_GATE_RANK), w.dtype),
    ], axis=1)


def _mixers(x2, bsz, seq, w_in_p, ret_gn_g, gla_w_gate, gla_b_gate, gla_gn_g, s5_a_re, s5_a_im,
            s5_log_dt, s5_b_re, s5_b_im, s5_c_re, s5_c_im, s5_d):
    pa, pb, uv, pf = _inproj(x2, _to_bf16(w_in_p))
    ya = _dilated_attention(pa, bsz, seq)
    yb = _retention(pb, pf, ret_gn_g.reshape(1, -1), bsz, seq)
    kwid = HEADS * GLA_DK
    w_gate_p = jnp.concatenate(
        [gla_w_gate, jnp.zeros((kwid - GLA_GATE_RANK, kwid), gla_w_gate.dtype)], axis=0).astype(BF16)
    yc = _gla(pb, pf, w_gate_p, gla_b_gate.reshape(1, -1), gla_gn_g.reshape(1, -1), bsz, seq)
    yd = _s5(uv, s5_a_re, s5_a_im, s5_log_dt, s5_b_re, s5_b_im, s5_c_re, s5_c_im, s5_d, bsz)
    return ya, yb, yc, yd


def kernel(x, mem, w_in, ret_gn_g, gla_w_gate, gla_b_gate, gla_gn_g, s5_a_re, s5_a_im, s5_log_dt, s5_b_re,
           s5_b_im, s5_c_re, s5_c_im, s5_d, s5_w_glu, s5_b_glu, w_mix_out, ln_mix_g, ln_mix_b, w_mem_q,
           w_mem_kv, w_mem_o, ln_mem_g, ln_mem_b, w_ff_gate, w_ff_up, w_ff_down, ln_ff_g, ln_ff_b):
    bsz, seq, dm = x.shape
    assert dm == D_MODEL and mem.shape == (bsz, MEM_LEN, D_MODEL)
    assert seq % DIL_SPAN == 0
    x2 = x.reshape(bsz * seq, dm)
    mem2 = mem.reshape(bsz * MEM_LEN, dm)
    row = lambda t: t.reshape(1, -1)
    for l in range(DEPTH):
        ys = _mixers(x2, bsz, seq, _permute_w_in(w_in[l]), ret_gn_g[l], gla_w_gate[l], gla_b_gate[l],
                     gla_gn_g[l], s5_a_re[l], s5_a_im[l], s5_log_dt[l], s5_b_re[l], s5_b_im[l],
                     s5_c_re[l], s5_c_im[l], s5_d[l])
        kv = _kvproj(mem2, _to_bf16(w_mem_kv, l))
        x2 = _mixmem(x2, *ys, _to_bf16(s5_w_glu, l), row(s5_b_glu[l]), _to_bf16(w_mix_out, l),
                     row(ln_mix_g[l]), row(ln_mix_b[l]), _to_bf16(w_mem_q, l), kv, _to_bf16(w_mem_o, l),
                     row(ln_mem_g[l]), row(ln_mem_b[l]), bsz, seq)
        x2 = _ffn(x2, _to_bf16(w_ff_gate, l), _to_bf16(w_ff_up, l), _to_bf16(w_ff_down, l),
                  row(ln_ff_g[l]), row(ln_ff_b[l]))
    return x2.reshape(bsz, seq, dm)
```

```python
import functools
import math

import jax
import jax.numpy as jnp
from jax import lax
from jax.experimental import pallas as pl
from jax.experimental.pallas import tpu as pltpu

F32 = jnp.float32
BF16 = jnp.bfloat16
HIGHEST = lax.Precision.HIGHEST
LANES = 128

D_MODEL = 1024
DEPTH = 2
GROUP_WIDTH = 256
HEADS = 4
HEAD_DIM = 64
DIL_BRANCHES = ((128, 1), (512, 4), (2048, 16))
ATTN_BLOCK = 128
RET_CHUNK = 128
GLA_DK = 32
GLA_GATE_RANK = 16
GLA_TAU = 16.0
GLA_CHUNK = 64
S5_CH = 16
S5_GROUPS = 16
S5_STATE = 64
S5_T = 8
MEM_LEN = 256
MEM_HEADS = 4
MEM_HEAD_DIM = 256
D_FF = 2816
DEEPNORM_ALPHA = (2 * DEPTH) ** 0.25
LN_EPS = 1e-5
NEG_INF = -1e30
LOG2_E = math.log2(math.e)
LN_2 = math.log(2.0)

PA_COLS = 768
PB_COLS = 1280
PF_COLS = 640
W_IN_COLS = PA_COLS + PB_COLS + GROUP_WIDTH + PF_COLS

VMEM_LIMIT = 56 * 1024 * 1024


def _cparams(n_axes):
    return pltpu.CompilerParams(dimension_semantics=("arbitrary",) * n_axes,
                                vmem_limit_bytes=VMEM_LIMIT)


def _dot(a, b):
    return jnp.dot(a, b, preferred_element_type=F32)


def _dot_nt(a, b):
    return lax.dot_general(a, b, (((1,), (1,)), ((), ())), preferred_element_type=F32)


def _dot_tn(a, b):
    return lax.dot_general(a, b, (((0,), (0,)), ((), ())), preferred_element_type=F32)


def _store_pages(ref, page, val, rows=slice(None)):
    for k in range(val.shape[1] // LANES):
        ref[page + k, rows, :] = val[:, k * LANES:(k + 1) * LANES]


def _load_pages(ref, page, rows, npages=2):
    return jnp.concatenate([ref[page + k, rows, :] for k in range(npages)], axis=1)


def _layer_norm(z, g, b):
    mu = jnp.mean(z, axis=-1, keepdims=True)
    d = z - mu
    var = jnp.mean(d * d, axis=-1, keepdims=True)
    return d * lax.rsqrt(var + LN_EPS) * g + b


def _sigmoid(x):
    return 1.0 / (1.0 + jnp.exp(-x))


def _head_of_lane(shape, head_width):
    return lax.broadcasted_iota(jnp.int32, shape, len(shape) - 1) // head_width


def _stack_heads(t, head_width):
    lane_head = _head_of_lane(t.shape, head_width)
    zero = jnp.zeros_like(t)
    return jnp.concatenate([jnp.where(lane_head == h, t, zero) for h in range(HEADS)], axis=0)


def _stack_heads_paged(t):
    n = t.shape[0]
    per_page = LANES // HEAD_DIM
    first = _head_of_lane((n, LANES), HEAD_DIM)
    zero = jnp.zeros((n, LANES), t.dtype)
    copies = []
    for h in range(HEADS):
        page = h // per_page
        own = jnp.where(first == h % per_page, t[:, page * LANES:(page + 1) * LANES], zero)
        copies.append(jnp.concatenate([own if pg == page else zero for pg in range(GROUP_WIDTH // LANES)], axis=1))
    return jnp.concatenate(copies, axis=0)


def _select_heads_paged(t4, col4=None):
    n = t4.shape[0] // HEADS
    per_page = LANES // HEAD_DIM
    first = _head_of_lane((n, LANES), HEAD_DIM)
    pages = []
    for page in range(GROUP_WIDTH // LANES):
        out = None
        for k in range(per_page - 1, -1, -1):
            h = page * per_page + k
            blk = t4[h * n:(h + 1) * n, page * LANES:(page + 1) * LANES]
            if col4 is not None:
                blk = blk * col4[h * n:(h + 1) * n]
            out = blk if out is None else jnp.where(first == k, blk, out)
        pages.append(out)
    return pages


def _split_dot(x, w_bf16, terms):
    acc = None
    for _ in range(terms):
        piece = x.astype(BF16)
        part = _dot(piece, w_bf16)
        acc = part if acc is None else acc + part
        x = x - piece.astype(F32)
    return acc


def _head_norm(o, avg_bf16):
    d = o - _split_dot(o, avg_bf16, 2)
    var = _split_dot(d * d, avg_bf16, 2)
    return d * lax.rsqrt(var + LN_EPS)


def _head_avg_matrix(width, head_width):
    r = lax.broadcasted_iota(jnp.int32, (width, width), 0) // head_width
    c = lax.broadcasted_iota(jnp.int32, (width, width), 1) // head_width
    return jnp.where(r == c, 1.0 / head_width, 0.0).astype(BF16)


CAST_BLOCK_BYTES = 4 * 1024 * 1024


def _cast_kernel(w_ref, o_ref):
    o_ref[...] = w_ref[...].astype(BF16)


def _to_bf16(w, layer=None):
    rows, cols = w.shape[-2:]
    tr = rows
    while tr * cols * 4 > CAST_BLOCK_BYTES and tr % 16 == 0:
        tr //= 2
    if layer is None:
        in_spec = pl.BlockSpec((tr, cols), lambda i: (i, 0))
    else:
        in_spec = pl.BlockSpec((None, tr, cols), lambda i: (layer, i, 0))
    return pl.pallas_call(
        _cast_kernel,
        grid=(rows // tr,),
        in_specs=[in_spec],
        out_specs=pl.BlockSpec((tr, cols), lambda i: (i, 0)),
        out_shape=jax.ShapeDtypeStruct((rows, cols), BF16),
        compiler_params=_cparams(1),
        name="weight_cast",
    )(w)


def _inproj_kernel(x_ref, w_ref, pa_ref, pb_ref, suv_ref, pf_ref, su_scr):
    xb = x_ref[...].astype(BF16)
    step = 256
    for j in range(0, PA_COLS, step):
        _store_pages(pa_ref, j // LANES, _dot(xb, w_ref[:, j:j + step]))
    for j in range(0, PB_COLS, step):
        pb_ref[:, j:j + step] = _dot(xb, w_ref[:, PA_COLS + j:PA_COLS + j + step]).astype(BF16)
    off = PA_COLS + PB_COLS
    _store_pages(su_scr, 0, _dot(xb, w_ref[:, off:off + GROUP_WIDTH]))
    nrow = su_scr.shape[1] // S5_T
    for page in range(GROUP_WIDTH // LANES):
        for s in range(S5_T):
            suv_ref[page, :, s * LANES:(s + 1) * LANES] = su_scr[page, pl.ds(s, nrow, stride=S5_T), :]
    pf_ref[...] = _dot(xb, w_ref[:, off + GROUP_WIDTH:])


def _inproj(x2, w_p, tm=512):
    n = x2.shape[0]
    return pl.pallas_call(
        _inproj_kernel,
        grid=(n // tm,),
        in_specs=[pl.BlockSpec((tm, D_MODEL), lambda i: (i, 0)),
                  pl.BlockSpec((D_MODEL, W_IN_COLS), lambda i: (0, 0))],
        out_specs=[pl.BlockSpec((PA_COLS // LANES, tm, LANES), lambda i: (0, i, 0)),
                   pl.BlockSpec((tm, PB_COLS), lambda i: (i, 0)),
                   pl.BlockSpec((GROUP_WIDTH // LANES, tm // S5_T, S5_T * LANES), lambda i: (0, i, 0)),
                   pl.BlockSpec((tm, PF_COLS), lambda i: (i, 0))],
        out_shape=[jax.ShapeDtypeStruct((PA_COLS // LANES, n, LANES), F32),
                   jax.ShapeDtypeStruct((n, PB_COLS), BF16),
                   jax.ShapeDtypeStruct((GROUP_WIDTH // LANES, n // S5_T, S5_T * LANES), F32),
                   jax.ShapeDtypeStruct((n, PF_COLS), F32)],
        scratch_shapes=[pltpu.VMEM((GROUP_WIDTH // LANES, tm, LANES), F32)],
        compiler_params=_cparams(1),
        name="inproj",
    )(x2, w_p)


DIL_SPAN = ATTN_BLOCK * max(r for _, r in DIL_BRANCHES)
DIL_GROUP = 2
DIL_GROUP_INTERIOR = {1: 5, 4: 3}


def _dil_tables(dilation):
    blk = ATTN_BLOCK
    shape = (HEADS * blk, 2 * blk)
    row = lax.broadcasted_iota(jnp.int32, shape, 0)
    ki = lax.broadcasted_iota(jnp.int32, shape, 1)
    head = row // blk
    dist = row % blk + blk - ki
    valid = (dist >= 0) & (dist <= blk)
    slope = jnp.where(head == 0, 2.0 ** -2, jnp.where(head == 1, 2.0 ** -4,
                      jnp.where(head == 2, 2.0 ** -6, 2.0 ** -8))).astype(F32)
    alibi = -slope * (dist * dilation).astype(F32) * LOG2_E
    return jnp.where(valid, alibi, NEG_INF), jnp.where(valid & (ki >= blk), alibi, NEG_INF)


def _dil_attend(qs, kwins, vwins, tbs):
    scale = HEAD_DIM ** -0.5 * LOG2_E
    ss = [_dot_nt(_stack_heads_paged((q * scale).astype(BF16)), kwin.astype(BF16))
          for q, kwin in zip(qs, kwins)]
    ms, ps, ls = [], [], []
    for s, tb in zip(ss, tbs):
        s = s + tb
        m = jnp.max(s, axis=-1, keepdims=True)
        p = jnp.exp2(s - m)
        ms.append(m)
        ls.append(jnp.sum(p, axis=-1, keepdims=True))
        ps.append(p.astype(BF16))
    o4s = [_dot(p, vwin.astype(BF16)) for p, vwin in zip(ps, vwins)]
    outs = []
    for o4, m, l in zip(o4s, ms, ls):
        lse4 = jnp.broadcast_to((m + jnp.log2(l)) * LN_2, (o4.shape[0], LANES))
        outs.append((_select_heads_paged(o4, 1.0 / l),
                     _select_heads_paged(jnp.concatenate([lse4] * (GROUP_WIDTH // LANES), axis=1))))
    return outs


def _dil_kernel(q_ref, kp_ref, kc_ref, vp_ref, vc_ref, o_ref, tb_ref, ob_ref, lb_ref):
    blk = ATTN_BLOCK
    span = DIL_SPAN

    @pl.when((pl.program_id(0) == 0) & (pl.program_id(1) == 0))
    def _init_tables():
        for bi, (_, r) in enumerate(DIL_BRANCHES):
            t_any, t_first = _dil_tables(r)
            tb_ref[bi, 0] = t_any
            tb_ref[bi, 1] = t_first

    first = jnp.where(pl.program_id(1) == 0, 1, 0)

    def rows(start, size, r):
        if r == 1:
            return pl.ds(pl.multiple_of(start, blk), size)
        return pl.ds(start, size, stride=r)

    def store(bi, rqs, outs):
        for rq, (o_pages, lse_pages) in zip(rqs, outs):
            for pg, (o, lse) in enumerate(zip(o_pages, lse_pages)):
                ob_ref[2 * bi + pg, rq, :] = o
                lb_ref[2 * bi + pg, rq, :] = lse

    for bi, (_, r) in enumerate(DIL_BRANCHES):
        per_sub = span // (blk * r)
        group = min(r, DIL_GROUP)

        def boundary(g, carry, bi=bi, r=r, group=group):
            rqs = [rows(g * group + u, blk, r) for u in range(group)]
            rps = [rows(g * group + u + span - blk * r, blk, r) for u in range(group)]
            kwins = [jnp.concatenate([_load_pages(kp_ref, 0, rp), _load_pages(kc_ref, 0, rq)], axis=0)
                     for rp, rq in zip(rps, rqs)]
            vwins = [jnp.concatenate([_load_pages(vp_ref, 0, rp), _load_pages(vc_ref, 0, rq)], axis=0)
                     for rp, rq in zip(rps, rqs)]
            tb = tb_ref[bi, first]
            store(bi, rqs, _dil_attend([_load_pages(q_ref, 0, rq) for rq in rqs], kwins, vwins, [tb] * group))
            return carry

        lax.fori_loop(0, r // group, boundary, 0)

        if per_sub > 1:
            group = DIL_GROUP_INTERIOR[r]

            def interior(g, carry, bi=bi, r=r, group=group):
                rqs, rks = [], []
                for u in range(group):
                    idx = g * group + u
                    c = lax.rem(idx, r)
                    j = 1 + lax.div(idx, r)
                    rqs.append(rows(c + blk * r * j, blk, r))
                    rks.append(rows(c + blk * r * (j - 1), 2 * blk, r))
                tb = tb_ref[bi, 0]
                store(bi, rqs, _dil_attend([_load_pages(q_ref, 0, rq) for rq in rqs],
                                           [_load_pages(kc_ref, 0, rk) for rk in rks],
                                           [_load_pages(vc_ref, 0, rk) for rk in rks], [tb] * group))
                return carry

            lax.fori_loop(0, r * (per_sub - 1) // group, interior, 0)

    def combine(i, carry):
        rr = pl.ds(pl.multiple_of(i * blk, blk), blk)
        l1, l4, l16 = (_load_pages(lb_ref, 2 * bi, rr) for bi in range(3))
        mx = jnp.maximum(jnp.maximum(l1, l4), l16)
        e1, e4, e16 = jnp.exp(l1 - mx), jnp.exp(l4 - mx), jnp.exp(l16 - mx)
        o1, o4, o16 = (_load_pages(ob_ref, 2 * bi, rr) for bi in range(3))
        ya = (e1 * o1 + e4 * o4 + e16 * o16) / (e1 + e4 + e16)
        o_ref[rr, :] = ya.astype(BF16)
        return carry

    lax.fori_loop(0, span // blk, combine, 0)


def _dilated_attention(pa, bsz, seq):
    n = bsz * seq
    span = DIL_SPAN
    per_b = seq // span
    w = GROUP_WIDTH
    pages = w // LANES

    def cur(col):
        return pl.BlockSpec((pages, span, LANES), lambda b, i: (col, b * per_b + i, 0))

    def prev(col):
        return pl.BlockSpec((pages, span, LANES), lambda b, i: (col, b * per_b + jnp.maximum(i - 1, 0), 0))

    nbr = len(DIL_BRANCHES)
    return pl.pallas_call(
        _dil_kernel,
        grid=(bsz, per_b),
        in_specs=[cur(0), prev(1), cur(1), prev(2), cur(2)],
        out_specs=pl.BlockSpec((span, w), lambda b, i: (b * per_b + i, 0)),
        out_shape=jax.ShapeDtypeStruct((n, w), BF16),
        scratch_shapes=[pltpu.VMEM((nbr, 2, HEADS * ATTN_BLOCK, 2 * ATTN_BLOCK), F32),
                        pltpu.VMEM((nbr * pages, span, LANES), F32),
                        pltpu.VMEM((nbr * pages, span, LANES), F32)],
        compiler_params=_cparams(2),
        name="dilated_attention",
    )(pa, pa, pa, pa, pa)


def _ret_kernel(q_ref, k_ref, v_ref, g_ref, gn_ref, o_ref, r_ref, dtab_ref, qd_ref, kd_ref, o_scr, *, nchunk):
    c = RET_CHUNK
    width = GROUP_WIDTH
    scale = HEAD_DIM ** -0.5
    log_g = [math.log(1.0 - 2.0 ** (-5.0 - h)) for h in range(HEADS)]

    def head_log_g(head):
        return jnp.where(head == 0, log_g[0], jnp.where(head == 1, log_g[1],
                         jnp.where(head == 2, log_g[2], log_g[3]))).astype(F32)

    @pl.when((pl.program_id(0) == 0) & (pl.program_id(1) == 0))
    def _init_tables():
        qry = lax.broadcasted_iota(jnp.int32, (c, HEADS * c), 0)
        col = lax.broadcasted_iota(jnp.int32, (c, HEADS * c), 1)
        rel = (qry - col % c).astype(F32)
        decay = jnp.exp(jnp.maximum(rel, 0.0) * head_log_g(col // c))
        dtab_ref[...] = jnp.where(rel >= 0, decay, 0.0) * scale
        pos = (lax.broadcasted_iota(jnp.int32, (nchunk * c, width), 0) % c).astype(F32)
        lgl = head_log_g(_head_of_lane((nchunk * c, width), HEAD_DIM))
        qd_ref[...] = jnp.exp((pos + 1.0) * lgl)
        kd_ref[...] = jnp.exp((c - 1.0 - pos) * lgl) * scale

    @pl.when(pl.program_id(1) == 0)
    def _reset_state():
        r_ref[...] = jnp.zeros_like(r_ref)

    rr = lax.broadcasted_iota(jnp.int32, (width, width), 0) // HEAD_DIM
    cc = lax.broadcasted_iota(jnp.int32, (width, width), 1) // HEAD_DIM
    same_head = rr == cc
    chunk_decay = jnp.exp(c * head_log_g(_head_of_lane((1, width), HEAD_DIM)))
    q_dec = (q_ref[...].astype(F32) * qd_ref[...]).astype(BF16)
    k_dec = (k_ref[...].astype(F32) * kd_ref[...]).astype(BF16)

    sls = [slice(ci * c, (ci + 1) * c) for ci in range(nchunk)]
    scores = [_dot_nt(q_ref[sl, :], _stack_heads(k_ref[sl, :], HEAD_DIM)) for sl in sls]
    probs = [(sc * dtab_ref[...]).astype(BF16) for sc in scores]
    inner = [_dot(pr, _stack_heads(v_ref[sl, :], HEAD_DIM)) for pr, sl in zip(probs, sls)]
    upd = [_dot_tn(k_dec[sl, :], v_ref[sl, :]) for sl in sls]
    states = [r_ref[...]]
    for ci in range(nchunk):
        states.append(states[-1] * chunk_decay + jnp.where(same_head, upd[ci], 0.0))
    r_ref[...] = states[nchunk]
    cross = [_dot(q_dec[sl, :], states[ci].astype(BF16)) for ci, sl in enumerate(sls)]
    for ci, sl in enumerate(sls):
        o_scr[sl, :] = inner[ci] + cross[ci]

    o = _head_norm(o_scr[...], _head_avg_matrix(width, HEAD_DIM)) * gn_ref[...]
    g = g_ref[...]
    o_ref[...] = (o * (g * _sigmoid(g))).astype(BF16)


def _retention(pb, pf, gn, bsz, seq, tq=2048):
    n = bsz * seq
    per_b = seq // tq
    w = GROUP_WIDTH
    nchunk = tq // RET_CHUNK

    def tok(col):
        return pl.BlockSpec((tq, w), lambda b, i: (b * per_b + i, col))

    return pl.pallas_call(
        functools.partial(_ret_kernel, nchunk=nchunk),
        grid=(bsz, per_b),
        in_specs=[tok(0), tok(1), tok(2), tok(0), pl.BlockSpec((1, w), lambda b, i: (0, 0))],
        out_specs=tok(0),
        out_shape=jax.ShapeDtypeStruct((n, w), BF16),
        scratch_shapes=[pltpu.VMEM((w, w), F32),
                        pltpu.VMEM((RET_CHUNK, HEADS * RET_CHUNK), F32),
                        pltpu.VMEM((tq, w), F32),
                        pltpu.VMEM((tq, w), F32),
                        pltpu.VMEM((tq, w), F32)],
        compiler_params=_cparams(2),
        name="retention",
    )(pb, pb, pb, pf, gn)


def _gla_kernel(q_ref, k_ref, v_ref, gl_ref, og_ref, wg_ref, bg_ref, gn_ref, o_ref, rt_ref, cum_ref, o_scr,
                *, nchunk):
    c = GLA_CHUNK
    tq = nchunk * c
    kwid = HEADS * GLA_DK
    vwid = GROUP_WIDTH
    scale = GLA_DK ** -0.5

    @pl.when((pl.program_id(0) == 0) & (pl.program_id(1) == 0))
    def _init_tables():
        i = lax.broadcasted_iota(jnp.int32, (2 * c, 2 * c), 0)
        j = lax.broadcasted_iota(jnp.int32, (2 * c, 2 * c), 1)
        cum_ref[...] = jnp.where((i // c == j // c) & (i >= j), 1.0, 0.0).astype(BF16)

    @pl.when(pl.program_id(1) == 0)
    def _reset_state():
        rt_ref[...] = jnp.zeros_like(rt_ref)

    causal = lax.broadcasted_iota(jnp.int32, (c, HEADS * c), 0) >= (
        lax.broadcasted_iota(jnp.int32, (c, HEADS * c), 1) % c)
    rr = lax.broadcasted_iota(jnp.int32, (vwid, kwid), 0) // HEAD_DIM
    cc = lax.broadcasted_iota(jnp.int32, (vwid, kwid), 1) // GLA_DK
    same_head = rr == cc

    z = _dot(gl_ref[...].astype(BF16), wg_ref[...]) + bg_ref[...]
    log_a = (jnp.minimum(z, 0.0) - jnp.log(1.0 + jnp.exp(-jnp.abs(z)))) * (1.0 / GLA_TAU)
    hi = log_a.astype(BF16)
    rest = log_a - hi.astype(F32)
    mid = rest.astype(BF16)
    lo = (rest - mid.astype(F32)).astype(BF16)
    la3 = jnp.concatenate([hi, mid, lo], axis=1)
    sums = jnp.concatenate([_dot(cum_ref[...], la3[p:p + 2 * c, :]) for p in range(0, tq, 2 * c)], axis=0)
    bcum = sums[:, :kwid] + sums[:, kwid:2 * kwid] + sums[:, 2 * kwid:]
    blast = jnp.concatenate([jnp.broadcast_to(bcum[e - 1:e, :], (c, kwid)) for e in range(c, tq + c, c)],
                            axis=0)
    q = q_ref[...].astype(F32)
    k = k_ref[...].astype(F32)
    q_in = (q * scale * jnp.exp(bcum)).astype(BF16)
    k_out = (k * jnp.exp(-bcum)).astype(BF16)
    k_st = (k * jnp.exp(blast - bcum)).astype(BF16)
    chunk_decay = jnp.exp(blast)

    sls = [slice(ci * c, (ci + 1) * c) for ci in range(nchunk)]
    att = [_dot_nt(q_in[sl, :], _stack_heads(k_out[sl, :], GLA_DK)) for sl in sls]
    att = [jnp.where(causal, a, 0.0).astype(BF16) for a in att]
    intra = [_dot(a, _stack_heads(v_ref[sl, :], HEAD_DIM)) for a, sl in zip(att, sls)]
    upd = [_dot_tn(v_ref[sl, :], k_st[sl, :]) for sl in sls]
    states = [rt_ref[...]]
    for ci in range(nchunk):
        states.append(states[-1] * chunk_decay[ci * c:ci * c + 1, :] + jnp.where(same_head, upd[ci], 0.0))
    rt_ref[...] = states[nchunk]
    cross = [_dot_nt(q_in[sl, :], states[ci].astype(BF16)) for ci, sl in enumerate(sls)]
    for ci, sl in enumerate(sls):
        o_scr[sl, :] = intra[ci] + cross[ci]

    o = _head_norm(o_scr[...], _head_avg_matrix(vwid, HEAD_DIM)) * gn_ref[...]
    g = og_ref[...]
    o_ref[...] = (o * (g * _sigmoid(g))).astype(BF16)


def _gla(pb, pf, w_gate_p, b_gate, gn, bsz, seq, tq=2048):
    n = bsz * seq
    per_b = seq // tq
    kwid = HEADS * GLA_DK
    nchunk = tq // GLA_CHUNK

    def tok(width, col):
        return pl.BlockSpec((tq, width), lambda b, i: (b * per_b + i, col))

    def const(shape):
        return pl.BlockSpec(shape, lambda b, i: (0, 0))

    return pl.pallas_call(
        functools.partial(_gla_kernel, nchunk=nchunk),
        grid=(bsz, per_b),
        in_specs=[tok(kwid, 6), tok(kwid, 7), tok(GROUP_WIDTH, 4),
                  tok(kwid, 4), tok(GROUP_WIDTH, 1),
                  const((kwid, kwid)), const((1, kwid)), const((1, GROUP_WIDTH))],
        out_specs=tok(GROUP_WIDTH, 0),
        out_shape=jax.ShapeDtypeStruct((n, GROUP_WIDTH), BF16),
        scratch_shapes=[pltpu.VMEM((GROUP_WIDTH, kwid), F32),
                        pltpu.VMEM((2 * GLA_CHUNK, 2 * GLA_CHUNK), BF16),
                        pltpu.VMEM((tq, GROUP_WIDTH), F32)],
        compiler_params=_cparams(2),
        name="gla",
    )(pb, pb, pb, pf, pf, w_gate_p, b_gate, gn)


S5_PAGES = GROUP_WIDTH // LANES
S5_ROW = S5_T * LANES
S5_NSTATE = S5_GROUPS * S5_STATE // S5_PAGES
S5_SW = 2 * S5_NSTATE


def _s5prep_kernel(ar_ref, ai_ref, ld_ref, arc_ref, aic_ref, ldc_ref, bre_ref, bim_ref, cre_ref, cim_ref,
                   bst_ref, bigk_ref, ccr_ref, a8_ref):
    w = LANES

    def discretise(ar, ai, ld):
        dt = jnp.exp(ld)
        ea = jnp.exp(ar * dt)
        return ea * jnp.cos(ai * dt), ea * jnp.sin(ai * dt)

    def powers(a_re, a_im, n):
        pw = [(jnp.ones_like(a_re), jnp.zeros_like(a_re)), (a_re, a_im)]
        for _ in range(2, n + 1):
            pr, pi = pw[-1]
            pw.append((pr * a_re - pi * a_im, pr * a_im + pi * a_re))
        return pw

    ar, ai = ar_ref[...], ai_ref[...]
    a_re, a_im = discretise(ar, ai, ld_ref[...])
    pw = powers(a_re, a_im, S5_T)
    den = ar * ar + ai * ai
    nr = a_re - 1.0
    f_re = (nr * ar + a_im * ai) / den
    f_im = (a_im * ar - nr * ai) / den
    b_re, b_im = bre_ref[...], bim_ref[...]
    bb_re = f_re * b_re - f_im * b_im
    bb_im = f_re * b_im + f_im * b_re
    c_re, c_im = cre_ref[...], cim_ref[...]
    cmat = jnp.concatenate([c_re, -c_im], axis=0)

    bigk_ref[...] = jnp.zeros_like(bigk_ref)
    for lag in range(S5_T):
        pr, pi = pw[lag]
        lmat = jnp.concatenate([pr * bb_re - pi * bb_im, pr * bb_im + pi * bb_re], axis=1)
        s = S5_T - 1 - lag
        bst_ref[s * w:(s + 1) * w, :] = lmat.astype(BF16)
        m = jnp.dot(lmat, cmat, precision=HIGHEST, preferred_element_type=F32).astype(BF16)
        for s in range(S5_T - lag):
            t = s + lag
            bigk_ref[s * w:(s + 1) * w, t * w:(t + 1) * w] = m

    ac_re, ac_im = discretise(arc_ref[...], aic_ref[...], ldc_ref[...])
    pwc = powers(ac_re, ac_im, S5_T)
    for t in range(S5_T):
        pr, pi = pwc[t + 1]
        ccr_ref[:S5_NSTATE, t * w:(t + 1) * w] = (c_re * pr - c_im * pi).astype(BF16)
        ccr_ref[S5_NSTATE:, t * w:(t + 1) * w] = (-(c_re * pi + c_im * pr)).astype(BF16)

    a8_ref[...] = jnp.concatenate(pw[S5_T], axis=1)


def _s5_prep(a_re, a_im, log_dt, b_re, b_im, c_re, c_im):
    pg, g, p, ch = S5_PAGES, S5_GROUPS // S5_PAGES, S5_STATE, S5_CH
    eye = jnp.eye(g, dtype=bool)
    row = lambda t: t.reshape(pg, 1, g * p)
    col = lambda t: t.reshape(pg, g * p, 1)
    ld = jnp.broadcast_to(log_dt[:, None], (pg * g, p))

    def expand_b(b):
        t = jnp.transpose(b.reshape(pg, g, p, ch), (0, 1, 3, 2))[:, :, :, None, :]
        return jnp.where(eye[None, :, None, :, None], t, 0.0).reshape(pg, g * ch, g * p)

    def expand_c(c):
        t = jnp.transpose(c.reshape(pg, g, ch, p), (0, 1, 3, 2))[:, :, :, None, :]
        return jnp.where(eye[None, :, None, :, None], t, 0.0).reshape(pg, g * p, g * ch)

    sd = lambda shape, dt: jax.ShapeDtypeStruct((pg,) + shape, dt)
    spec = lambda shape: pl.BlockSpec((None,) + shape, lambda i: (i, 0, 0))
    return pl.pallas_call(
        _s5prep_kernel,
        grid=(pg,),
        in_specs=[spec((1, g * p))] * 3 + [spec((g * p, 1))] * 3 + [spec((g * ch, g * p))] * 2
                 + [spec((g * p, g * ch))] * 2,
        out_specs=[spec((S5_ROW, S5_SW)), spec((S5_ROW, S5_ROW)), spec((S5_SW, S5_ROW)), spec((1, S5_SW))],
        out_shape=[sd((S5_ROW, S5_SW), BF16), sd((S5_ROW, S5_ROW), BF16),
                   sd((S5_SW, S5_ROW), BF16), sd((1, S5_SW), F32)],
        compiler_params=_cparams(1),
        name="s5_prep",
    )(row(a_re), row(a_im), row(ld), col(a_re), col(a_im), col(ld),
      expand_b(b_re), expand_b(b_im), expand_c(c_re), expand_c(c_im))


def _s5_kernel(u_ref, bst_ref, bigk_ref, ccr_ref, a8_ref, d_ref, o_ref, v_scr, xs_scr, st_ref, *, tr):
    ns = S5_NSTATE
    w = LANES
    pages = range(S5_PAGES)

    @pl.when(pl.program_id(1) == 0)
    def _reset_state():
        st_ref[...] = jnp.zeros_like(st_ref)

    us = [u_ref[pg] for pg in pages]
    ubs = [u.astype(BF16) for u in us]
    for pg in pages:
        v_scr[pg] = _dot(ubs[pg], bst_ref[pg])

    a_re = [a8_ref[pg, :, :ns] for pg in pages]
    a_im = [a8_ref[pg, :, ns:] for pg in pages]

    def body(r, carry):
        new = []
        for pg in pages:
            xr, xi = carry[2 * pg], carry[2 * pg + 1]
            xs_scr[pg, pl.ds(r, 1), :ns] = xr
            xs_scr[pg, pl.ds(r, 1), ns:] = xi
            vr = v_scr[pg, pl.ds(r, 1), :ns]
            vi = v_scr[pg, pl.ds(r, 1), ns:]
            new += [a_re[pg] * xr - a_im[pg] * xi + vr, a_re[pg] * xi + a_im[pg] * xr + vi]
        return tuple(new)

    init = []
    for pg in pages:
        init += [st_ref[pg, 0:1, :ns], st_ref[pg, 0:1, ns:]]
    fin = lax.fori_loop(0, tr, body, tuple(init), unroll=8)
    for pg in pages:
        st_ref[pg, 0:1, :ns] = fin[2 * pg]
        st_ref[pg, 0:1, ns:] = fin[2 * pg + 1]

    step = 2
    for pg in pages:
        xb = xs_scr[pg].astype(BF16)
        d_row = jnp.concatenate([d_ref[pg]] * step, axis=1)
        for t in range(0, S5_T, step):
            cols = slice(t * w, (t + step) * w)
            y = (_dot(ubs[pg][:, :(t + step) * w], bigk_ref[pg, :(t + step) * w, cols])
                 + _dot(xb, ccr_ref[pg, :, cols]) + d_row * us[pg][:, cols])
            y = 0.5 * y * (1.0 + jnp.tanh(math.sqrt(2.0 / math.pi) * (y + 0.044715 * (y * y * y))))
            for k in range(step):
                o_ref[pg, pl.ds(t + k, tr, stride=S5_T), :] = y[:, k * w:(k + 1) * w]


def _s5(uv, a_re, a_im, log_dt, b_re, b_im, c_re, c_im, d, bsz, tr=512):
    bst, bigk, ccr, a8 = _s5_prep(a_re, a_im, log_dt, b_re, b_im, c_re, c_im)
    pg, rows = uv.shape[:2]
    per_b = rows // bsz // tr
    const = lambda shape: pl.BlockSpec((pg,) + shape, lambda b, i: (0, 0, 0), pipeline_mode=pl.Buffered(1))
    return pl.pallas_call(
        functools.partial(_s5_kernel, tr=tr),
        grid=(bsz, per_b),
        in_specs=[pl.BlockSpec((pg, tr, S5_ROW), lambda b, i: (0, b * per_b + i, 0)),
                  const((S5_ROW, S5_SW)), const((S5_ROW, S5_ROW)), const((S5_SW, S5_ROW)),
                  const((1, S5_SW)), const((1, LANES))],
        out_specs=pl.BlockSpec((pg, tr * S5_T, LANES), lambda b, i: (0, b * per_b + i, 0)),
        out_shape=jax.ShapeDtypeStruct((pg, rows * S5_T, LANES), F32),
        scratch_shapes=[pltpu.VMEM((pg, tr, S5_SW), F32), pltpu.VMEM((pg, tr, S5_SW), F32),
                        pltpu.VMEM((pg, 8, S5_SW), F32)],
        compiler_params=_cparams(2),
        name="s5",
    )(uv, bst, bigk, ccr, a8, d.reshape(pg, 1, LANES))


def _kvproj_kernel(m_ref, w_ref, o_ref):
    mb = m_ref[...].astype(BF16)
    step = 512
    for j in range(0, 2 * D_MODEL, step):
        o_ref[:, j:j + step] = _dot(mb, w_ref[:, j:j + step]).astype(BF16)


def _kvproj(mem2, w_kv, tm=512):
    n = mem2.shape[0]
    return pl.pallas_call(
        _kvproj_kernel,
        grid=(n // tm,),
        in_specs=[pl.BlockSpec((tm, D_MODEL), lambda i: (i, 0)),
                  pl.BlockSpec((D_MODEL, 2 * D_MODEL), lambda i: (0, 0))],
        out_specs=pl.BlockSpec((tm, 2 * D_MODEL), lambda i: (i, 0)),
        out_shape=jax.ShapeDtypeStruct((n, 2 * D_MODEL), BF16),
        compiler_params=_cparams(1),
        name="mem_kv_proj",
    )(mem2, w_kv)


def _mixmem_kernel(x_ref, ya_ref, yb_ref, yc_ref, yd_ref, wglu_ref, bglu_ref, wmix_ref, gmix_ref, bmix_ref,
                   wq_ref, kv_ref, wo_ref, gmem_ref, bmem_ref, out_ref, *, parts):
    hd = MEM_HEAD_DIM
    rows = x_ref.shape[0] // parts
    sls = [slice(part * rows, (part + 1) * rows) for part in range(parts)]
    yds = [_load_pages(yd_ref, 0, sl) for sl in sls]
    gates = [_dot(yd.astype(BF16), wglu_ref[...]) for yd in yds]
    yds = [yd * _sigmoid(g + bglu_ref[...]) for yd, g in zip(yds, gates)]
    ys = [jnp.concatenate([ya_ref[sl, :], yb_ref[sl, :], yc_ref[sl, :], yd.astype(BF16)], axis=1)
          for sl, yd in zip(sls, yds)]
    hs = [_dot(y, wmix_ref[...]) for y in ys]
    xs = [_layer_norm(DEEPNORM_ALPHA * x_ref[sl, :] + h, gmix_ref[...], bmix_ref[...]) for sl, h in zip(sls, hs)]
    qs = [_dot(x.astype(BF16), wq_ref[...]).astype(BF16) for x in xs]
    outs = [[] for _ in sls]
    for h in range(MEM_HEADS):
        k = kv_ref[:, h * hd:(h + 1) * hd]
        v = kv_ref[:, D_MODEL + h * hd:D_MODEL + (h + 1) * hd]
        ss = [_dot_nt(q[:, h * hd:(h + 1) * hd], k) * (hd ** -0.5) for q in qs]
        ps = []
        for s in ss:
            p = jnp.exp(s - jnp.max(s, axis=-1, keepdims=True))
            ps.append((p / jnp.sum(p, axis=-1, keepdims=True)).astype(BF16))
        for o, p in zip(outs, ps):
            o.append(_dot(p, v).astype(BF16))
    hs = [_dot(jnp.concatenate(o, axis=1), wo_ref[...]) for o in outs]
    for sl, x, h in zip(sls, xs, hs):
        out_ref[sl, :] = _layer_norm(DEEPNORM_ALPHA * x + h, gmem_ref[...], bmem_ref[...])


def _mixmem(x2, ya, yb, yc, yd, wglu, bglu, wmix, gmix, bmix, wq, kv, wo, gmem, bmem, bsz, seq, tm=1024, parts=4):
    per_b = seq // tm
    w4 = GROUP_WIDTH
    const = lambda shape: pl.BlockSpec(shape, lambda bi, i: (0, 0), pipeline_mode=pl.Buffered(1))
    tok = lambda width: pl.BlockSpec((tm, width), lambda bi, i: (bi * per_b + i, 0))
    return pl.pallas_call(
        functools.partial(_mixmem_kernel, parts=parts),
        grid=(bsz, per_b),
        in_specs=[tok(D_MODEL), tok(w4), tok(w4), tok(w4),
                  pl.BlockSpec((w4 // LANES, tm, LANES), lambda bi, i: (0, bi * per_b + i, 0)),
                  const((w4, w4)), const((1, w4)), const((D_MODEL, D_MODEL)), const((1, D_MODEL)),
                  const((1, D_MODEL)), const((D_MODEL, D_MODEL)),
                  pl.BlockSpec((MEM_LEN, 2 * D_MODEL), lambda bi, i: (bi, 0)),
                  const((D_MODEL, D_MODEL)), const((1, D_MODEL)), const((1, D_MODEL))],
        out_specs=tok(D_MODEL),
        out_shape=jax.ShapeDtypeStruct(x2.shape, F32),
        compiler_params=_cparams(2),
        name="mix_out_mem_attn",
    )(x2, ya, yb, yc, yd, wglu, bglu, wmix, gmix, bmix, wq, kv, wo, gmem, bmem)


FFN_CHUNK = 256


def _ffn_kernel(x_ref, wg_ref, wu_ref, wd_ref, g_ref, b_ref, out_ref, h_ref, *, parts):
    rows = x_ref.shape[0] // parts
    sls = [slice(p * rows, (p + 1) * rows) for p in range(parts)]
    xbs = [x_ref[sl, :].astype(BF16) for sl in sls]
    for j in range(0, D_FF, FFN_CHUNK):
        gates = [_dot(xb, wg_ref[:, j:j + FFN_CHUNK]) for xb in xbs]
        ups = [_dot(xb, wu_ref[:, j:j + FFN_CHUNK]) for xb in xbs]
        for sl, gate, up in zip(sls, gates, ups):
            h_ref[sl, j:j + FFN_CHUNK] = (gate * _sigmoid(gate) * up).astype(BF16)
    ys = [_dot(h_ref[sl, :], wd_ref[...]) for sl in sls]
    for sl, y in zip(sls, ys):
        out_ref[sl, :] = _layer_norm(DEEPNORM_ALPHA * x_ref[sl, :] + y, g_ref[...], b_ref[...])


def _ffn(x2, wg, wu, wd, g, b, tm=1024, parts=2):
    n = x2.shape[0]
    tok = pl.BlockSpec((tm, D_MODEL), lambda i: (i, 0))
    const = lambda shape: pl.BlockSpec(shape, lambda i: (0, 0), pipeline_mode=pl.Buffered(1))
    return pl.pallas_call(
        functools.partial(_ffn_kernel, parts=parts),
        grid=(n // tm,),
        in_specs=[tok, const((D_MODEL, D_FF)), const((D_MODEL, D_FF)), const((D_FF, D_MODEL)),
                  const((1, D_MODEL)), const((1, D_MODEL))],
        out_specs=tok,
        out_shape=jax.ShapeDtypeStruct(x2.shape, F32),
        scratch_shapes=[pltpu.VMEM((tm, D_FF), BF16)],
        compiler_params=_cparams(1),
        name="ffn",
    )(x2, wg, wu, wd, g, b)


def _permute_w_in(w):
    c = w.shape[0]
    return jnp.concatenate([
        w[:, 0:1536], w[:, 1792:2304],
        w[:, 2576:2832],
        w[:, 1536:1792], w[:, 2320:2576],
        w[:, 2304:2320], jnp.zeros((c, LANES - GLA_GATE_RANK), w.dtype),
    ], axis=1)


def _mixers(x2, bsz, seq, w_in_p, ret_gn_g, gla_w_gate, gla_b_gate, gla_gn_g, s5_a_re, s5_a_im,
            s5_log_dt, s5_b_re, s5_b_im, s5_c_re, s5_c_im, s5_d):
    pa, pb, uv, pf = _inproj(x2, w_in_p)
    ya = _dilated_attention(pa, bsz, seq)
    yb = _retention(pb, pf, ret_gn_g.reshape(1, -1), bsz, seq)
    kwid = HEADS * GLA_DK
    w_gate_p = jnp.concatenate(
        [gla_w_gate, jnp.zeros((kwid - GLA_GATE_RANK, kwid), gla_w_gate.dtype)], axis=0).astype(BF16)
    yc = _gla(pb, pf, w_gate_p, gla_b_gate.reshape(1, -1), gla_gn_g.reshape(1, -1), bsz, seq)
    yd = _s5(uv, s5_a_re, s5_a_im, s5_log_dt, s5_b_re, s5_b_im, s5_c_re, s5_c_im, s5_d, bsz)
    return ya, yb, yc, yd


def kernel(x, mem, w_in, ret_gn_g, gla_w_gate, gla_b_gate, gla_gn_g, s5_a_re, s5_a_im, s5_log_dt, s5_b_re,
           s5_b_im, s5_c_re, s5_c_im, s5_d, s5_w_glu, s5_b_glu, w_mix_out, ln_mix_g, ln_mix_b, w_mem_q,
           w_mem_kv, w_mem_o, ln_mem_g, ln_mem_b, w_ff_gate, w_ff_up, w_ff_down, ln_ff_g, ln_ff_b):
    bsz, seq, dm = x.shape
    assert dm == D_MODEL and mem.shape == (bsz, MEM_LEN, D_MODEL)
    assert seq % DIL_SPAN == 0
    x2 = x.reshape(bsz * seq, dm)
    mem2 = mem.reshape(bsz * MEM_LEN, dm)
    row = lambda t: t.reshape(1, -1)
    for l in range(DEPTH):
        ys = _mixers(x2, bsz, seq, _permute_w_in(_to_bf16(w_in, l)), ret_gn_g[l], gla_w_gate[l], gla_b_gate[l],
                     gla_gn_g[l], s5_a_re[l], s5_a_im[l], s5_log_dt[l], s5_b_re[l], s5_b_im[l],
                     s5_c_re[l], s5_c_im[l], s5_d[l])
        kv = _kvproj(mem2, _to_bf16(w_mem_kv, l))
        x2 = _mixmem(x2, *ys, _to_bf16(s5_w_glu, l), row(s5_b_glu[l]), _to_bf16(w_mix_out, l),
                     row(ln_mix_g[l]), row(ln_mix_b[l]), _to_bf16(w_mem_q, l), kv, _to_bf16(w_mem_o, l),
                     row(ln_mem_g[l]), row(ln_mem_b[l]), bsz, seq)
        x2 = _ffn(x2, _to_bf16(w_ff_gate, l), _to_bf16(w_ff_up, l), _to_bf16(w_ff_down, l),
                  row(ln_ff_g[l]), row(ln_ff_b[l]))
    return x2.reshape(bsz, seq, dm)
```

```python
import functools
import math

import jax
import jax.numpy as jnp
from jax import lax
from jax.experimental import pallas as pl
from jax.experimental.pallas import tpu as pltpu

F32 = jnp.float32
BF16 = jnp.bfloat16
HIGHEST = lax.Precision.HIGHEST
LANES = 128

D_MODEL = 1024
DEPTH = 2
GROUP_WIDTH = 256
HEADS = 4
HEAD_DIM = 64
DIL_BRANCHES = ((128, 1), (512, 4), (2048, 16))
ATTN_BLOCK = 128
RET_CHUNK = 128
GLA_DK = 32
GLA_GATE_RANK = 16
GLA_TAU = 16.0
GLA_CHUNK = 64
S5_CH = 16
S5_GROUPS = 16
S5_STATE = 64
S5_T = 8
MEM_LEN = 256
MEM_HEADS = 4
MEM_HEAD_DIM = 256
D_FF = 2816
DEEPNORM_ALPHA = (2 * DEPTH) ** 0.25
LN_EPS = 1e-5
NEG_INF = -1e30
LOG2_E = math.log2(math.e)
LN_2 = math.log(2.0)

PA_COLS = 768
PB_COLS = 1280
PF_COLS = 640
W_IN_COLS = PA_COLS + PB_COLS + GROUP_WIDTH + PF_COLS

VMEM_LIMIT = 56 * 1024 * 1024


def _cparams(n_axes):
    return pltpu.CompilerParams(dimension_semantics=("arbitrary",) * n_axes,
                                vmem_limit_bytes=VMEM_LIMIT)


def _dot(a, b):
    return jnp.dot(a, b, preferred_element_type=F32)


def _dot_nt(a, b):
    return lax.dot_general(a, b, (((1,), (1,)), ((), ())), preferred_element_type=F32)


def _dot_tn(a, b):
    return lax.dot_general(a, b, (((0,), (0,)), ((), ())), preferred_element_type=F32)


def _store_pages(ref, page, val, rows=slice(None)):
    for k in range(val.shape[1] // LANES):
        ref[page + k, rows, :] = val[:, k * LANES:(k + 1) * LANES]


def _load_pages(ref, page, rows, npages=2):
    return jnp.concatenate([ref[page + k, rows, :] for k in range(npages)], axis=1)


def _layer_norm(z, g, b):
    mu = jnp.mean(z, axis=-1, keepdims=True)
    d = z - mu
    var = jnp.mean(d * d, axis=-1, keepdims=True)
    return d * lax.rsqrt(var + LN_EPS) * g + b


def _sigmoid(x):
    return 1.0 / (1.0 + jnp.exp(-x))


def _head_of_lane(shape, head_width):
    return lax.broadcasted_iota(jnp.int32, shape, len(shape) - 1) // head_width


def _stack_heads(t, head_width):
    lane_head = _head_of_lane(t.shape, head_width)
    zero = jnp.zeros_like(t)
    return jnp.concatenate([jnp.where(lane_head == h, t, zero) for h in range(HEADS)], axis=0)


def _stack_heads_paged(t):
    n = t.shape[0]
    per_page = LANES // HEAD_DIM
    first = _head_of_lane((n, LANES), HEAD_DIM)
    zero = jnp.zeros((n, LANES), t.dtype)
    copies = []
    for h in range(HEADS):
        page = h // per_page
        own = jnp.where(first == h % per_page, t[:, page * LANES:(page + 1) * LANES], zero)
        copies.append(jnp.concatenate([own if pg == page else zero for pg in range(GROUP_WIDTH // LANES)], axis=1))
    return jnp.concatenate(copies, axis=0)


def _select_heads_paged(t4, col4=None):
    n = t4.shape[0] // HEADS
    per_page = LANES // HEAD_DIM
    first = _head_of_lane((n, LANES), HEAD_DIM)
    pages = []
    for page in range(GROUP_WIDTH // LANES):
        out = None
        for k in range(per_page - 1, -1, -1):
            h = page * per_page + k
            blk = t4[h * n:(h + 1) * n, page * LANES:(page + 1) * LANES]
            if col4 is not None:
                blk = blk * col4[h * n:(h + 1) * n]
            out = blk if out is None else jnp.where(first == k, blk, out)
        pages.append(out)
    return pages


def _split_dot(x, w_bf16, terms):
    acc = None
    for _ in range(terms):
        piece = x.astype(BF16)
        part = _dot(piece, w_bf16)
        acc = part if acc is None else acc + part
        x = x - piece.astype(F32)
    return acc


def _head_norm(o, avg_bf16):
    d = o - _split_dot(o, avg_bf16, 2)
    var = _split_dot(d * d, avg_bf16, 2)
    return d * lax.rsqrt(var + LN_EPS)


def _head_avg_matrix(width, head_width):
    r = lax.broadcasted_iota(jnp.int32, (width, width), 0) // head_width
    c = lax.broadcasted_iota(jnp.int32, (width, width), 1) // head_width
    return jnp.where(r == c, 1.0 / head_width, 0.0).astype(BF16)


CAST_BLOCK_BYTES = 4 * 1024 * 1024


def _cast_kernel(w_ref, o_ref):
    o_ref[...] = w_ref[...].astype(BF16)


def _to_bf16(w, layer=None):
    rows, cols = w.shape[-2:]
    tr = rows
    while tr * cols * 4 > CAST_BLOCK_BYTES and tr % 16 == 0:
        tr //= 2
    if layer is None:
        in_spec = pl.BlockSpec((tr, cols), lambda i: (i, 0))
    else:
        in_spec = pl.BlockSpec((None, tr, cols), lambda i: (layer, i, 0))
    return pl.pallas_call(
        _cast_kernel,
        grid=(rows // tr,),
        in_specs=[in_spec],
        out_specs=pl.BlockSpec((tr, cols), lambda i: (i, 0)),
        out_shape=jax.ShapeDtypeStruct((rows, cols), BF16),
        compiler_params=_cparams(1),
        name="weight_cast",
    )(w)


def _inproj_kernel(x_ref, w_ref, pa_ref, pb_ref, suv_ref, pf_ref, su_scr):
    xb = x_ref[...].astype(BF16)
    step = 256
    for j in range(0, PA_COLS, step):
        _store_pages(pa_ref, j // LANES, _dot(xb, w_ref[:, j:j + step]))
    for j in range(0, PB_COLS, step):
        pb_ref[:, j:j + step] = _dot(xb, w_ref[:, PA_COLS + j:PA_COLS + j + step]).astype(BF16)
    off = PA_COLS + PB_COLS
    _store_pages(su_scr, 0, _dot(xb, w_ref[:, off:off + GROUP_WIDTH]))
    nrow = su_scr.shape[1] // S5_T
    for page in range(GROUP_WIDTH // LANES):
        for s in range(S5_T):
            suv_ref[page, :, s * LANES:(s + 1) * LANES] = su_scr[page, pl.ds(s, nrow, stride=S5_T), :]
    pf_ref[...] = _dot(xb, w_ref[:, off + GROUP_WIDTH:])


def _inproj(x2, w_p, tm=512):
    n = x2.shape[0]
    return pl.pallas_call(
        _inproj_kernel,
        grid=(n // tm,),
        in_specs=[pl.BlockSpec((tm, D_MODEL), lambda i: (i, 0)),
                  pl.BlockSpec((D_MODEL, W_IN_COLS), lambda i: (0, 0))],
        out_specs=[pl.BlockSpec((PA_COLS // LANES, tm, LANES), lambda i: (0, i, 0)),
                   pl.BlockSpec((tm, PB_COLS), lambda i: (i, 0)),
                   pl.BlockSpec((GROUP_WIDTH // LANES, tm // S5_T, S5_T * LANES), lambda i: (0, i, 0)),
                   pl.BlockSpec((tm, PF_COLS), lambda i: (i, 0))],
        out_shape=[jax.ShapeDtypeStruct((PA_COLS // LANES, n, LANES), F32),
                   jax.ShapeDtypeStruct((n, PB_COLS), BF16),
                   jax.ShapeDtypeStruct((GROUP_WIDTH // LANES, n // S5_T, S5_T * LANES), F32),
                   jax.ShapeDtypeStruct((n, PF_COLS), F32)],
        scratch_shapes=[pltpu.VMEM((GROUP_WIDTH // LANES, tm, LANES), F32)],
        compiler_params=_cparams(1),
        name="inproj",
    )(x2, w_p)


DIL_SPAN = ATTN_BLOCK * max(r for _, r in DIL_BRANCHES)
DIL_GROUP = 2
DIL_GROUP_INTERIOR = {1: 5, 4: 3}


def _dil_tables(dilation):
    blk = ATTN_BLOCK
    shape = (HEADS * blk, 2 * blk)
    row = lax.broadcasted_iota(jnp.int32, shape, 0)
    ki = lax.broadcasted_iota(jnp.int32, shape, 1)
    head = row // blk
    dist = row % blk + blk - ki
    valid = (dist >= 0) & (dist <= blk)
    slope = jnp.where(head == 0, 2.0 ** -2, jnp.where(head == 1, 2.0 ** -4,
                      jnp.where(head == 2, 2.0 ** -6, 2.0 ** -8))).astype(F32)
    alibi = -slope * (dist * dilation).astype(F32) * LOG2_E
    return jnp.where(valid, alibi, NEG_INF), jnp.where(valid & (ki >= blk), alibi, NEG_INF)


def _dil_attend(qs, kwins, vwins, tbs):
    scale = HEAD_DIM ** -0.5 * LOG2_E
    ss = [_dot_nt(_stack_heads_paged((q * scale).astype(BF16)), kwin.astype(BF16))
          for q, kwin in zip(qs, kwins)]
    ms, ps, ls = [], [], []
    for s, tb in zip(ss, tbs):
        s = s + tb
        m = jnp.max(s, axis=-1, keepdims=True)
        p = jnp.exp2(s - m)
        ms.append(m)
        ls.append(jnp.sum(p, axis=-1, keepdims=True))
        ps.append(p.astype(BF16))
    o4s = [_dot(p, vwin.astype(BF16)) for p, vwin in zip(ps, vwins)]
    outs = []
    for o4, m, l in zip(o4s, ms, ls):
        lse4 = jnp.broadcast_to((m + jnp.log2(l)) * LN_2, (o4.shape[0], LANES))
        outs.append((_select_heads_paged(o4, 1.0 / l),
                     _select_heads_paged(jnp.concatenate([lse4] * (GROUP_WIDTH // LANES), axis=1))))
    return outs


def _dil_kernel(q_ref, kp_ref, kc_ref, vp_ref, vc_ref, o_ref, tb_ref, ob_ref, lb_ref):
    blk = ATTN_BLOCK
    span = DIL_SPAN

    @pl.when((pl.program_id(0) == 0) & (pl.program_id(1) == 0))
    def _init_tables():
        for bi, (_, r) in enumerate(DIL_BRANCHES):
            t_any, t_first = _dil_tables(r)
            tb_ref[bi, 0] = t_any
            tb_ref[bi, 1] = t_first

    first = jnp.where(pl.program_id(1) == 0, 1, 0)

    def rows(start, size, r):
        if r == 1:
            return pl.ds(pl.multiple_of(start, blk), size)
        return pl.ds(start, size, stride=r)

    def store(bi, rqs, outs):
        for rq, (o_pages, lse_pages) in zip(rqs, outs):
            for pg, (o, lse) in enumerate(zip(o_pages, lse_pages)):
                ob_ref[2 * bi + pg, rq, :] = o
                lb_ref[2 * bi + pg, rq, :] = lse

    for bi, (_, r) in enumerate(DIL_BRANCHES):
        per_sub = span // (blk * r)
        group = min(r, DIL_GROUP)

        def boundary(g, carry, bi=bi, r=r, group=group):
            rqs = [rows(g * group + u, blk, r) for u in range(group)]
            rps = [rows(g * group + u + span - blk * r, blk, r) for u in range(group)]
            kwins = [jnp.concatenate([_load_pages(kp_ref, 0, rp), _load_pages(kc_ref, 0, rq)], axis=0)
                     for rp, rq in zip(rps, rqs)]
            vwins = [jnp.concatenate([_load_pages(vp_ref, 0, rp), _load_pages(vc_ref, 0, rq)], axis=0)
                     for rp, rq in zip(rps, rqs)]
            tb = tb_ref[bi, first]
            store(bi, rqs, _dil_attend([_load_pages(q_ref, 0, rq) for rq in rqs], kwins, vwins, [tb] * group))
            return carry

        lax.fori_loop(0, r // group, boundary, 0)

        if per_sub > 1:
            group = DIL_GROUP_INTERIOR[r]

            def interior(g, carry, bi=bi, r=r, group=group):
                rqs, rks = [], []
                for u in range(group):
                    idx = g * group + u
                    c = lax.rem(idx, r)
                    j = 1 + lax.div(idx, r)
                    rqs.append(rows(c + blk * r * j, blk, r))
                    rks.append(rows(c + blk * r * (j - 1), 2 * blk, r))
                tb = tb_ref[bi, 0]
                store(bi, rqs, _dil_attend([_load_pages(q_ref, 0, rq) for rq in rqs],
                                           [_load_pages(kc_ref, 0, rk) for rk in rks],
                                           [_load_pages(vc_ref, 0, rk) for rk in rks], [tb] * group))
                return carry

            lax.fori_loop(0, r * (per_sub - 1) // group, interior, 0)

    def combine(i, carry):
        rr = pl.ds(pl.multiple_of(i * blk, blk), blk)
        l1, l4, l16 = (_load_pages(lb_ref, 2 * bi, rr) for bi in range(3))
        mx = jnp.maximum(jnp.maximum(l1, l4), l16)
        e1, e4, e16 = jnp.exp(l1 - mx), jnp.exp(l4 - mx), jnp.exp(l16 - mx)
        o1, o4, o16 = (_load_pages(ob_ref, 2 * bi, rr) for bi in range(3))
        ya = (e1 * o1 + e4 * o4 + e16 * o16) / (e1 + e4 + e16)
        o_ref[rr, :] = ya.astype(BF16)
        return carry

    lax.fori_loop(0, span // blk, combine, 0)


def _dilated_attention(pa, bsz, seq):
    n = bsz * seq
    span = DIL_SPAN
    per_b = seq // span
    w = GROUP_WIDTH
    pages = w // LANES

    def cur(col):
        return pl.BlockSpec((pages, span, LANES), lambda b, i: (col, b * per_b + i, 0))

    def prev(col):
        return pl.BlockSpec((pages, span, LANES), lambda b, i: (col, b * per_b + jnp.maximum(i - 1, 0), 0))

    nbr = len(DIL_BRANCHES)
    return pl.pallas_call(
        _dil_kernel,
        grid=(bsz, per_b),
        in_specs=[cur(0), prev(1), cur(1), prev(2), cur(2)],
        out_specs=pl.BlockSpec((span, w), lambda b, i: (b * per_b + i, 0)),
        out_shape=jax.ShapeDtypeStruct((n, w), BF16),
        scratch_shapes=[pltpu.VMEM((nbr, 2, HEADS * ATTN_BLOCK, 2 * ATTN_BLOCK), F32),
                        pltpu.VMEM((nbr * pages, span, LANES), F32),
                        pltpu.VMEM((nbr * pages, span, LANES), F32)],
        compiler_params=_cparams(2),
        name="dilated_attention",
    )(pa, pa, pa, pa, pa)


def _ret_kernel(q_ref, k_ref, v_ref, g_ref, gn_ref, o_ref, r_ref, dtab_ref, qd_ref, kd_ref, o_scr, *, nchunk):
    c = RET_CHUNK
    width = GROUP_WIDTH
    scale = HEAD_DIM ** -0.5
    log_g = [math.log(1.0 - 2.0 ** (-5.0 - h)) for h in range(HEADS)]

    def head_log_g(head):
        return jnp.where(head == 0, log_g[0], jnp.where(head == 1, log_g[1],
                         jnp.where(head == 2, log_g[2], log_g[3]))).astype(F32)

    @pl.when((pl.program_id(0) == 0) & (pl.program_id(1) == 0))
    def _init_tables():
        qry = lax.broadcasted_iota(jnp.int32, (c, HEADS * c), 0)
        col = lax.broadcasted_iota(jnp.int32, (c, HEADS * c), 1)
        rel = (qry - col % c).astype(F32)
        decay = jnp.exp(jnp.maximum(rel, 0.0) * head_log_g(col // c))
        dtab_ref[...] = jnp.where(rel >= 0, decay, 0.0) * scale
        pos = (lax.broadcasted_iota(jnp.int32, (nchunk * c, width), 0) % c).astype(F32)
        lgl = head_log_g(_head_of_lane((nchunk * c, width), HEAD_DIM))
        qd_ref[...] = jnp.exp((pos + 1.0) * lgl)
        kd_ref[...] = jnp.exp((c - 1.0 - pos) * lgl) * scale

    @pl.when(pl.program_id(1) == 0)
    def _reset_state():
        r_ref[...] = jnp.zeros_like(r_ref)

    rr = lax.broadcasted_iota(jnp.int32, (width, width), 0) // HEAD_DIM
    cc = lax.broadcasted_iota(jnp.int32, (width, width), 1) // HEAD_DIM
    same_head = rr == cc
    chunk_decay = jnp.exp(c * head_log_g(_head_of_lane((1, width), HEAD_DIM)))
    q_dec = (q_ref[...].astype(F32) * qd_ref[...]).astype(BF16)
    k_dec = (k_ref[...].astype(F32) * kd_ref[...]).astype(BF16)

    sls = [slice(ci * c, (ci + 1) * c) for ci in range(nchunk)]
    scores = [_dot_nt(q_ref[sl, :], _stack_heads(k_ref[sl, :], HEAD_DIM)) for sl in sls]
    probs = [(sc * dtab_ref[...]).astype(BF16) for sc in scores]
    inner = [_dot(pr, _stack_heads(v_ref[sl, :], HEAD_DIM)) for pr, sl in zip(probs, sls)]
    upd = [_dot_tn(k_dec[sl, :], v_ref[sl, :]) for sl in sls]
    states = [r_ref[...]]
    for ci in range(nchunk):
        states.append(states[-1] * chunk_decay + jnp.where(same_head, upd[ci], 0.0))
    r_ref[...] = states[nchunk]
    cross = [_dot(q_dec[sl, :], states[ci].astype(BF16)) for ci, sl in enumerate(sls)]
    for ci, sl in enumerate(sls):
        o_scr[sl, :] = inner[ci] + cross[ci]

    o = _head_norm(o_scr[...], _head_avg_matrix(width, HEAD_DIM)) * gn_ref[...]
    g = g_ref[...]
    o_ref[...] = (o * (g * _sigmoid(g))).astype(BF16)


def _retention(pb, pf, gn, bsz, seq, tq=2048):
    n = bsz * seq
    per_b = seq // tq
    w = GROUP_WIDTH
    nchunk = tq // RET_CHUNK

    def tok(col):
        return pl.BlockSpec((tq, w), lambda b, i: (b * per_b + i, col))

    return pl.pallas_call(
        functools.partial(_ret_kernel, nchunk=nchunk),
        grid=(bsz, per_b),
        in_specs=[tok(0), tok(1), tok(2), tok(0), pl.BlockSpec((1, w), lambda b, i: (0, 0))],
        out_specs=tok(0),
        out_shape=jax.ShapeDtypeStruct((n, w), BF16),
        scratch_shapes=[pltpu.VMEM((w, w), F32),
                        pltpu.VMEM((RET_CHUNK, HEADS * RET_CHUNK), F32),
                        pltpu.VMEM((tq, w), F32),
                        pltpu.VMEM((tq, w), F32),
                        pltpu.VMEM((tq, w), F32)],
        compiler_params=_cparams(2),
        name="retention",
    )(pb, pb, pb, pf, gn)


def _gla_kernel(q_ref, k_ref, v_ref, gl_ref, og_ref, wg_ref, bg_ref, gn_ref, o_ref, rt_ref, cum_ref, o_scr,
                *, nchunk):
    c = GLA_CHUNK
    tq = nchunk * c
    kwid = HEADS * GLA_DK
    vwid = GROUP_WIDTH
    scale = GLA_DK ** -0.5

    @pl.when((pl.program_id(0) == 0) & (pl.program_id(1) == 0))
    def _init_tables():
        i = lax.broadcasted_iota(jnp.int32, (2 * c, 2 * c), 0)
        j = lax.broadcasted_iota(jnp.int32, (2 * c, 2 * c), 1)
        cum_ref[...] = jnp.where((i // c == j // c) & (i >= j), 1.0, 0.0).astype(BF16)

    @pl.when(pl.program_id(1) == 0)
    def _reset_state():
        rt_ref[...] = jnp.zeros_like(rt_ref)

    causal = lax.broadcasted_iota(jnp.int32, (c, HEADS * c), 0) >= (
        lax.broadcasted_iota(jnp.int32, (c, HEADS * c), 1) % c)
    rr = lax.broadcasted_iota(jnp.int32, (vwid, kwid), 0) // HEAD_DIM
    cc = lax.broadcasted_iota(jnp.int32, (vwid, kwid), 1) // GLA_DK
    same_head = rr == cc

    z = _dot(gl_ref[...].astype(BF16), wg_ref[...]) + bg_ref[...]
    log_a = (jnp.minimum(z, 0.0) - jnp.log(1.0 + jnp.exp(-jnp.abs(z)))) * (1.0 / GLA_TAU)
    hi = log_a.astype(BF16)
    rest = log_a - hi.astype(F32)
    mid = rest.astype(BF16)
    lo = (rest - mid.astype(F32)).astype(BF16)
    la3 = jnp.concatenate([hi, mid, lo], axis=1)
    sums = jnp.concatenate([_dot(cum_ref[...], la3[p:p + 2 * c, :]) for p in range(0, tq, 2 * c)], axis=0)
    bcum = sums[:, :kwid] + sums[:, kwid:2 * kwid] + sums[:, 2 * kwid:]
    blast = jnp.concatenate([jnp.broadcast_to(bcum[e - 1:e, :], (c, kwid)) for e in range(c, tq + c, c)],
                            axis=0)
    q = q_ref[...].astype(F32)
    k = k_ref[...].astype(F32)
    q_in = (q * scale * jnp.exp(bcum)).astype(BF16)
    k_out = (k * jnp.exp(-bcum)).astype(BF16)
    k_st = (k * jnp.exp(blast - bcum)).astype(BF16)
    chunk_decay = jnp.exp(blast)

    sls = [slice(ci * c, (ci + 1) * c) for ci in range(nchunk)]
    att = [_dot_nt(q_in[sl, :], _stack_heads(k_out[sl, :], GLA_DK)) for sl in sls]
    att = [jnp.where(causal, a, 0.0).astype(BF16) for a in att]
    intra = [_dot(a, _stack_heads(v_ref[sl, :], HEAD_DIM)) for a, sl in zip(att, sls)]
    upd = [_dot_tn(v_ref[sl, :], k_st[sl, :]) for sl in sls]
    states = [rt_ref[...]]
    for ci in range(nchunk):
        states.append(states[-1] * chunk_decay[ci * c:ci * c + 1, :] + jnp.where(same_head, upd[ci], 0.0))
    rt_ref[...] = states[nchunk]
    cross = [_dot_nt(q_in[sl, :], states[ci].astype(BF16)) for ci, sl in enumerate(sls)]
    for ci, sl in enumerate(sls):
        o_scr[sl, :] = intra[ci] + cross[ci]

    o = _head_norm(o_scr[...], _head_avg_matrix(vwid, HEAD_DIM)) * gn_ref[...]
    g = og_ref[...]
    o_ref[...] = (o * (g * _sigmoid(g))).astype(BF16)


def _gla(pb, pf, w_gate_p, b_gate, gn, bsz, seq, tq=2048):
    n = bsz * seq
    per_b = seq // tq
    kwid = HEADS * GLA_DK
    nchunk = tq // GLA_CHUNK

    def tok(width, col):
        return pl.BlockSpec((tq, width), lambda b, i: (b * per_b + i, col))

    def const(shape):
        return pl.BlockSpec(shape, lambda b, i: (0, 0))

    return pl.pallas_call(
        functools.partial(_gla_kernel, nchunk=nchunk),
        grid=(bsz, per_b),
        in_specs=[tok(kwid, 6), tok(kwid, 7), tok(GROUP_WIDTH, 4),
                  tok(kwid, 4), tok(GROUP_WIDTH, 1),
                  const((kwid, kwid)), const((1, kwid)), const((1, GROUP_WIDTH))],
        out_specs=tok(GROUP_WIDTH, 0),
        out_shape=jax.ShapeDtypeStruct((n, GROUP_WIDTH), BF16),
        scratch_shapes=[pltpu.VMEM((GROUP_WIDTH, kwid), F32),
                        pltpu.VMEM((2 * GLA_CHUNK, 2 * GLA_CHUNK), BF16),
                        pltpu.VMEM((tq, GROUP_WIDTH), F32)],
        compiler_params=_cparams(2),
        name="gla",
    )(pb, pb, pb, pf, pf, w_gate_p, b_gate, gn)


S5_PAGES = GROUP_WIDTH // LANES
S5_ROW = S5_T * LANES
S5_NSTATE = S5_GROUPS * S5_STATE // S5_PAGES
S5_SW = 2 * S5_NSTATE


def _s5prep_kernel(ar_ref, ai_ref, ld_ref, arc_ref, aic_ref, ldc_ref, bre_ref, bim_ref, cre_ref, cim_ref,
                   bst_ref, bigk_ref, ccr_ref, a8_ref):
    w = LANES

    def discretise(ar, ai, ld):
        dt = jnp.exp(ld)
        ea = jnp.exp(ar * dt)
        return ea * jnp.cos(ai * dt), ea * jnp.sin(ai * dt)

    def powers(a_re, a_im, n):
        pw = [(jnp.ones_like(a_re), jnp.zeros_like(a_re)), (a_re, a_im)]
        for _ in range(2, n + 1):
            pr, pi = pw[-1]
            pw.append((pr * a_re - pi * a_im, pr * a_im + pi * a_re))
        return pw

    ar, ai = ar_ref[...], ai_ref[...]
    a_re, a_im = discretise(ar, ai, ld_ref[...])
    pw = powers(a_re, a_im, S5_T)
    den = ar * ar + ai * ai
    nr = a_re - 1.0
    f_re = (nr * ar + a_im * ai) / den
    f_im = (a_im * ar - nr * ai) / den
    b_re, b_im = bre_ref[...], bim_ref[...]
    bb_re = f_re * b_re - f_im * b_im
    bb_im = f_re * b_im + f_im * b_re
    c_re, c_im = cre_ref[...], cim_ref[...]
    cmat = jnp.concatenate([c_re, -c_im], axis=0)

    bigk_ref[...] = jnp.zeros_like(bigk_ref)
    for lag in range(S5_T):
        pr, pi = pw[lag]
        lmat = jnp.concatenate([pr * bb_re - pi * bb_im, pr * bb_im + pi * bb_re], axis=1)
        s = S5_T - 1 - lag
        bst_ref[s * w:(s + 1) * w, :] = lmat.astype(BF16)
        m = jnp.dot(lmat, cmat, precision=HIGHEST, preferred_element_type=F32).astype(BF16)
        for s in range(S5_T - lag):
            t = s + lag
            bigk_ref[s * w:(s + 1) * w, t * w:(t + 1) * w] = m

    ac_re, ac_im = discretise(arc_ref[...], aic_ref[...], ldc_ref[...])
    pwc = powers(ac_re, ac_im, S5_T)
    for t in range(S5_T):
        pr, pi = pwc[t + 1]
        ccr_ref[:S5_NSTATE, t * w:(t + 1) * w] = (c_re * pr - c_im * pi).astype(BF16)
        ccr_ref[S5_NSTATE:, t * w:(t + 1) * w] = (-(c_re * pi + c_im * pr)).astype(BF16)

    a8_ref[...] = jnp.concatenate(pw[S5_T], axis=1)


def _s5_prep(a_re, a_im, log_dt, b_re, b_im, c_re, c_im):
    pg, g, p, ch = S5_PAGES, S5_GROUPS // S5_PAGES, S5_STATE, S5_CH
    eye = jnp.eye(g, dtype=bool)
    row = lambda t: t.reshape(pg, 1, g * p)
    col = lambda t: t.reshape(pg, g * p, 1)
    ld = jnp.broadcast_to(log_dt[:, None], (pg * g, p))

    def expand_b(b):
        t = jnp.transpose(b.reshape(pg, g, p, ch), (0, 1, 3, 2))[:, :, :, None, :]
        return jnp.where(eye[None, :, None, :, None], t, 0.0).reshape(pg, g * ch, g * p)

    def expand_c(c):
        t = jnp.transpose(c.reshape(pg, g, ch, p), (0, 1, 3, 2))[:, :, :, None, :]
        return jnp.where(eye[None, :, None, :, None], t, 0.0).reshape(pg, g * p, g * ch)

    sd = lambda shape, dt: jax.ShapeDtypeStruct((pg,) + shape, dt)
    spec = lambda shape: pl.BlockSpec((None,) + shape, lambda i: (i, 0, 0))
    return pl.pallas_call(
        _s5prep_kernel,
        grid=(pg,),
        in_specs=[spec((1, g * p))] * 3 + [spec((g * p, 1))] * 3 + [spec((g * ch, g * p))] * 2
                 + [spec((g * p, g * ch))] * 2,
        out_specs=[spec((S5_ROW, S5_SW)), spec((S5_ROW, S5_ROW)), spec((S5_SW, S5_ROW)), spec((1, S5_SW))],
        out_shape=[sd((S5_ROW, S5_SW), BF16), sd((S5_ROW, S5_ROW), BF16),
                   sd((S5_SW, S5_ROW), BF16), sd((1, S5_SW), F32)],
        compiler_params=_cparams(1),
        name="s5_prep",
    )(row(a_re), row(a_im), row(ld), col(a_re), col(a_im), col(ld),
      expand_b(b_re), expand_b(b_im), expand_c(c_re), expand_c(c_im))


def _s5_kernel(u_ref, bst_ref, bigk_ref, ccr_ref, a8_ref, d_ref, o_ref, v_scr, xs_scr, st_ref, *, tr):
    ns = S5_NSTATE
    w = LANES
    pages = range(S5_PAGES)

    @pl.when(pl.program_id(1) == 0)
    def _reset_state():
        st_ref[...] = jnp.zeros_like(st_ref)

    us = [u_ref[pg] for pg in pages]
    ubs = [u.astype(BF16) for u in us]
    for pg in pages:
        v_scr[pg] = _dot(ubs[pg], bst_ref[pg])

    a_re = [a8_ref[pg, :, :ns] for pg in pages]
    a_im = [a8_ref[pg, :, ns:] for pg in pages]

    def body(r, carry):
        new = []
        for pg in pages:
            xr, xi = carry[2 * pg], carry[2 * pg + 1]
            xs_scr[pg, pl.ds(r, 1), :ns] = xr
            xs_scr[pg, pl.ds(r, 1), ns:] = xi
            vr = v_scr[pg, pl.ds(r, 1), :ns]
            vi = v_scr[pg, pl.ds(r, 1), ns:]
            new += [a_re[pg] * xr - a_im[pg] * xi + vr, a_re[pg] * xi + a_im[pg] * xr + vi]
        return tuple(new)

    init = []
    for pg in pages:
        init += [st_ref[pg, 0:1, :ns], st_ref[pg, 0:1, ns:]]
    fin = lax.fori_loop(0, tr, body, tuple(init), unroll=8)
    for pg in pages:
        st_ref[pg, 0:1, :ns] = fin[2 * pg]
        st_ref[pg, 0:1, ns:] = fin[2 * pg + 1]

    step = 2
    for pg in pages:
        xb = xs_scr[pg].astype(BF16)
        d_row = jnp.concatenate([d_ref[pg]] * step, axis=1)
        for t in range(0, S5_T, step):
            cols = slice(t * w, (t + step) * w)
            y = (_dot(ubs[pg][:, :(t + step) * w], bigk_ref[pg, :(t + step) * w, cols])
                 + _dot(xb, ccr_ref[pg, :, cols]) + d_row * us[pg][:, cols])
            y = 0.5 * y * (1.0 + jnp.tanh(math.sqrt(2.0 / math.pi) * (y + 0.044715 * (y * y * y))))
            for k in range(step):
                o_ref[pg, pl.ds(t + k, tr, stride=S5_T), :] = y[:, k * w:(k + 1) * w]


def _s5(uv, a_re, a_im, log_dt, b_re, b_im, c_re, c_im, d, bsz, tr=512):
    bst, bigk, ccr, a8 = _s5_prep(a_re, a_im, log_dt, b_re, b_im, c_re, c_im)
    pg, rows = uv.shape[:2]
    per_b = rows // bsz // tr
    const = lambda shape: pl.BlockSpec((pg,) + shape, lambda b, i: (0, 0, 0), pipeline_mode=pl.Buffered(1))
    return pl.pallas_call(
        functools.partial(_s5_kernel, tr=tr),
        grid=(bsz, per_b),
        in_specs=[pl.BlockSpec((pg, tr, S5_ROW), lambda b, i: (0, b * per_b + i, 0)),
                  const((S5_ROW, S5_SW)), const((S5_ROW, S5_ROW)), const((S5_SW, S5_ROW)),
                  const((1, S5_SW)), const((1, LANES))],
        out_specs=pl.BlockSpec((pg, tr * S5_T, LANES), lambda b, i: (0, b * per_b + i, 0)),
        out_shape=jax.ShapeDtypeStruct((pg, rows * S5_T, LANES), F32),
        scratch_shapes=[pltpu.VMEM((pg, tr, S5_SW), F32), pltpu.VMEM((pg, tr, S5_SW), F32),
                        pltpu.VMEM((pg, 8, S5_SW), F32)],
        compiler_params=_cparams(2),
        name="s5",
    )(uv, bst, bigk, ccr, a8, d.reshape(pg, 1, LANES))


def _kvproj_kernel(m_ref, w_ref, o_ref):
    mb = m_ref[...].astype(BF16)
    step = 512
    for j in range(0, 2 * D_MODEL, step):
        o_ref[:, j:j + step] = _dot(mb, w_ref[:, j:j + step]).astype(BF16)


def _kvproj(mem2, w_kv, tm=512):
    n = mem2.shape[0]
    return pl.pallas_call(
        _kvproj_kernel,
        grid=(n // tm,),
        in_specs=[pl.BlockSpec((tm, D_MODEL), lambda i: (i, 0)),
                  pl.BlockSpec((D_MODEL, 2 * D_MODEL), lambda i: (0, 0))],
        out_specs=pl.BlockSpec((tm, 2 * D_MODEL), lambda i: (i, 0)),
        out_shape=jax.ShapeDtypeStruct((n, 2 * D_MODEL), BF16),
        compiler_params=_cparams(1),
        name="mem_kv_proj",
    )(mem2, w_kv)


def _mixmem_kernel(x_ref, ya_ref, yb_ref, yc_ref, yd_ref, wglu_ref, bglu_ref, wmix_ref, gmix_ref, bmix_ref,
                   wq_ref, kv_ref, wo_ref, gmem_ref, bmem_ref, out_ref, *, parts):
    hd = MEM_HEAD_DIM
    rows = x_ref.shape[0] // parts
    sls = [slice(part * rows, (part + 1) * rows) for part in range(parts)]
    yds = [_load_pages(yd_ref, 0, sl) for sl in sls]
    gates = [_dot(yd.astype(BF16), wglu_ref[...]) for yd in yds]
    yds = [yd * _sigmoid(g + bglu_ref[...]) for yd, g in zip(yds, gates)]
    ys = [jnp.concatenate([ya_ref[sl, :], yb_ref[sl, :], yc_ref[sl, :], yd.astype(BF16)], axis=1)
          for sl, yd in zip(sls, yds)]
    hs = [_dot(y, wmix_ref[...]) for y in ys]
    xs = [_layer_norm(DEEPNORM_ALPHA * x_ref[sl, :] + h, gmix_ref[...], bmix_ref[...]) for sl, h in zip(sls, hs)]
    qs = [_dot(x.astype(BF16), wq_ref[...]).astype(BF16) for x in xs]
    outs = [[] for _ in sls]
    for h in range(MEM_HEADS):
        k = kv_ref[:, h * hd:(h + 1) * hd]
        v = kv_ref[:, D_MODEL + h * hd:D_MODEL + (h + 1) * hd]
        ss = [_dot_nt(q[:, h * hd:(h + 1) * hd], k) * (hd ** -0.5) for q in qs]
        ps = []
        for s in ss:
            p = jnp.exp(s - jnp.max(s, axis=-1, keepdims=True))
            ps.append((p / jnp.sum(p, axis=-1, keepdims=True)).astype(BF16))
        for o, p in zip(outs, ps):
            o.append(_dot(p, v).astype(BF16))
    hs = [_dot(jnp.concatenate(o, axis=1), wo_ref[...]) for o in outs]
    for sl, x, h in zip(sls, xs, hs):
        out_ref[sl, :] = _layer_norm(DEEPNORM_ALPHA * x + h, gmem_ref[...], bmem_ref[...])


def _mixmem(x2, ya, yb, yc, yd, wglu, bglu, wmix, gmix, bmix, wq, kv, wo, gmem, bmem, bsz, seq, tm=1024, parts=4):
    per_b = seq // tm
    w4 = GROUP_WIDTH
    const = lambda shape: pl.BlockSpec(shape, lambda bi, i: (0, 0), pipeline_mode=pl.Buffered(1))
    tok = lambda width: pl.BlockSpec((tm, width), lambda bi, i: (bi * per_b + i, 0))
    return pl.pallas_call(
        functools.partial(_mixmem_kernel, parts=parts),
        grid=(bsz, per_b),
        in_specs=[tok(D_MODEL), tok(w4), tok(w4), tok(w4),
                  pl.BlockSpec((w4 // LANES, tm, LANES), lambda bi, i: (0, bi * per_b + i, 0)),
                  const((w4, w4)), const((1, w4)), const((D_MODEL, D_MODEL)), const((1, D_MODEL)),
                  const((1, D_MODEL)), const((D_MODEL, D_MODEL)),
                  pl.BlockSpec((MEM_LEN, 2 * D_MODEL), lambda bi, i: (bi, 0)),
                  const((D_MODEL, D_MODEL)), const((1, D_MODEL)), const((1, D_MODEL))],
        out_specs=tok(D_MODEL),
        out_shape=jax.ShapeDtypeStruct(x2.shape, F32),
        compiler_params=_cparams(2),
        name="mix_out_mem_attn",
    )(x2, ya, yb, yc, yd, wglu, bglu, wmix, gmix, bmix, wq, kv, wo, gmem, bmem)


FFN_CHUNK = 256


def _ffn_kernel(x_ref, wg_ref, wu_ref, wd_ref, g_ref, b_ref, out_ref, h_ref, *, parts):
    rows = x_ref.shape[0] // parts
    for p in range(parts):
        sl = slice(p * rows, (p + 1) * rows)
        xb = x_ref[sl, :].astype(BF16)
        for j in range(0, D_FF, FFN_CHUNK):
            gate = _dot(xb, wg_ref[:, j:j + FFN_CHUNK])
            up = _dot(xb, wu_ref[:, j:j + FFN_CHUNK])
            h_ref[sl, j:j + FFN_CHUNK] = (gate * _sigmoid(gate) * up).astype(BF16)
        y = _dot(h_ref[sl, :], wd_ref[...])
        out_ref[sl, :] = _layer_norm(DEEPNORM_ALPHA * x_ref[sl, :] + y, g_ref[...], b_ref[...])


def _ffn(x2, wg, wu, wd, g, b, tm=1024, parts=4):
    n = x2.shape[0]
    tok = pl.BlockSpec((tm, D_MODEL), lambda i: (i, 0))
    const = lambda shape: pl.BlockSpec(shape, lambda i: (0, 0), pipeline_mode=pl.Buffered(1))
    return pl.pallas_call(
        functools.partial(_ffn_kernel, parts=parts),
        grid=(n // tm,),
        in_specs=[tok, const((D_MODEL, D_FF)), const((D_MODEL, D_FF)), const((D_FF, D_MODEL)),
                  const((1, D_MODEL)), const((1, D_MODEL))],
        out_specs=tok,
        out_shape=jax.ShapeDtypeStruct(x2.shape, F32),
        scratch_shapes=[pltpu.VMEM((tm, D_FF), BF16)],
        compiler_params=_cparams(1),
        name="ffn",
    )(x2, wg, wu, wd, g, b)


def _permute_w_in(w):
    c = w.shape[0]
    return jnp.concatenate([
        w[:, 0:1536], w[:, 1792:2304],
        w[:, 2576:2832],
        w[:, 1536:1792], w[:, 2320:2576],
        w[:, 2304:2320], jnp.zeros((c, LANES - GLA_GATE_RANK), w.dtype),
    ], axis=1)


def _mixers(x2, bsz, seq, w_in_p, ret_gn_g, gla_w_gate, gla_b_gate, gla_gn_g, s5_a_re, s5_a_im,
            s5_log_dt, s5_b_re, s5_b_im, s5_c_re, s5_c_im, s5_d):
    pa, pb, uv, pf = _inproj(x2, w_in_p)
    ya = _dilated_attention(pa, bsz, seq)
    yb = _retention(pb, pf, ret_gn_g.reshape(1, -1), bsz, seq)
    kwid = HEADS * GLA_DK
    w_gate_p = jnp.concatenate(
        [gla_w_gate, jnp.zeros((kwid - GLA_GATE_RANK, kwid), gla_w_gate.dtype)], axis=0).astype(BF16)
    yc = _gla(pb, pf, w_gate_p, gla_b_gate.reshape(1, -1), gla_gn_g.reshape(1, -1), bsz, seq)
    yd = _s5(uv, s5_a_re, s5_a_im, s5_log_dt, s5_b_re, s5_b_im, s5_c_re, s5_c_im, s5_d, bsz)
    return ya, yb, yc, yd


def kernel(x, mem, w_in, ret_gn_g, gla_w_gate, gla_b_gate, gla_gn_g, s5_a_re, s5_a_im, s5_log_dt, s5_b_re,
           s5_b_im, s5_c_re, s5_c_im, s5_d, s5_w_glu, s5_b_glu, w_mix_out, ln_mix_g, ln_mix_b, w_mem_q,
           w_mem_kv, w_mem_o, ln_mem_g, ln_mem_b, w_ff_gate, w_ff_up, w_ff_down, ln_ff_g, ln_ff_b):
    bsz, seq, dm = x.shape
    assert dm == D_MODEL and mem.shape == (bsz, MEM_LEN, D_MODEL)
    assert seq % DIL_SPAN == 0
    x2 = x.reshape(bsz * seq, dm)
    mem2 = mem.reshape(bsz * MEM_LEN, dm)
    row = lambda t: t.reshape(1, -1)
    for l in range(DEPTH):
        ys = _mixers(x2, bsz, seq, _to_bf16(_permute_w_in(w_in[l])), ret_gn_g[l], gla_w_gate[l], gla_b_gate[l],
                     gla_gn_g[l], s5_a_re[l], s5_a_im[l], s5_log_dt[l], s5_b_re[l], s5_b_im[l],
                     s5_c_re[l], s5_c_im[l], s5_d[l])
        kv = _kvproj(mem2, _to_bf16(w_mem_kv, l))
        x2 = _mixmem(x2, *ys, _to_bf16(s5_w_glu, l), row(s5_b_glu[l]), _to_bf16(w_mix_out, l),
                     row(ln_mix_g[l]), row(ln_mix_b[l]), _to_bf16(w_mem_q, l), kv, _to_bf16(w_mem_o, l),
                     row(ln_mem_g[l]), row(ln_mem_b[l]), bsz, seq)
        x2 = _ffn(x2, _to_bf16(w_ff_gate, l), _to_bf16(w_ff_up, l), _to_bf16(w_ff_down, l),
                  row(ln_ff_g[l]), row(ln_ff_b[l]))
    return x2.reshape(bsz, seq, dm)
```

```python
import functools
import math

import jax
import jax.numpy as jnp
from jax import lax
from jax.experimental import pallas as pl
from jax.experimental.pallas import tpu as pltpu

F32 = jnp.float32
BF16 = jnp.bfloat16
HIGHEST = lax.Precision.HIGHEST
LANES = 128

D_MODEL = 1024
DEPTH = 2
GROUP_WIDTH = 256
HEADS = 4
HEAD_DIM = 64
DIL_BRANCHES = ((128, 1), (512, 4), (2048, 16))
ATTN_BLOCK = 128
RET_CHUNK = 128
GLA_DK = 32
GLA_GATE_RANK = 16
GLA_TAU = 16.0
GLA_CHUNK = 64
S5_CH = 16
S5_GROUPS = 16
S5_STATE = 64
S5_T = 8
MEM_LEN = 256
MEM_HEADS = 4
MEM_HEAD_DIM = 256
D_FF = 2816
DEEPNORM_ALPHA = (2 * DEPTH) ** 0.25
LN_EPS = 1e-5
NEG_INF = -1e30
LOG2_E = math.log2(math.e)
LN_2 = math.log(2.0)

PA_COLS = 768
PB_COLS = 1280
PF_COLS = 640
W_IN_COLS = PA_COLS + PB_COLS + GROUP_WIDTH + PF_COLS

VMEM_LIMIT = 56 * 1024 * 1024


def _cparams(n_axes):
    return pltpu.CompilerParams(dimension_semantics=("arbitrary",) * n_axes,
                                vmem_limit_bytes=VMEM_LIMIT)


def _dot(a, b):
    return jnp.dot(a, b, preferred_element_type=F32)


def _dot_nt(a, b):
    return lax.dot_general(a, b, (((1,), (1,)), ((), ())), preferred_element_type=F32)


def _dot_tn(a, b):
    return lax.dot_general(a, b, (((0,), (0,)), ((), ())), preferred_element_type=F32)


def _store_pages(ref, page, val, rows=slice(None)):
    for k in range(val.shape[1] // LANES):
        ref[page + k, rows, :] = val[:, k * LANES:(k + 1) * LANES]


def _load_pages(ref, page, rows, npages=2):
    return jnp.concatenate([ref[page + k, rows, :] for k in range(npages)], axis=1)


def _layer_norm(z, g, b):
    mu = jnp.mean(z, axis=-1, keepdims=True)
    d = z - mu
    var = jnp.mean(d * d, axis=-1, keepdims=True)
    return d * lax.rsqrt(var + LN_EPS) * g + b


def _sigmoid(x):
    return 1.0 / (1.0 + jnp.exp(-x))


def _head_of_lane(shape, head_width):
    return lax.broadcasted_iota(jnp.int32, shape, len(shape) - 1) // head_width


def _stack_heads(t, head_width):
    lane_head = _head_of_lane(t.shape, head_width)
    zero = jnp.zeros_like(t)
    return jnp.concatenate([jnp.where(lane_head == h, t, zero) for h in range(HEADS)], axis=0)


def _stack_heads_paged(t):
    n = t.shape[0]
    per_page = LANES // HEAD_DIM
    first = _head_of_lane((n, LANES), HEAD_DIM)
    zero = jnp.zeros((n, LANES), t.dtype)
    copies = []
    for h in range(HEADS):
        page = h // per_page
        own = jnp.where(first == h % per_page, t[:, page * LANES:(page + 1) * LANES], zero)
        copies.append(jnp.concatenate([own if pg == page else zero for pg in range(GROUP_WIDTH // LANES)], axis=1))
    return jnp.concatenate(copies, axis=0)


def _select_heads_paged(t4, col4=None):
    n = t4.shape[0] // HEADS
    per_page = LANES // HEAD_DIM
    first = _head_of_lane((n, LANES), HEAD_DIM)
    pages = []
    for page in range(GROUP_WIDTH // LANES):
        out = None
        for k in range(per_page - 1, -1, -1):
            h = page * per_page + k
            blk = t4[h * n:(h + 1) * n, page * LANES:(page + 1) * LANES]
            if col4 is not None:
                blk = blk * col4[h * n:(h + 1) * n]
            out = blk if out is None else jnp.where(first == k, blk, out)
        pages.append(out)
    return pages


def _split_dot(x, w_bf16, terms):
    acc = None
    for _ in range(terms):
        piece = x.astype(BF16)
        part = _dot(piece, w_bf16)
        acc = part if acc is None else acc + part
        x = x - piece.astype(F32)
    return acc


def _head_norm(o, avg_bf16):
    d = o - _split_dot(o, avg_bf16, 2)
    var = _split_dot(d * d, avg_bf16, 2)
    return d * lax.rsqrt(var + LN_EPS)


def _head_avg_matrix(width, head_width):
    r = lax.broadcasted_iota(jnp.int32, (width, width), 0) // head_width
    c = lax.broadcasted_iota(jnp.int32, (width, width), 1) // head_width
    return jnp.where(r == c, 1.0 / head_width, 0.0).astype(BF16)


CAST_BLOCK_BYTES = 4 * 1024 * 1024


def _cast_kernel(w_ref, o_ref):
    o_ref[...] = w_ref[...].astype(BF16)


def _to_bf16(w, layer=None):
    rows, cols = w.shape[-2:]
    tr = rows
    while tr * cols * 4 > CAST_BLOCK_BYTES and tr % 16 == 0:
        tr //= 2
    if layer is None:
        in_spec = pl.BlockSpec((tr, cols), lambda i: (i, 0))
    else:
        in_spec = pl.BlockSpec((None, tr, cols), lambda i: (layer, i, 0))
    return pl.pallas_call(
        _cast_kernel,
        grid=(rows // tr,),
        in_specs=[in_spec],
        out_specs=pl.BlockSpec((tr, cols), lambda i: (i, 0)),
        out_shape=jax.ShapeDtypeStruct((rows, cols), BF16),
        compiler_params=_cparams(1),
        name="weight_cast",
    )(w)


def _inproj_kernel(x_ref, w_ref, pa_ref, pb_ref, suv_ref, pf_ref, su_scr):
    xb = x_ref[...].astype(BF16)
    step = 256
    for j in range(0, PA_COLS, step):
        _store_pages(pa_ref, j // LANES, _dot(xb, w_ref[:, j:j + step]))
    for j in range(0, PB_COLS, step):
        pb_ref[:, j:j + step] = _dot(xb, w_ref[:, PA_COLS + j:PA_COLS + j + step]).astype(BF16)
    off = PA_COLS + PB_COLS
    _store_pages(su_scr, 0, _dot(xb, w_ref[:, off:off + GROUP_WIDTH]))
    nrow = su_scr.shape[1] // S5_T
    for page in range(GROUP_WIDTH // LANES):
        for s in range(S5_T):
            suv_ref[page, :, s * LANES:(s + 1) * LANES] = su_scr[page, pl.ds(s, nrow, stride=S5_T), :]
    pf_ref[...] = _dot(xb, w_ref[:, off + GROUP_WIDTH:])


def _inproj(x2, w_p, tm=512):
    n = x2.shape[0]
    return pl.pallas_call(
        _inproj_kernel,
        grid=(n // tm,),
        in_specs=[pl.BlockSpec((tm, D_MODEL), lambda i: (i, 0)),
                  pl.BlockSpec((D_MODEL, W_IN_COLS), lambda i: (0, 0))],
        out_specs=[pl.BlockSpec((PA_COLS // LANES, tm, LANES), lambda i: (0, i, 0)),
                   pl.BlockSpec((tm, PB_COLS), lambda i: (i, 0)),
                   pl.BlockSpec((GROUP_WIDTH // LANES, tm // S5_T, S5_T * LANES), lambda i: (0, i, 0)),
                   pl.BlockSpec((tm, PF_COLS), lambda i: (i, 0))],
        out_shape=[jax.ShapeDtypeStruct((PA_COLS // LANES, n, LANES), F32),
                   jax.ShapeDtypeStruct((n, PB_COLS), BF16),
                   jax.ShapeDtypeStruct((GROUP_WIDTH // LANES, n // S5_T, S5_T * LANES), F32),
                   jax.ShapeDtypeStruct((n, PF_COLS), F32)],
        scratch_shapes=[pltpu.VMEM((GROUP_WIDTH // LANES, tm, LANES), F32)],
        compiler_params=_cparams(1),
        name="inproj",
    )(x2, w_p)


DIL_SPAN = ATTN_BLOCK * max(r for _, r in DIL_BRANCHES)
DIL_GROUP = 2
DIL_GROUP_INTERIOR = {1: 5, 4: 3}


def _dil_tables(dilation):
    blk = ATTN_BLOCK
    shape = (HEADS * blk, 2 * blk)
    row = lax.broadcasted_iota(jnp.int32, shape, 0)
    ki = lax.broadcasted_iota(jnp.int32, shape, 1)
    head = row // blk
    dist = row % blk + blk - ki
    valid = (dist >= 0) & (dist <= blk)
    slope = jnp.where(head == 0, 2.0 ** -2, jnp.where(head == 1, 2.0 ** -4,
                      jnp.where(head == 2, 2.0 ** -6, 2.0 ** -8))).astype(F32)
    alibi = -slope * (dist * dilation).astype(F32) * LOG2_E
    return jnp.where(valid, alibi, NEG_INF), jnp.where(valid & (ki >= blk), alibi, NEG_INF)


def _dil_attend(qs, kwins, vwins, tbs):
    scale = HEAD_DIM ** -0.5 * LOG2_E
    ss = [_dot_nt(_stack_heads_paged((q * scale).astype(BF16)), kwin.astype(BF16))
          for q, kwin in zip(qs, kwins)]
    ms, ps, ls = [], [], []
    for s, tb in zip(ss, tbs):
        s = s + tb
        m = jnp.max(s, axis=-1, keepdims=True)
        p = jnp.exp2(s - m)
        ms.append(m)
        ls.append(jnp.sum(p, axis=-1, keepdims=True))
        ps.append(p.astype(BF16))
    o4s = [_dot(p, vwin.astype(BF16)) for p, vwin in zip(ps, vwins)]
    outs = []
    for o4, m, l in zip(o4s, ms, ls):
        lse4 = jnp.broadcast_to((m + jnp.log2(l)) * LN_2, (o4.shape[0], LANES))
        outs.append((_select_heads_paged(o4, 1.0 / l),
                     _select_heads_paged(jnp.concatenate([lse4] * (GROUP_WIDTH // LANES), axis=1))))
    return outs


def _dil_kernel(q_ref, kp_ref, kc_ref, vp_ref, vc_ref, o_ref, tb_ref, ob_ref, lb_ref):
    blk = ATTN_BLOCK
    span = DIL_SPAN

    @pl.when((pl.program_id(0) == 0) & (pl.program_id(1) == 0))
    def _init_tables():
        for bi, (_, r) in enumerate(DIL_BRANCHES):
            t_any, t_first = _dil_tables(r)
            tb_ref[bi, 0] = t_any
            tb_ref[bi, 1] = t_first

    first = jnp.where(pl.program_id(1) == 0, 1, 0)

    def rows(start, size, r):
        if r == 1:
            return pl.ds(pl.multiple_of(start, blk), size)
        return pl.ds(start, size, stride=r)

    def store(bi, rqs, outs):
        for rq, (o_pages, lse_pages) in zip(rqs, outs):
            for pg, (o, lse) in enumerate(zip(o_pages, lse_pages)):
                ob_ref[2 * bi + pg, rq, :] = o
                lb_ref[2 * bi + pg, rq, :] = lse

    for bi, (_, r) in enumerate(DIL_BRANCHES):
        per_sub = span // (blk * r)
        group = min(r, DIL_GROUP)

        def boundary(g, carry, bi=bi, r=r, group=group):
            rqs = [rows(g * group + u, blk, r) for u in range(group)]
            rps = [rows(g * group + u + span - blk * r, blk, r) for u in range(group)]
            kwins = [jnp.concatenate([_load_pages(kp_ref, 0, rp), _load_pages(kc_ref, 0, rq)], axis=0)
                     for rp, rq in zip(rps, rqs)]
            vwins = [jnp.concatenate([_load_pages(vp_ref, 0, rp), _load_pages(vc_ref, 0, rq)], axis=0)
                     for rp, rq in zip(rps, rqs)]
            tb = tb_ref[bi, first]
            store(bi, rqs, _dil_attend([_load_pages(q_ref, 0, rq) for rq in rqs], kwins, vwins, [tb] * group))
            return carry

        lax.fori_loop(0, r // group, boundary, 0)

        if per_sub > 1:
            group = DIL_GROUP_INTERIOR[r]

            def interior(g, carry, bi=bi, r=r, group=group):
                rqs, rks = [], []
                for u in range(group):
                    idx = g * group + u
                    c = lax.rem(idx, r)
                    j = 1 + lax.div(idx, r)
                    rqs.append(rows(c + blk * r * j, blk, r))
                    rks.append(rows(c + blk * r * (j - 1), 2 * blk, r))
                tb = tb_ref[bi, 0]
                store(bi, rqs, _dil_attend([_load_pages(q_ref, 0, rq) for rq in rqs],
                                           [_load_pages(kc_ref, 0, rk) for rk in rks],
                                           [_load_pages(vc_ref, 0, rk) for rk in rks], [tb] * group))
                return carry

            lax.fori_loop(0, r * (per_sub - 1) // group, interior, 0)

    def combine(i, carry):
        rr = pl.ds(pl.multiple_of(i * blk, blk), blk)
        l1, l4, l16 = (_load_pages(lb_ref, 2 * bi, rr) for bi in range(3))
        mx = jnp.maximum(jnp.maximum(l1, l4), l16)
        e1, e4, e16 = jnp.exp(l1 - mx), jnp.exp(l4 - mx), jnp.exp(l16 - mx)
        o1, o4, o16 = (_load_pages(ob_ref, 2 * bi, rr) for bi in range(3))
        ya = (e1 * o1 + e4 * o4 + e16 * o16) / (e1 + e4 + e16)
        o_ref[rr, :] = ya.astype(BF16)
        return carry

    lax.fori_loop(0, span // blk, combine, 0)


def _dilated_attention(pa, bsz, seq):
    n = bsz * seq
    span = DIL_SPAN
    per_b = seq // span
    w = GROUP_WIDTH
    pages = w // LANES

    def cur(col):
        return pl.BlockSpec((pages, span, LANES), lambda b, i: (col, b * per_b + i, 0))

    def prev(col):
        return pl.BlockSpec((pages, span, LANES), lambda b, i: (col, b * per_b + jnp.maximum(i - 1, 0), 0))

    nbr = len(DIL_BRANCHES)
    return pl.pallas_call(
        _dil_kernel,
        grid=(bsz, per_b),
        in_specs=[cur(0), prev(1), cur(1), prev(2), cur(2)],
        out_specs=pl.BlockSpec((span, w), lambda b, i: (b * per_b + i, 0)),
        out_shape=jax.ShapeDtypeStruct((n, w), BF16),
        scratch_shapes=[pltpu.VMEM((nbr, 2, HEADS * ATTN_BLOCK, 2 * ATTN_BLOCK), F32),
                        pltpu.VMEM((nbr * pages, span, LANES), F32),
                        pltpu.VMEM((nbr * pages, span, LANES), F32)],
        compiler_params=_cparams(2),
        name="dilated_attention",
    )(pa, pa, pa, pa, pa)


def _ret_kernel(q_ref, k_ref, v_ref, g_ref, gn_ref, o_ref, r_ref, dtab_ref, qd_ref, kd_ref, o_scr, *, nchunk):
    c = RET_CHUNK
    width = GROUP_WIDTH
    scale = HEAD_DIM ** -0.5
    log_g = [math.log(1.0 - 2.0 ** (-5.0 - h)) for h in range(HEADS)]

    def head_log_g(head):
        return jnp.where(head == 0, log_g[0], jnp.where(head == 1, log_g[1],
                         jnp.where(head == 2, log_g[2], log_g[3]))).astype(F32)

    @pl.when((pl.program_id(0) == 0) & (pl.program_id(1) == 0))
    def _init_tables():
        qry = lax.broadcasted_iota(jnp.int32, (c, HEADS * c), 0)
        col = lax.broadcasted_iota(jnp.int32, (c, HEADS * c), 1)
        rel = (qry - col % c).astype(F32)
        decay = jnp.exp(jnp.maximum(rel, 0.0) * head_log_g(col // c))
        dtab_ref[...] = jnp.where(rel >= 0, decay, 0.0) * scale
        pos = (lax.broadcasted_iota(jnp.int32, (nchunk * c, width), 0) % c).astype(F32)
        lgl = head_log_g(_head_of_lane((nchunk * c, width), HEAD_DIM))
        qd_ref[...] = jnp.exp((pos + 1.0) * lgl)
        kd_ref[...] = jnp.exp((c - 1.0 - pos) * lgl) * scale

    @pl.when(pl.program_id(1) == 0)
    def _reset_state():
        r_ref[...] = jnp.zeros_like(r_ref)

    rr = lax.broadcasted_iota(jnp.int32, (width, width), 0) // HEAD_DIM
    cc = lax.broadcasted_iota(jnp.int32, (width, width), 1) // HEAD_DIM
    same_head = rr == cc
    chunk_decay = jnp.exp(c * head_log_g(_head_of_lane((1, width), HEAD_DIM)))
    q_dec = (q_ref[...].astype(F32) * qd_ref[...]).astype(BF16)
    k_dec = (k_ref[...].astype(F32) * kd_ref[...]).astype(BF16)

    sls = [slice(ci * c, (ci + 1) * c) for ci in range(nchunk)]
    scores = [_dot_nt(q_ref[sl, :], _stack_heads(k_ref[sl, :], HEAD_DIM)) for sl in sls]
    probs = [(sc * dtab_ref[...]).astype(BF16) for sc in scores]
    inner = [_dot(pr, _stack_heads(v_ref[sl, :], HEAD_DIM)) for pr, sl in zip(probs, sls)]
    upd = [_dot_tn(k_dec[sl, :], v_ref[sl, :]) for sl in sls]
    states = [r_ref[...]]
    for ci in range(nchunk):
        states.append(states[-1] * chunk_decay + jnp.where(same_head, upd[ci], 0.0))
    r_ref[...] = states[nchunk]
    cross = [_dot(q_dec[sl, :], states[ci].astype(BF16)) for ci, sl in enumerate(sls)]
    for ci, sl in enumerate(sls):
        o_scr[sl, :] = inner[ci] + cross[ci]

    o = _head_norm(o_scr[...], _head_avg_matrix(width, HEAD_DIM)) * gn_ref[...]
    g = g_ref[...]
    o_ref[...] = (o * (g * _sigmoid(g))).astype(BF16)


def _retention(pb, pf, gn, bsz, seq, tq=2048):
    n = bsz * seq
    per_b = seq // tq
    w = GROUP_WIDTH
    nchunk = tq // RET_CHUNK

    def tok(col):
        return pl.BlockSpec((tq, w), lambda b, i: (b * per_b + i, col))

    return pl.pallas_call(
        functools.partial(_ret_kernel, nchunk=nchunk),
        grid=(bsz, per_b),
        in_specs=[tok(0), tok(1), tok(2), tok(0), pl.BlockSpec((1, w), lambda b, i: (0, 0))],
        out_specs=tok(0),
        out_shape=jax.ShapeDtypeStruct((n, w), BF16),
        scratch_shapes=[pltpu.VMEM((w, w), F32),
                        pltpu.VMEM((RET_CHUNK, HEADS * RET_CHUNK), F32),
                        pltpu.VMEM((tq, w), F32),
                        pltpu.VMEM((tq, w), F32),
                        pltpu.VMEM((tq, w), F32)],
        compiler_params=_cparams(2),
        name="retention",
    )(pb, pb, pb, pf, gn)


def _gla_kernel(q_ref, k_ref, v_ref, gl_ref, og_ref, wg_ref, bg_ref, gn_ref, o_ref, rt_ref, cum_ref, o_scr,
                *, nchunk):
    c = GLA_CHUNK
    tq = nchunk * c
    kwid = HEADS * GLA_DK
    vwid = GROUP_WIDTH
    scale = GLA_DK ** -0.5

    @pl.when((pl.program_id(0) == 0) & (pl.program_id(1) == 0))
    def _init_tables():
        i = lax.broadcasted_iota(jnp.int32, (2 * c, 2 * c), 0)
        j = lax.broadcasted_iota(jnp.int32, (2 * c, 2 * c), 1)
        cum_ref[...] = jnp.where((i // c == j // c) & (i >= j), 1.0, 0.0).astype(BF16)

    @pl.when(pl.program_id(1) == 0)
    def _reset_state():
        rt_ref[...] = jnp.zeros_like(rt_ref)

    causal = lax.broadcasted_iota(jnp.int32, (c, HEADS * c), 0) >= (
        lax.broadcasted_iota(jnp.int32, (c, HEADS * c), 1) % c)
    rr = lax.broadcasted_iota(jnp.int32, (vwid, kwid), 0) // HEAD_DIM
    cc = lax.broadcasted_iota(jnp.int32, (vwid, kwid), 1) // GLA_DK
    same_head = rr == cc

    z = _dot(gl_ref[...].astype(BF16), wg_ref[...]) + bg_ref[...]
    log_a = (jnp.minimum(z, 0.0) - jnp.log(1.0 + jnp.exp(-jnp.abs(z)))) * (1.0 / GLA_TAU)
    hi = log_a.astype(BF16)
    rest = log_a - hi.astype(F32)
    mid = rest.astype(BF16)
    lo = (rest - mid.astype(F32)).astype(BF16)
    la3 = jnp.concatenate([hi, mid, lo], axis=1)
    sums = jnp.concatenate([_dot(cum_ref[...], la3[p:p + 2 * c, :]) for p in range(0, tq, 2 * c)], axis=0)
    bcum = sums[:, :kwid] + sums[:, kwid:2 * kwid] + sums[:, 2 * kwid:]
    blast = jnp.concatenate([jnp.broadcast_to(bcum[e - 1:e, :], (c, kwid)) for e in range(c, tq + c, c)],
                            axis=0)
    q = q_ref[...].astype(F32)
    k = k_ref[...].astype(F32)
    q_in = (q * scale * jnp.exp(bcum)).astype(BF16)
    k_out = (k * jnp.exp(-bcum)).astype(BF16)
    k_st = (k * jnp.exp(blast - bcum)).astype(BF16)
    chunk_decay = jnp.exp(blast)

    sls = [slice(ci * c, (ci + 1) * c) for ci in range(nchunk)]
    att = [_dot_nt(q_in[sl, :], _stack_heads(k_out[sl, :], GLA_DK)) for sl in sls]
    att = [jnp.where(causal, a, 0.0).astype(BF16) for a in att]
    intra = [_dot(a, _stack_heads(v_ref[sl, :], HEAD_DIM)) for a, sl in zip(att, sls)]
    upd = [_dot_tn(v_ref[sl, :], k_st[sl, :]) for sl in sls]
    states = [rt_ref[...]]
    for ci in range(nchunk):
        states.append(states[-1] * chunk_decay[ci * c:ci * c + 1, :] + jnp.where(same_head, upd[ci], 0.0))
    rt_ref[...] = states[nchunk]
    cross = [_dot_nt(q_in[sl, :], states[ci].astype(BF16)) for ci, sl in enumerate(sls)]
    for ci, sl in enumerate(sls):
        o_scr[sl, :] = intra[ci] + cross[ci]

    o = _head_norm(o_scr[...], _head_avg_matrix(vwid, HEAD_DIM)) * gn_ref[...]
    g = og_ref[...]
    o_ref[...] = (o * (g * _sigmoid(g))).astype(BF16)


def _gla(pb, pf, w_gate_p, b_gate, gn, bsz, seq, tq=2048):
    n = bsz * seq
    per_b = seq // tq
    kwid = HEADS * GLA_DK
    nchunk = tq // GLA_CHUNK

    def tok(width, col):
        return pl.BlockSpec((tq, width), lambda b, i: (b * per_b + i, col))

    def const(shape):
        return pl.BlockSpec(shape, lambda b, i: (0, 0))

    return pl.pallas_call(
        functools.partial(_gla_kernel, nchunk=nchunk),
        grid=(bsz, per_b),
        in_specs=[tok(kwid, 6), tok(kwid, 7), tok(GROUP_WIDTH, 4),
                  tok(kwid, 4), tok(GROUP_WIDTH, 1),
                  const((kwid, kwid)), const((1, kwid)), const((1, GROUP_WIDTH))],
        out_specs=tok(GROUP_WIDTH, 0),
        out_shape=jax.ShapeDtypeStruct((n, GROUP_WIDTH), BF16),
        scratch_shapes=[pltpu.VMEM((GROUP_WIDTH, kwid), F32),
                        pltpu.VMEM((2 * GLA_CHUNK, 2 * GLA_CHUNK), BF16),
                        pltpu.VMEM((tq, GROUP_WIDTH), F32)],
        compiler_params=_cparams(2),
        name="gla",
    )(pb, pb, pb, pf, pf, w_gate_p, b_gate, gn)


S5_PAGES = GROUP_WIDTH // LANES
S5_ROW = S5_T * LANES
S5_NSTATE = S5_GROUPS * S5_STATE // S5_PAGES
S5_SW = 2 * S5_NSTATE


def _s5prep_kernel(ar_ref, ai_ref, ld_ref, arc_ref, aic_ref, ldc_ref, bre_ref, bim_ref, cre_ref, cim_ref,
                   bst_ref, bigk_ref, ccr_ref, a8_ref):
    w = LANES

    def discretise(ar, ai, ld):
        dt = jnp.exp(ld)
        ea = jnp.exp(ar * dt)
        return ea * jnp.cos(ai * dt), ea * jnp.sin(ai * dt)

    def powers(a_re, a_im, n):
        pw = [(jnp.ones_like(a_re), jnp.zeros_like(a_re)), (a_re, a_im)]
        for _ in range(2, n + 1):
            pr, pi = pw[-1]
            pw.append((pr * a_re - pi * a_im, pr * a_im + pi * a_re))
        return pw

    ar, ai = ar_ref[...], ai_ref[...]
    a_re, a_im = discretise(ar, ai, ld_ref[...])
    pw = powers(a_re, a_im, S5_T)
    den = ar * ar + ai * ai
    nr = a_re - 1.0
    f_re = (nr * ar + a_im * ai) / den
    f_im = (a_im * ar - nr * ai) / den
    b_re, b_im = bre_ref[...], bim_ref[...]
    bb_re = f_re * b_re - f_im * b_im
    bb_im = f_re * b_im + f_im * b_re
    c_re, c_im = cre_ref[...], cim_ref[...]
    cmat = jnp.concatenate([c_re, -c_im], axis=0)

    bigk_ref[...] = jnp.zeros_like(bigk_ref)
    for lag in range(S5_T):
        pr, pi = pw[lag]
        lmat = jnp.concatenate([pr * bb_re - pi * bb_im, pr * bb_im + pi * bb_re], axis=1)
        s = S5_T - 1 - lag
        bst_ref[s * w:(s + 1) * w, :] = lmat.astype(BF16)
        m = jnp.dot(lmat, cmat, precision=HIGHEST, preferred_element_type=F32).astype(BF16)
        for s in range(S5_T - lag):
            t = s + lag
            bigk_ref[s * w:(s + 1) * w, t * w:(t + 1) * w] = m

    ac_re, ac_im = discretise(arc_ref[...], aic_ref[...], ldc_ref[...])
    pwc = powers(ac_re, ac_im, S5_T)
    for t in range(S5_T):
        pr, pi = pwc[t + 1]
        ccr_ref[:S5_NSTATE, t * w:(t + 1) * w] = (c_re * pr - c_im * pi).astype(BF16)
        ccr_ref[S5_NSTATE:, t * w:(t + 1) * w] = (-(c_re * pi + c_im * pr)).astype(BF16)

    a8_ref[...] = jnp.concatenate(pw[S5_T], axis=1)


def _s5_prep(a_re, a_im, log_dt, b_re, b_im, c_re, c_im):
    pg, g, p, ch = S5_PAGES, S5_GROUPS // S5_PAGES, S5_STATE, S5_CH
    eye = jnp.eye(g, dtype=bool)
    row = lambda t: t.reshape(pg, 1, g * p)
    col = lambda t: t.reshape(pg, g * p, 1)
    ld = jnp.broadcast_to(log_dt[:, None], (pg * g, p))

    def expand_b(b):
        t = jnp.transpose(b.reshape(pg, g, p, ch), (0, 1, 3, 2))[:, :, :, None, :]
        return jnp.where(eye[None, :, None, :, None], t, 0.0).reshape(pg, g * ch, g * p)

    def expand_c(c):
        t = jnp.transpose(c.reshape(pg, g, ch, p), (0, 1, 3, 2))[:, :, :, None, :]
        return jnp.where(eye[None, :, None, :, None], t, 0.0).reshape(pg, g * p, g * ch)

    sd = lambda shape, dt: jax.ShapeDtypeStruct((pg,) + shape, dt)
    spec = lambda shape: pl.BlockSpec((None,) + shape, lambda i: (i, 0, 0))
    return pl.pallas_call(
        _s5prep_kernel,
        grid=(pg,),
        in_specs=[spec((1, g * p))] * 3 + [spec((g * p, 1))] * 3 + [spec((g * ch, g * p))] * 2
                 + [spec((g * p, g * ch))] * 2,
        out_specs=[spec((S5_ROW, S5_SW)), spec((S5_ROW, S5_ROW)), spec((S5_SW, S5_ROW)), spec((1, S5_SW))],
        out_shape=[sd((S5_ROW, S5_SW), BF16), sd((S5_ROW, S5_ROW), BF16),
                   sd((S5_SW, S5_ROW), BF16), sd((1, S5_SW), F32)],
        compiler_params=_cparams(1),
        name="s5_prep",
    )(row(a_re), row(a_im), row(ld), col(a_re), col(a_im), col(ld),
      expand_b(b_re), expand_b(b_im), expand_c(c_re), expand_c(c_im))


def _s5_kernel(u_ref, bst_ref, bigk_ref, ccr_ref, a8_ref, d_ref, o_ref, v_scr, xs_scr, st_ref, *, tr):
    ns = S5_NSTATE
    w = LANES
    pages = range(S5_PAGES)

    @pl.when(pl.program_id(1) == 0)
    def _reset_state():
        st_ref[...] = jnp.zeros_like(st_ref)

    us = [u_ref[pg] for pg in pages]
    ubs = [u.astype(BF16) for u in us]
    for pg in pages:
        v_scr[pg] = _dot(ubs[pg], bst_ref[pg])

    a_re = [a8_ref[pg, :, :ns] for pg in pages]
    a_im = [a8_ref[pg, :, ns:] for pg in pages]

    def body(r, carry):
        new = []
        for pg in pages:
            xr, xi = carry[2 * pg], carry[2 * pg + 1]
            xs_scr[pg, pl.ds(r, 1), :ns] = xr
            xs_scr[pg, pl.ds(r, 1), ns:] = xi
            vr = v_scr[pg, pl.ds(r, 1), :ns]
            vi = v_scr[pg, pl.ds(r, 1), ns:]
            new += [a_re[pg] * xr - a_im[pg] * xi + vr, a_re[pg] * xi + a_im[pg] * xr + vi]
        return tuple(new)

    init = []
    for pg in pages:
        init += [st_ref[pg, 0:1, :ns], st_ref[pg, 0:1, ns:]]
    fin = lax.fori_loop(0, tr, body, tuple(init), unroll=8)
    for pg in pages:
        st_ref[pg, 0:1, :ns] = fin[2 * pg]
        st_ref[pg, 0:1, ns:] = fin[2 * pg + 1]

    step = 2
    for pg in pages:
        xb = xs_scr[pg].astype(BF16)
        d_row = jnp.concatenate([d_ref[pg]] * step, axis=1)
        for t in range(0, S5_T, step):
            cols = slice(t * w, (t + step) * w)
            y = (_dot(ubs[pg][:, :(t + step) * w], bigk_ref[pg, :(t + step) * w, cols])
                 + _dot(xb, ccr_ref[pg, :, cols]) + d_row * us[pg][:, cols])
            y = 0.5 * y * (1.0 + jnp.tanh(math.sqrt(2.0 / math.pi) * (y + 0.044715 * (y * y * y))))
            for k in range(step):
                o_ref[pg, pl.ds(t + k, tr, stride=S5_T), :] = y[:, k * w:(k + 1) * w]


def _s5(uv, a_re, a_im, log_dt, b_re, b_im, c_re, c_im, d, bsz, tr=512):
    bst, bigk, ccr, a8 = _s5_prep(a_re, a_im, log_dt, b_re, b_im, c_re, c_im)
    pg, rows = uv.shape[:2]
    per_b = rows // bsz // tr
    const = lambda shape: pl.BlockSpec((pg,) + shape, lambda b, i: (0, 0, 0), pipeline_mode=pl.Buffered(1))
    return pl.pallas_call(
        functools.partial(_s5_kernel, tr=tr),
        grid=(bsz, per_b),
        in_specs=[pl.BlockSpec((pg, tr, S5_ROW), lambda b, i: (0, b * per_b + i, 0)),
                  const((S5_ROW, S5_SW)), const((S5_ROW, S5_ROW)), const((S5_SW, S5_ROW)),
                  const((1, S5_SW)), const((1, LANES))],
        out_specs=pl.BlockSpec((pg, tr * S5_T, LANES), lambda b, i: (0, b * per_b + i, 0)),
        out_shape=jax.ShapeDtypeStruct((pg, rows * S5_T, LANES), F32),
        scratch_shapes=[pltpu.VMEM((pg, tr, S5_SW), F32), pltpu.VMEM((pg, tr, S5_SW), F32),
                        pltpu.VMEM((pg, 8, S5_SW), F32)],
        compiler_params=_cparams(2),
        name="s5",
    )(uv, bst, bigk, ccr, a8, d.reshape(pg, 1, LANES))


def _kvproj_kernel(m_ref, w_ref, o_ref):
    mb = m_ref[...].astype(BF16)
    step = 512
    for j in range(0, 2 * D_MODEL, step):
        o_ref[:, j:j + step] = _dot(mb, w_ref[:, j:j + step]).astype(BF16)


def _kvproj(mem2, w_kv, tm=512):
    n = mem2.shape[0]
    return pl.pallas_call(
        _kvproj_kernel,
        grid=(n // tm,),
        in_specs=[pl.BlockSpec((tm, D_MODEL), lambda i: (i, 0)),
                  pl.BlockSpec((D_MODEL, 2 * D_MODEL), lambda i: (0, 0))],
        out_specs=pl.BlockSpec((tm, 2 * D_MODEL), lambda i: (i, 0)),
        out_shape=jax.ShapeDtypeStruct((n, 2 * D_MODEL), BF16),
        compiler_params=_cparams(1),
        name="mem_kv_proj",
    )(mem2, w_kv)


def _mixmem_kernel(x_ref, ya_ref, yb_ref, yc_ref, yd_ref, wglu_ref, bglu_ref, wmix_ref, gmix_ref, bmix_ref,
                   wq_ref, kv_ref, wo_ref, gmem_ref, bmem_ref, out_ref, *, parts):
    hd = MEM_HEAD_DIM
    rows = x_ref.shape[0] // parts
    sls = [slice(part * rows, (part + 1) * rows) for part in range(parts)]
    yds = [_load_pages(yd_ref, 0, sl) for sl in sls]
    gates = [_dot(yd.astype(BF16), wglu_ref[...]) for yd in yds]
    yds = [yd * _sigmoid(g + bglu_ref[...]) for yd, g in zip(yds, gates)]
    ys = [jnp.concatenate([ya_ref[sl, :], yb_ref[sl, :], yc_ref[sl, :], yd.astype(BF16)], axis=1)
          for sl, yd in zip(sls, yds)]
    hs = [_dot(y, wmix_ref[...]) for y in ys]
    xs = [_layer_norm(DEEPNORM_ALPHA * x_ref[sl, :] + h, gmix_ref[...], bmix_ref[...]) for sl, h in zip(sls, hs)]
    qs = [_dot(x.astype(BF16), wq_ref[...]).astype(BF16) for x in xs]
    outs = [[] for _ in sls]
    for h in range(MEM_HEADS):
        k = kv_ref[:, h * hd:(h + 1) * hd]
        v = kv_ref[:, D_MODEL + h * hd:D_MODEL + (h + 1) * hd]
        ss = [_dot_nt(q[:, h * hd:(h + 1) * hd], k) * (hd ** -0.5) for q in qs]
        ps = []
        for s in ss:
            p = jnp.exp(s - jnp.max(s, axis=-1, keepdims=True))
            ps.append((p / jnp.sum(p, axis=-1, keepdims=True)).astype(BF16))
        for o, p in zip(outs, ps):
            o.append(_dot(p, v).astype(BF16))
    hs = [_dot(jnp.concatenate(o, axis=1), wo_ref[...]) for o in outs]
    for sl, x, h in zip(sls, xs, hs):
        out_ref[sl, :] = _layer_norm(DEEPNORM_ALPHA * x + h, gmem_ref[...], bmem_ref[...])


def _mixmem(x2, ya, yb, yc, yd, wglu, bglu, wmix, gmix, bmix, wq, kv, wo, gmem, bmem, bsz, seq, tm=1024, parts=4):
    per_b = seq // tm
    w4 = GROUP_WIDTH
    const = lambda shape: pl.BlockSpec(shape, lambda bi, i: (0, 0), pipeline_mode=pl.Buffered(1))
    tok = lambda width: pl.BlockSpec((tm, width), lambda bi, i: (bi * per_b + i, 0))
    return pl.pallas_call(
        functools.partial(_mixmem_kernel, parts=parts),
        grid=(bsz, per_b),
        in_specs=[tok(D_MODEL), tok(w4), tok(w4), tok(w4),
                  pl.BlockSpec((w4 // LANES, tm, LANES), lambda bi, i: (0, bi * per_b + i, 0)),
                  const((w4, w4)), const((1, w4)), const((D_MODEL, D_MODEL)), const((1, D_MODEL)),
                  const((1, D_MODEL)), const((D_MODEL, D_MODEL)),
                  pl.BlockSpec((MEM_LEN, 2 * D_MODEL), lambda bi, i: (bi, 0)),
                  const((D_MODEL, D_MODEL)), const((1, D_MODEL)), const((1, D_MODEL))],
        out_specs=tok(D_MODEL),
        out_shape=jax.ShapeDtypeStruct(x2.shape, F32),
        compiler_params=_cparams(2),
        name="mix_out_mem_attn",
    )(x2, ya, yb, yc, yd, wglu, bglu, wmix, gmix, bmix, wq, kv, wo, gmem, bmem)


FFN_CHUNK = 256


def _ffn_kernel(x_ref, wg_ref, wu_ref, wd_ref, g_ref, b_ref, out_ref, h_ref, *, parts):
    rows = x_ref.shape[0] // parts
    for p in range(parts):
        sl = slice(p * rows, (p + 1) * rows)
        xb = x_ref[sl, :].astype(BF16)
        for j in range(0, D_FF, FFN_CHUNK):
            gate = _dot(xb, wg_ref[:, j:j + FFN_CHUNK])
            up = _dot(xb, wu_ref[:, j:j + FFN_CHUNK])
            h_ref[sl, j:j + FFN_CHUNK] = (gate * _sigmoid(gate) * up).astype(BF16)
        y = _dot(h_ref[sl, :], wd_ref[...])
        out_ref[sl, :] = _layer_norm(DEEPNORM_ALPHA * x_ref[sl, :] + y, g_ref[...], b_ref[...])


def _ffn(x2, wg, wu, wd, g, b, tm=1024, parts=4):
    n = x2.shape[0]
    tok = pl.BlockSpec((tm, D_MODEL), lambda i: (i, 0))
    const = lambda shape: pl.BlockSpec(shape, lambda i: (0, 0), pipeline_mode=pl.Buffered(1))
    return pl.pallas_call(
        functools.partial(_ffn_kernel, parts=parts),
        grid=(n // tm,),
        in_specs=[tok, const((D_MODEL, D_FF)), const((D_MODEL, D_FF)), const((D_FF, D_MODEL)),
                  const((1, D_MODEL)), const((1, D_MODEL))],
        out_specs=tok,
        out_shape=jax.ShapeDtypeStruct(x2.shape, F32),
        scratch_shapes=[pltpu.VMEM((tm, D_FF), BF16)],
        compiler_params=_cparams(1),
        name="ffn",
    )(x2, wg, wu, wd, g, b)


def _permute_w_in(w):
    c = w.shape[0]
    return jnp.concatenate([
        w[:, 0:1536], w[:, 1792:2304],
        w[:, 2576:2832],
        w[:, 1536:1792], w[:, 2320:2576],
        w[:, 2304:2320], jnp.zeros((c, LANES - GLA_GATE_RANK), w.dtype),
    ], axis=1)


def _mixers(x2, bsz, seq, w_in_p, ret_gn_g, gla_w_gate, gla_b_gate, gla_gn_g, s5_a_re, s5_a_im,
            s5_log_dt, s5_b_re, s5_b_im, s5_c_re, s5_c_im, s5_d):
    pa, pb, uv, pf = _inproj(x2, w_in_p)
    ya = _dilated_attention(pa, bsz, seq)
    yb = _retention(pb, pf, ret_gn_g.reshape(1, -1), bsz, seq)
    kwid = HEADS * GLA_DK
    w_gate_p = jnp.concatenate(
        [gla_w_gate, jnp.zeros((kwid - GLA_GATE_RANK, kwid), gla_w_gate.dtype)], axis=0).astype(BF16)
    yc = _gla(pb, pf, w_gate_p, gla_b_gate.reshape(1, -1), gla_gn_g.reshape(1, -1), bsz, seq)
    yd = _s5(uv, s5_a_re, s5_a_im, s5_log_dt, s5_b_re, s5_b_im, s5_c_re, s5_c_im, s5_d, bsz)
    return ya, yb, yc, yd


def kernel(x, mem, w_in, ret_gn_g, gla_w_gate, gla_b_gate, gla_gn_g, s5_a_re, s5_a_im, s5_log_dt, s5_b_re,
           s5_b_im, s5_c_re, s5_c_im, s5_d, s5_w_glu, s5_b_glu, w_mix_out, ln_mix_g, ln_mix_b, w_mem_q,
           w_mem_kv, w_mem_o, ln_mem_g, ln_mem_b, w_ff_gate, w_ff_up, w_ff_down, ln_ff_g, ln_ff_b):
    bsz, seq, dm = x.shape
    assert dm == D_MODEL and mem.shape == (bsz, MEM_LEN, D_MODEL)
    assert seq % DIL_SPAN == 0
    x2 = x.reshape(bsz * seq, dm)
    mem2 = mem.reshape(bsz * MEM_LEN, dm)
    row = lambda t: t.reshape(1, -1)
    for l in range(DEPTH):
        ys = _mixers(x2, bsz, seq, _permute_w_in(w_in[l]).astype(BF16), ret_gn_g[l], gla_w_gate[l], gla_b_gate[l],
                     gla_gn_g[l], s5_a_re[l], s5_a_im[l], s5_log_dt[l], s5_b_re[l], s5_b_im[l],
                     s5_c_re[l], s5_c_im[l], s5_d[l])
        kv = _kvproj(mem2, _to_bf16(w_mem_kv, l))
        x2 = _mixmem(x2, *ys, _to_bf16(s5_w_glu, l), row(s5_b_glu[l]), _to_bf16(w_mix_out, l),
                     row(ln_mix_g[l]), row(ln_mix_b[l]), _to_bf16(w_mem_q, l), kv, _to_bf16(w_mem_o, l),
                     row(ln_mem_g[l]), row(ln_mem_b[l]), bsz, seq)
        x2 = _ffn(x2, _to_bf16(w_ff_gate, l), _to_bf16(w_ff_up, l), _to_bf16(w_ff_down, l),
                  row(ln_ff_g[l]), row(ln_ff_b[l]))
    return x2.reshape(bsz, seq, dm)
```

```python
import functools
import math

import jax
import jax.numpy as jnp
from jax import lax
from jax.experimental import pallas as pl
from jax.experimental.pallas import tpu as pltpu

F32 = jnp.float32
BF16 = jnp.bfloat16
HIGHEST = lax.Precision.HIGHEST
LANES = 128

D_MODEL = 1024
DEPTH = 2
GROUP_WIDTH = 256
HEADS = 4
HEAD_DIM = 64
DIL_BRANCHES = ((128, 1), (512, 4), (2048, 16))
ATTN_BLOCK = 128
RET_CHUNK = 128
GLA_DK = 32
GLA_GATE_RANK = 16
GLA_TAU = 16.0
GLA_CHUNK = 64
S5_CH = 16
S5_GROUPS = 16
S5_STATE = 64
S5_T = 8
MEM_LEN = 256
MEM_HEADS = 4
MEM_HEAD_DIM = 256
D_FF = 2816
DEEPNORM_ALPHA = (2 * DEPTH) ** 0.25
LN_EPS = 1e-5
NEG_INF = -1e30
LOG2_E = math.log2(math.e)
LN_2 = math.log(2.0)

PA_COLS = 768
PB_COLS = 1280
PF_COLS = 640
W_IN_COLS = PA_COLS + PB_COLS + GROUP_WIDTH + PF_COLS

VMEM_LIMIT = 56 * 1024 * 1024


def _cparams(n_axes):
    return pltpu.CompilerParams(dimension_semantics=("arbitrary",) * n_axes,
                                vmem_limit_bytes=VMEM_LIMIT)


def _dot(a, b):
    return jnp.dot(a, b, preferred_element_type=F32)


def _dot_nt(a, b):
    return lax.dot_general(a, b, (((1,), (1,)), ((), ())), preferred_element_type=F32)


def _dot_tn(a, b):
    return lax.dot_general(a, b, (((0,), (0,)), ((), ())), preferred_element_type=F32)


def _store_pages(ref, page, val, rows=slice(None)):
    for k in range(val.shape[1] // LANES):
        ref[page + k, rows, :] = val[:, k * LANES:(k + 1) * LANES]


def _load_pages(ref, page, rows, npages=2):
    return jnp.concatenate([ref[page + k, rows, :] for k in range(npages)], axis=1)


def _layer_norm(z, g, b):
    mu = jnp.mean(z, axis=-1, keepdims=True)
    d = z - mu
    var = jnp.mean(d * d, axis=-1, keepdims=True)
    return d * lax.rsqrt(var + LN_EPS) * g + b


def _sigmoid(x):
    return 1.0 / (1.0 + jnp.exp(-x))


def _head_of_lane(shape, head_width):
    return lax.broadcasted_iota(jnp.int32, shape, len(shape) - 1) // head_width


def _stack_heads(t, head_width):
    lane_head = _head_of_lane(t.shape, head_width)
    zero = jnp.zeros_like(t)
    return jnp.concatenate([jnp.where(lane_head == h, t, zero) for h in range(HEADS)], axis=0)


def _stack_heads_paged(t):
    n = t.shape[0]
    per_page = LANES // HEAD_DIM
    first = _head_of_lane((n, LANES), HEAD_DIM)
    zero = jnp.zeros((n, LANES), t.dtype)
    copies = []
    for h in range(HEADS):
        page = h // per_page
        own = jnp.where(first == h % per_page, t[:, page * LANES:(page + 1) * LANES], zero)
        copies.append(jnp.concatenate([own if pg == page else zero for pg in range(GROUP_WIDTH // LANES)], axis=1))
    return jnp.concatenate(copies, axis=0)


def _select_heads_paged(t4, col4=None):
    n = t4.shape[0] // HEADS
    per_page = LANES // HEAD_DIM
    first = _head_of_lane((n, LANES), HEAD_DIM)
    pages = []
    for page in range(GROUP_WIDTH // LANES):
        out = None
        for k in range(per_page - 1, -1, -1):
            h = page * per_page + k
            blk = t4[h * n:(h + 1) * n, page * LANES:(page + 1) * LANES]
            if col4 is not None:
                blk = blk * col4[h * n:(h + 1) * n]
            out = blk if out is None else jnp.where(first == k, blk, out)
        pages.append(out)
    return pages


def _split_dot(x, w_bf16, terms):
    acc = None
    for _ in range(terms):
        piece = x.astype(BF16)
        part = _dot(piece, w_bf16)
        acc = part if acc is None else acc + part
        x = x - piece.astype(F32)
    return acc


def _head_norm(o, avg_bf16):
    d = o - _split_dot(o, avg_bf16, 2)
    var = _split_dot(d * d, avg_bf16, 2)
    return d * lax.rsqrt(var + LN_EPS)


def _head_avg_matrix(width, head_width):
    r = lax.broadcasted_iota(jnp.int32, (width, width), 0) // head_width
    c = lax.broadcasted_iota(jnp.int32, (width, width), 1) // head_width
    return jnp.where(r == c, 1.0 / head_width, 0.0).astype(BF16)


CAST_BLOCK_BYTES = 4 * 1024 * 1024


def _cast_kernel(w_ref, o_ref):
    o_ref[...] = w_ref[...].astype(BF16)


def _to_bf16(w, layer=None):
    rows, cols = w.shape[-2:]
    tr = rows
    while tr * cols * 4 > CAST_BLOCK_BYTES and tr % 16 == 0:
        tr //= 2
    if layer is None:
        in_spec = pl.BlockSpec((tr, cols), lambda i: (i, 0))
    else:
        in_spec = pl.BlockSpec((None, tr, cols), lambda i: (layer, i, 0))
    return pl.pallas_call(
        _cast_kernel,
        grid=(rows // tr,),
        in_specs=[in_spec],
        out_specs=pl.BlockSpec((tr, cols), lambda i: (i, 0)),
        out_shape=jax.ShapeDtypeStruct((rows, cols), BF16),
        compiler_params=_cparams(1),
        name="weight_cast",
    )(w)


def _inproj_kernel(x_ref, w_ref, pa_ref, pb_ref, suv_ref, pf_ref, su_scr):
    xb = x_ref[...].astype(BF16)
    step = 256
    for j in range(0, PA_COLS, step):
        _store_pages(pa_ref, j // LANES, _dot(xb, w_ref[:, j:j + step]))
    for j in range(0, PB_COLS, step):
        pb_ref[:, j:j + step] = _dot(xb, w_ref[:, PA_COLS + j:PA_COLS + j + step]).astype(BF16)
    off = PA_COLS + PB_COLS
    _store_pages(su_scr, 0, _dot(xb, w_ref[:, off:off + GROUP_WIDTH]))
    nrow = su_scr.shape[1] // S5_T
    for page in range(GROUP_WIDTH // LANES):
        for s in range(S5_T):
            suv_ref[page, :, s * LANES:(s + 1) * LANES] = su_scr[page, pl.ds(s, nrow, stride=S5_T), :]
    pf_ref[...] = _dot(xb, w_ref[:, off + GROUP_WIDTH:])


def _inproj(x2, w_p, tm=512):
    n = x2.shape[0]
    return pl.pallas_call(
        _inproj_kernel,
        grid=(n // tm,),
        in_specs=[pl.BlockSpec((tm, D_MODEL), lambda i: (i, 0)),
                  pl.BlockSpec((D_MODEL, W_IN_COLS), lambda i: (0, 0))],
        out_specs=[pl.BlockSpec((PA_COLS // LANES, tm, LANES), lambda i: (0, i, 0)),
                   pl.BlockSpec((tm, PB_COLS), lambda i: (i, 0)),
                   pl.BlockSpec((GROUP_WIDTH // LANES, tm // S5_T, S5_T * LANES), lambda i: (0, i, 0)),
                   pl.BlockSpec((tm, PF_COLS), lambda i: (i, 0))],
        out_shape=[jax.ShapeDtypeStruct((PA_COLS // LANES, n, LANES), F32),
                   jax.ShapeDtypeStruct((n, PB_COLS), BF16),
                   jax.ShapeDtypeStruct((GROUP_WIDTH // LANES, n // S5_T, S5_T * LANES), F32),
                   jax.ShapeDtypeStruct((n, PF_COLS), F32)],
        scratch_shapes=[pltpu.VMEM((GROUP_WIDTH // LANES, tm, LANES), F32)],
        compiler_params=_cparams(1),
        name="inproj",
    )(x2, w_p)


DIL_SPAN = ATTN_BLOCK * max(r for _, r in DIL_BRANCHES)
DIL_GROUP = 2
DIL_GROUP_INTERIOR = {1: 5, 4: 3}


def _dil_tables(dilation):
    blk = ATTN_BLOCK
    shape = (HEADS * blk, 2 * blk)
    row = lax.broadcasted_iota(jnp.int32, shape, 0)
    ki = lax.broadcasted_iota(jnp.int32, shape, 1)
    head = row // blk
    dist = row % blk + blk - ki
    valid = (dist >= 0) & (dist <= blk)
    slope = jnp.where(head == 0, 2.0 ** -2, jnp.where(head == 1, 2.0 ** -4,
                      jnp.where(head == 2, 2.0 ** -6, 2.0 ** -8))).astype(F32)
    alibi = -slope * (dist * dilation).astype(F32) * LOG2_E
    return jnp.where(valid, alibi, NEG_INF), jnp.where(valid & (ki >= blk), alibi, NEG_INF)


def _dil_attend(qs, kwins, vwins, tbs):
    scale = HEAD_DIM ** -0.5 * LOG2_E
    ss = [_dot_nt(_stack_heads_paged((q * scale).astype(BF16)), kwin.astype(BF16))
          for q, kwin in zip(qs, kwins)]
    ms, ps, ls = [], [], []
    for s, tb in zip(ss, tbs):
        s = s + tb
        m = jnp.max(s, axis=-1, keepdims=True)
        p = jnp.exp2(s - m)
        ms.append(m)
        ls.append(jnp.sum(p, axis=-1, keepdims=True))
        ps.append(p.astype(BF16))
    o4s = [_dot(p, vwin.astype(BF16)) for p, vwin in zip(ps, vwins)]
    outs = []
    for o4, m, l in zip(o4s, ms, ls):
        lse4 = jnp.broadcast_to((m + jnp.log2(l)) * LN_2, (o4.shape[0], LANES))
        outs.append((_select_heads_paged(o4, 1.0 / l),
                     _select_heads_paged(jnp.concatenate([lse4] * (GROUP_WIDTH // LANES), axis=1))))
    return outs


def _dil_kernel(q_ref, kp_ref, kc_ref, vp_ref, vc_ref, o_ref, tb_ref, ob_ref, lb_ref):
    blk = ATTN_BLOCK
    span = DIL_SPAN

    @pl.when((pl.program_id(0) == 0) & (pl.program_id(1) == 0))
    def _init_tables():
        for bi, (_, r) in enumerate(DIL_BRANCHES):
            t_any, t_first = _dil_tables(r)
            tb_ref[bi, 0] = t_any
            tb_ref[bi, 1] = t_first

    first = jnp.where(pl.program_id(1) == 0, 1, 0)

    def rows(start, size, r):
        if r == 1:
            return pl.ds(pl.multiple_of(start, blk), size)
        return pl.ds(start, size, stride=r)

    def store(bi, rqs, outs):
        for rq, (o_pages, lse_pages) in zip(rqs, outs):
            for pg, (o, lse) in enumerate(zip(o_pages, lse_pages)):
                ob_ref[2 * bi + pg, rq, :] = o
                lb_ref[2 * bi + pg, rq, :] = lse

    for bi, (_, r) in enumerate(DIL_BRANCHES):
        per_sub = span // (blk * r)
        group = min(r, DIL_GROUP)

        def boundary(g, carry, bi=bi, r=r, group=group):
            rqs = [rows(g * group + u, blk, r) for u in range(group)]
            rps = [rows(g * group + u + span - blk * r, blk, r) for u in range(group)]
            kwins = [jnp.concatenate([_load_pages(kp_ref, 0, rp), _load_pages(kc_ref, 0, rq)], axis=0)
                     for rp, rq in zip(rps, rqs)]
            vwins = [jnp.concatenate([_load_pages(vp_ref, 0, rp), _load_pages(vc_ref, 0, rq)], axis=0)
                     for rp, rq in zip(rps, rqs)]
            tb = tb_ref[bi, first]
            store(bi, rqs, _dil_attend([_load_pages(q_ref, 0, rq) for rq in rqs], kwins, vwins, [tb] * group))
            return carry

        lax.fori_loop(0, r // group, boundary, 0)

        if per_sub > 1:
            group = DIL_GROUP_INTERIOR[r]

            def interior(g, carry, bi=bi, r=r, group=group):
                rqs, rks = [], []
                for u in range(group):
                    idx = g * group + u
                    c = lax.rem(idx, r)
                    j = 1 + lax.div(idx, r)
                    rqs.append(rows(c + blk * r * j, blk, r))
                    rks.append(rows(c + blk * r * (j - 1), 2 * blk, r))
                tb = tb_ref[bi, 0]
                store(bi, rqs, _dil_attend([_load_pages(q_ref, 0, rq) for rq in rqs],
                                           [_load_pages(kc_ref, 0, rk) for rk in rks],
                                           [_load_pages(vc_ref, 0, rk) for rk in rks], [tb] * group))
                return carry

            lax.fori_loop(0, r * (per_sub - 1) // group, interior, 0)

    def combine(i, carry):
        rr = pl.ds(pl.multiple_of(i * blk, blk), blk)
        l1, l4, l16 = (_load_pages(lb_ref, 2 * bi, rr) for bi in range(3))
        mx = jnp.maximum(jnp.maximum(l1, l4), l16)
        e1, e4, e16 = jnp.exp(l1 - mx), jnp.exp(l4 - mx), jnp.exp(l16 - mx)
        o1, o4, o16 = (_load_pages(ob_ref, 2 * bi, rr) for bi in range(3))
        ya = (e1 * o1 + e4 * o4 + e16 * o16) / (e1 + e4 + e16)
        o_ref[rr, :] = ya.astype(BF16)
        return carry

    lax.fori_loop(0, span // blk, combine, 0)


def _dilated_attention(pa, bsz, seq):
    n = bsz * seq
    span = DIL_SPAN
    per_b = seq // span
    w = GROUP_WIDTH
    pages = w // LANES

    def cur(col):
        return pl.BlockSpec((pages, span, LANES), lambda b, i: (col, b * per_b + i, 0))

    def prev(col):
        return pl.BlockSpec((pages, span, LANES), lambda b, i: (col, b * per_b + jnp.maximum(i - 1, 0), 0))

    nbr = len(DIL_BRANCHES)
    return pl.pallas_call(
        _dil_kernel,
        grid=(bsz, per_b),
        in_specs=[cur(0), prev(1), cur(1), prev(2), cur(2)],
        out_specs=pl.BlockSpec((span, w), lambda b, i: (b * per_b + i, 0)),
        out_shape=jax.ShapeDtypeStruct((n, w), BF16),
        scratch_shapes=[pltpu.VMEM((nbr, 2, HEADS * ATTN_BLOCK, 2 * ATTN_BLOCK), F32),
                        pltpu.VMEM((nbr * pages, span, LANES), F32),
                        pltpu.VMEM((nbr * pages, span, LANES), F32)],
        compiler_params=_cparams(2),
        name="dilated_attention",
    )(pa, pa, pa, pa, pa)


def _ret_kernel(q_ref, k_ref, v_ref, g_ref, gn_ref, o_ref, r_ref, dtab_ref, qd_ref, kd_ref, o_scr, *, nchunk):
    c = RET_CHUNK
    width = GROUP_WIDTH
    scale = HEAD_DIM ** -0.5
    log_g = [math.log(1.0 - 2.0 ** (-5.0 - h)) for h in range(HEADS)]

    def head_log_g(head):
        return jnp.where(head == 0, log_g[0], jnp.where(head == 1, log_g[1],
                         jnp.where(head == 2, log_g[2], log_g[3]))).astype(F32)

    @pl.when((pl.program_id(0) == 0) & (pl.program_id(1) == 0))
    def _init_tables():
        qry = lax.broadcasted_iota(jnp.int32, (c, HEADS * c), 0)
        col = lax.broadcasted_iota(jnp.int32, (c, HEADS * c), 1)
        rel = (qry - col % c).astype(F32)
        decay = jnp.exp(jnp.maximum(rel, 0.0) * head_log_g(col // c))
        dtab_ref[...] = jnp.where(rel >= 0, decay, 0.0) * scale
        pos = (lax.broadcasted_iota(jnp.int32, (nchunk * c, width), 0) % c).astype(F32)
        lgl = head_log_g(_head_of_lane((nchunk * c, width), HEAD_DIM))
        qd_ref[...] = jnp.exp((pos + 1.0) * lgl)
        kd_ref[...] = jnp.exp((c - 1.0 - pos) * lgl) * scale

    @pl.when(pl.program_id(1) == 0)
    def _reset_state():
        r_ref[...] = jnp.zeros_like(r_ref)

    rr = lax.broadcasted_iota(jnp.int32, (width, width), 0) // HEAD_DIM
    cc = lax.broadcasted_iota(jnp.int32, (width, width), 1) // HEAD_DIM
    same_head = rr == cc
    chunk_decay = jnp.exp(c * head_log_g(_head_of_lane((1, width), HEAD_DIM)))
    q_dec = (q_ref[...].astype(F32) * qd_ref[...]).astype(BF16)
    k_dec = (k_ref[...].astype(F32) * kd_ref[...]).astype(BF16)

    sls = [slice(ci * c, (ci + 1) * c) for ci in range(nchunk)]
    scores = [_dot_nt(q_ref[sl, :], _stack_heads(k_ref[sl, :], HEAD_DIM)) for sl in sls]
    probs = [(sc * dtab_ref[...]).astype(BF16) for sc in scores]
    inner = [_dot(pr, _stack_heads(v_ref[sl, :], HEAD_DIM)) for pr, sl in zip(probs, sls)]
    upd = [_dot_tn(k_dec[sl, :], v_ref[sl, :]) for sl in sls]
    states = [r_ref[...]]
    for ci in range(nchunk):
        states.append(states[-1] * chunk_decay + jnp.where(same_head, upd[ci], 0.0))
    r_ref[...] = states[nchunk]
    cross = [_dot(q_dec[sl, :], states[ci].astype(BF16)) for ci, sl in enumerate(sls)]
    for ci, sl in enumerate(sls):
        o_scr[sl, :] = inner[ci] + cross[ci]

    o = _head_norm(o_scr[...], _head_avg_matrix(width, HEAD_DIM)) * gn_ref[...]
    g = g_ref[...]
    o_ref[...] = (o * (g * _sigmoid(g))).astype(BF16)


def _retention(pb, pf, gn, bsz, seq, tq=2048):
    n = bsz * seq
    per_b = seq // tq
    w = GROUP_WIDTH
    nchunk = tq // RET_CHUNK

    def tok(col):
        return pl.BlockSpec((tq, w), lambda b, i: (b * per_b + i, col))

    return pl.pallas_call(
        functools.partial(_ret_kernel, nchunk=nchunk),
        grid=(bsz, per_b),
        in_specs=[tok(0), tok(1), tok(2), tok(0), pl.BlockSpec((1, w), lambda b, i: (0, 0))],
        out_specs=tok(0),
        out_shape=jax.ShapeDtypeStruct((n, w), BF16),
        scratch_shapes=[pltpu.VMEM((w, w), F32),
                        pltpu.VMEM((RET_CHUNK, HEADS * RET_CHUNK), F32),
                        pltpu.VMEM((tq, w), F32),
                        pltpu.VMEM((tq, w), F32),
                        pltpu.VMEM((tq, w), F32)],
        compiler_params=_cparams(2),
        name="retention",
    )(pb, pb, pb, pf, gn)


def _gla_kernel(q_ref, k_ref, v_ref, gl_ref, og_ref, wg_ref, bg_ref, gn_ref, o_ref, rt_ref, cum_ref, o_scr,
                *, nchunk):
    c = GLA_CHUNK
    tq = nchunk * c
    kwid = HEADS * GLA_DK
    vwid = GROUP_WIDTH
    scale = GLA_DK ** -0.5

    @pl.when((pl.program_id(0) == 0) & (pl.program_id(1) == 0))
    def _init_tables():
        i = lax.broadcasted_iota(jnp.int32, (2 * c, 2 * c), 0)
        j = lax.broadcasted_iota(jnp.int32, (2 * c, 2 * c), 1)
        cum_ref[...] = jnp.where((i // c == j // c) & (i >= j), 1.0, 0.0).astype(BF16)

    @pl.when(pl.program_id(1) == 0)
    def _reset_state():
        rt_ref[...] = jnp.zeros_like(rt_ref)

    causal = lax.broadcasted_iota(jnp.int32, (c, HEADS * c), 0) >= (
        lax.broadcasted_iota(jnp.int32, (c, HEADS * c), 1) % c)
    rr = lax.broadcasted_iota(jnp.int32, (vwid, kwid), 0) // HEAD_DIM
    cc = lax.broadcasted_iota(jnp.int32, (vwid, kwid), 1) // GLA_DK
    same_head = rr == cc

    z = _dot(gl_ref[...].astype(BF16), wg_ref[...]) + bg_ref[...]
    log_a = (jnp.minimum(z, 0.0) - jnp.log(1.0 + jnp.exp(-jnp.abs(z)))) * (1.0 / GLA_TAU)
    hi = log_a.astype(BF16)
    rest = log_a - hi.astype(F32)
    mid = rest.astype(BF16)
    lo = (rest - mid.astype(F32)).astype(BF16)
    la3 = jnp.concatenate([hi, mid, lo], axis=1)
    sums = jnp.concatenate([_dot(cum_ref[...], la3[p:p + 2 * c, :]) for p in range(0, tq, 2 * c)], axis=0)
    bcum = sums[:, :kwid] + sums[:, kwid:2 * kwid] + sums[:, 2 * kwid:]
    blast = jnp.concatenate([jnp.broadcast_to(bcum[e - 1:e, :], (c, kwid)) for e in range(c, tq + c, c)],
                            axis=0)
    q = q_ref[...].astype(F32)
    k = k_ref[...].astype(F32)
    q_in = (q * scale * jnp.exp(bcum)).astype(BF16)
    k_out = (k * jnp.exp(-bcum)).astype(BF16)
    k_st = (k * jnp.exp(blast - bcum)).astype(BF16)
    chunk_decay = jnp.exp(blast)

    sls = [slice(ci * c, (ci + 1) * c) for ci in range(nchunk)]
    att = [_dot_nt(q_in[sl, :], _stack_heads(k_out[sl, :], GLA_DK)) for sl in sls]
    att = [jnp.where(causal, a, 0.0).astype(BF16) for a in att]
    intra = [_dot(a, _stack_heads(v_ref[sl, :], HEAD_DIM)) for a, sl in zip(att, sls)]
    upd = [_dot_tn(v_ref[sl, :], k_st[sl, :]) for sl in sls]
    states = [rt_ref[...]]
    for ci in range(nchunk):
        states.append(states[-1] * chunk_decay[ci * c:ci * c + 1, :] + jnp.where(same_head, upd[ci], 0.0))
    rt_ref[...] = states[nchunk]
    cross = [_dot_nt(q_in[sl, :], states[ci].astype(BF16)) for ci, sl in enumerate(sls)]
    for ci, sl in enumerate(sls):
        o_scr[sl, :] = intra[ci] + cross[ci]

    o = _head_norm(o_scr[...], _head_avg_matrix(vwid, HEAD_DIM)) * gn_ref[...]
    g = og_ref[...]
    o_ref[...] = (o * (g * _sigmoid(g))).astype(BF16)


def _gla(pb, pf, w_gate_p, b_gate, gn, bsz, seq, tq=2048):
    n = bsz * seq
    per_b = seq // tq
    kwid = HEADS * GLA_DK
    nchunk = tq // GLA_CHUNK

    def tok(width, col):
        return pl.BlockSpec((tq, width), lambda b, i: (b * per_b + i, col))

    def const(shape):
        return pl.BlockSpec(shape, lambda b, i: (0, 0))

    return pl.pallas_call(
        functools.partial(_gla_kernel, nchunk=nchunk),
        grid=(bsz, per_b),
        in_specs=[tok(kwid, 6), tok(kwid, 7), tok(GROUP_WIDTH, 4),
                  tok(kwid, 4), tok(GROUP_WIDTH, 1),
                  const((kwid, kwid)), const((1, kwid)), const((1, GROUP_WIDTH))],
        out_specs=tok(GROUP_WIDTH, 0),
        out_shape=jax.ShapeDtypeStruct((n, GROUP_WIDTH), BF16),
        scratch_shapes=[pltpu.VMEM((GROUP_WIDTH, kwid), F32),
                        pltpu.VMEM((2 * GLA_CHUNK, 2 * GLA_CHUNK), BF16),
                        pltpu.VMEM((tq, GROUP_WIDTH), F32)],
        compiler_params=_cparams(2),
        name="gla",
    )(pb, pb, pb, pf, pf, w_gate_p, b_gate, gn)


S5_PAGES = GROUP_WIDTH // LANES
S5_ROW = S5_T * LANES
S5_NSTATE = S5_GROUPS * S5_STATE // S5_PAGES
S5_SW = 2 * S5_NSTATE


def _s5prep_kernel(ar_ref, ai_ref, ld_ref, arc_ref, aic_ref, ldc_ref, bre_ref, bim_ref, cre_ref, cim_ref,
                   bst_ref, bigk_ref, ccr_ref, a8_ref):
    w = LANES

    def discretise(ar, ai, ld):
        dt = jnp.exp(ld)
        ea = jnp.exp(ar * dt)
        return ea * jnp.cos(ai * dt), ea * jnp.sin(ai * dt)

    def powers(a_re, a_im, n):
        pw = [(jnp.ones_like(a_re), jnp.zeros_like(a_re)), (a_re, a_im)]
        for _ in range(2, n + 1):
            pr, pi = pw[-1]
            pw.append((pr * a_re - pi * a_im, pr * a_im + pi * a_re))
        return pw

    ar, ai = ar_ref[...], ai_ref[...]
    a_re, a_im = discretise(ar, ai, ld_ref[...])
    pw = powers(a_re, a_im, S5_T)
    den = ar * ar + ai * ai
    nr = a_re - 1.0
    f_re = (nr * ar + a_im * ai) / den
    f_im = (a_im * ar - nr * ai) / den
    b_re, b_im = bre_ref[...], bim_ref[...]
    bb_re = f_re * b_re - f_im * b_im
    bb_im = f_re * b_im + f_im * b_re
    c_re, c_im = cre_ref[...], cim_ref[...]
    cmat = jnp.concatenate([c_re, -c_im], axis=0)

    bigk_ref[...] = jnp.zeros_like(bigk_ref)
    for lag in range(S5_T):
        pr, pi = pw[lag]
        lmat = jnp.concatenate([pr * bb_re - pi * bb_im, pr * bb_im + pi * bb_re], axis=1)
        s = S5_T - 1 - lag
        bst_ref[s * w:(s + 1) * w, :] = lmat.astype(BF16)
        m = jnp.dot(lmat, cmat, precision=HIGHEST, preferred_element_type=F32).astype(BF16)
        for s in range(S5_T - lag):
            t = s + lag
            bigk_ref[s * w:(s + 1) * w, t * w:(t + 1) * w] = m

    ac_re, ac_im = discretise(arc_ref[...], aic_ref[...], ldc_ref[...])
    pwc = powers(ac_re, ac_im, S5_T)
    for t in range(S5_T):
        pr, pi = pwc[t + 1]
        ccr_ref[:S5_NSTATE, t * w:(t + 1) * w] = (c_re * pr - c_im * pi).astype(BF16)
        ccr_ref[S5_NSTATE:, t * w:(t + 1) * w] = (-(c_re * pi + c_im * pr)).astype(BF16)

    a8_ref[...] = jnp.concatenate(pw[S5_T], axis=1)


def _s5_prep(a_re, a_im, log_dt, b_re, b_im, c_re, c_im):
    pg, g, p, ch = S5_PAGES, S5_GROUPS // S5_PAGES, S5_STATE, S5_CH
    eye = jnp.eye(g, dtype=bool)
    row = lambda t: t.reshape(pg, 1, g * p)
    col = lambda t: t.reshape(pg, g * p, 1)
    ld = jnp.broadcast_to(log_dt[:, None], (pg * g, p))

    def expand_b(b):
        t = jnp.transpose(b.reshape(pg, g, p, ch), (0, 1, 3, 2))[:, :, :, None, :]
        return jnp.where(eye[None, :, None, :, None], t, 0.0).reshape(pg, g * ch, g * p)

    def expand_c(c):
        t = jnp.transpose(c.reshape(pg, g, ch, p), (0, 1, 3, 2))[:, :, :, None, :]
        return jnp.where(eye[None, :, None, :, None], t, 0.0).reshape(pg, g * p, g * ch)

    sd = lambda shape, dt: jax.ShapeDtypeStruct((pg,) + shape, dt)
    spec = lambda shape: pl.BlockSpec((None,) + shape, lambda i: (i, 0, 0))
    return pl.pallas_call(
        _s5prep_kernel,
        grid=(pg,),
        in_specs=[spec((1, g * p))] * 3 + [spec((g * p, 1))] * 3 + [spec((g * ch, g * p))] * 2
                 + [spec((g * p, g * ch))] * 2,
        out_specs=[spec((S5_ROW, S5_SW)), spec((S5_ROW, S5_ROW)), spec((S5_SW, S5_ROW)), spec((1, S5_SW))],
        out_shape=[sd((S5_ROW, S5_SW), BF16), sd((S5_ROW, S5_ROW), BF16),
                   sd((S5_SW, S5_ROW), BF16), sd((1, S5_SW), F32)],
        compiler_params=_cparams(1),
        name="s5_prep",
    )(row(a_re), row(a_im), row(ld), col(a_re), col(a_im), col(ld),
      expand_b(b_re), expand_b(b_im), expand_c(c_re), expand_c(c_im))


def _s5_kernel(u_ref, bst_ref, bigk_ref, ccr_ref, a8_ref, d_ref, o_ref, v_scr, xs_scr, st_ref, *, tr):
    ns = S5_NSTATE
    w = LANES
    pages = range(S5_PAGES)

    @pl.when(pl.program_id(1) == 0)
    def _reset_state():
        st_ref[...] = jnp.zeros_like(st_ref)

    us = [u_ref[pg] for pg in pages]
    ubs = [u.astype(BF16) for u in us]
    for pg in pages:
        v_scr[pg] = _dot(ubs[pg], bst_ref[pg])

    a_re = [a8_ref[pg, :, :ns] for pg in pages]
    a_im = [a8_ref[pg, :, ns:] for pg in pages]

    def body(r, carry):
        new = []
        for pg in pages:
            xr, xi = carry[2 * pg], carry[2 * pg + 1]
            xs_scr[pg, pl.ds(r, 1), :ns] = xr
            xs_scr[pg, pl.ds(r, 1), ns:] = xi
            vr = v_scr[pg, pl.ds(r, 1), :ns]
            vi = v_scr[pg, pl.ds(r, 1), ns:]
            new += [a_re[pg] * xr - a_im[pg] * xi + vr, a_re[pg] * xi + a_im[pg] * xr + vi]
        return tuple(new)

    step = 2
    y0 = {(pg, t): _dot(ubs[pg][:, :(t + step) * w], bigk_ref[pg, :(t + step) * w, t * w:(t + step) * w])
          for pg in pages for t in range(0, S5_T, step)}

    init = []
    for pg in pages:
        init += [st_ref[pg, 0:1, :ns], st_ref[pg, 0:1, ns:]]
    fin = lax.fori_loop(0, tr, body, tuple(init), unroll=True)
    for pg in pages:
        st_ref[pg, 0:1, :ns] = fin[2 * pg]
        st_ref[pg, 0:1, ns:] = fin[2 * pg + 1]

    for pg in pages:
        xb = xs_scr[pg].astype(BF16)
        d_row = jnp.concatenate([d_ref[pg]] * step, axis=1)
        for t in range(0, S5_T, step):
            cols = slice(t * w, (t + step) * w)
            y = y0[pg, t] + _dot(xb, ccr_ref[pg, :, cols]) + d_row * us[pg][:, cols]
            y = 0.5 * y * (1.0 + jnp.tanh(math.sqrt(2.0 / math.pi) * (y + 0.044715 * (y * y * y))))
            for k in range(step):
                o_ref[pg, pl.ds(t + k, tr, stride=S5_T), :] = y[:, k * w:(k + 1) * w]


def _s5(uv, a_re, a_im, log_dt, b_re, b_im, c_re, c_im, d, bsz, tr=512):
    bst, bigk, ccr, a8 = _s5_prep(a_re, a_im, log_dt, b_re, b_im, c_re, c_im)
    pg, rows = uv.shape[:2]
    per_b = rows // bsz // tr
    const = lambda shape: pl.BlockSpec((pg,) + shape, lambda b, i: (0, 0, 0), pipeline_mode=pl.Buffered(1))
    return pl.pallas_call(
        functools.partial(_s5_kernel, tr=tr),
        grid=(bsz, per_b),
        in_specs=[pl.BlockSpec((pg, tr, S5_ROW), lambda b, i: (0, b * per_b + i, 0)),
                  const((S5_ROW, S5_SW)), const((S5_ROW, S5_ROW)), const((S5_SW, S5_ROW)),
                  const((1, S5_SW)), const((1, LANES))],
        out_specs=pl.BlockSpec((pg, tr * S5_T, LANES), lambda b, i: (0, b * per_b + i, 0)),
        out_shape=jax.ShapeDtypeStruct((pg, rows * S5_T, LANES), F32),
        scratch_shapes=[pltpu.VMEM((pg, tr, S5_SW), F32), pltpu.VMEM((pg, tr, S5_SW), F32),
                        pltpu.VMEM((pg, 8, S5_SW), F32)],
        compiler_params=_cparams(2),
        name="s5",
    )(uv, bst, bigk, ccr, a8, d.reshape(pg, 1, LANES))


def _kvproj_kernel(m_ref, w_ref, o_ref):
    mb = m_ref[...].astype(BF16)
    step = 512
    for j in range(0, 2 * D_MODEL, step):
        o_ref[:, j:j + step] = _dot(mb, w_ref[:, j:j + step]).astype(BF16)


def _kvproj(mem2, w_kv, tm=512):
    n = mem2.shape[0]
    return pl.pallas_call(
        _kvproj_kernel,
        grid=(n // tm,),
        in_specs=[pl.BlockSpec((tm, D_MODEL), lambda i: (i, 0)),
                  pl.BlockSpec((D_MODEL, 2 * D_MODEL), lambda i: (0, 0))],
        out_specs=pl.BlockSpec((tm, 2 * D_MODEL), lambda i: (i, 0)),
        out_shape=jax.ShapeDtypeStruct((n, 2 * D_MODEL), BF16),
        compiler_params=_cparams(1),
        name="mem_kv_proj",
    )(mem2, w_kv)


def _mixmem_kernel(x_ref, ya_ref, yb_ref, yc_ref, yd_ref, wglu_ref, bglu_ref, wmix_ref, gmix_ref, bmix_ref,
                   wq_ref, kv_ref, wo_ref, gmem_ref, bmem_ref, out_ref, *, parts):
    hd = MEM_HEAD_DIM
    rows = x_ref.shape[0] // parts
    sls = [slice(part * rows, (part + 1) * rows) for part in range(parts)]
    yds = [_load_pages(yd_ref, 0, sl) for sl in sls]
    gates = [_dot(yd.astype(BF16), wglu_ref[...]) for yd in yds]
    yds = [yd * _sigmoid(g + bglu_ref[...]) for yd, g in zip(yds, gates)]
    ys = [jnp.concatenate([ya_ref[sl, :], yb_ref[sl, :], yc_ref[sl, :], yd.astype(BF16)], axis=1)
          for sl, yd in zip(sls, yds)]
    hs = [_dot(y, wmix_ref[...]) for y in ys]
    xs = [_layer_norm(DEEPNORM_ALPHA * x_ref[sl, :] + h, gmix_ref[...], bmix_ref[...]) for sl, h in zip(sls, hs)]
    qs = [(_dot(x.astype(BF16), wq_ref[...]) * (hd ** -0.5)).astype(BF16) for x in xs]
    outs = [[] for _ in sls]
    for h in range(MEM_HEADS):
        k = kv_ref[:, h * hd:(h + 1) * hd]
        v = kv_ref[:, D_MODEL + h * hd:D_MODEL + (h + 1) * hd]
        ss = [_dot_nt(q[:, h * hd:(h + 1) * hd], k) for q in qs]
        ps = []
        for s in ss:
            p = jnp.exp(s - jnp.max(s, axis=-1, keepdims=True))
            ps.append((p / jnp.sum(p, axis=-1, keepdims=True)).astype(BF16))
        for o, p in zip(outs, ps):
            o.append(_dot(p, v).astype(BF16))
    hs = [_dot(jnp.concatenate(o, axis=1), wo_ref[...]) for o in outs]
    for sl, x, h in zip(sls, xs, hs):
        out_ref[sl, :] = _layer_norm(DEEPNORM_ALPHA * x + h, gmem_ref[...], bmem_ref[...])


def _mixmem(x2, ya, yb, yc, yd, wglu, bglu, wmix, gmix, bmix, wq, kv, wo, gmem, bmem, bsz, seq, tm=1024, parts=4):
    per_b = seq // tm
    w4 = GROUP_WIDTH
    const = lambda shape: pl.BlockSpec(shape, lambda bi, i: (0, 0), pipeline_mode=pl.Buffered(1))
    tok = lambda width: pl.BlockSpec((tm, width), lambda bi, i: (bi * per_b + i, 0))
    return pl.pallas_call(
        functools.partial(_mixmem_kernel, parts=parts),
        grid=(bsz, per_b),
        in_specs=[tok(D_MODEL), tok(w4), tok(w4), tok(w4),
                  pl.BlockSpec((w4 // LANES, tm, LANES), lambda bi, i: (0, bi * per_b + i, 0)),
                  const((w4, w4)), const((1, w4)), const((D_MODEL, D_MODEL)), const((1, D_MODEL)),
                  const((1, D_MODEL)), const((D_MODEL, D_MODEL)),
                  pl.BlockSpec((MEM_LEN, 2 * D_MODEL), lambda bi, i: (bi, 0)),
                  const((D_MODEL, D_MODEL)), const((1, D_MODEL)), const((1, D_MODEL))],
        out_specs=tok(D_MODEL),
        out_shape=jax.ShapeDtypeStruct(x2.shape, F32),
        compiler_params=_cparams(2),
        name="mix_out_mem_attn",
    )(x2, ya, yb, yc, yd, wglu, bglu, wmix, gmix, bmix, wq, kv, wo, gmem, bmem)


FFN_CHUNK = 256


def _ffn_kernel(x_ref, wg_ref, wu_ref, wd_ref, g_ref, b_ref, out_ref, h_ref, *, parts):
    rows = x_ref.shape[0] // parts
    for p in range(parts):
        sl = slice(p * rows, (p + 1) * rows)
        xb = x_ref[sl, :].astype(BF16)
        for j in range(0, D_FF, FFN_CHUNK):
            gate = _dot(xb, wg_ref[:, j:j + FFN_CHUNK])
            up = _dot(xb, wu_ref[:, j:j + FFN_CHUNK])
            h_ref[sl, j:j + FFN_CHUNK] = (gate * _sigmoid(gate) * up).astype(BF16)
        y = _dot(h_ref[sl, :], wd_ref[...])
        out_ref[sl, :] = _layer_norm(DEEPNORM_ALPHA * x_ref[sl, :] + y, g_ref[...], b_ref[...])


def _ffn(x2, wg, wu, wd, g, b, tm=1024, parts=4):
    n = x2.shape[0]
    tok = pl.BlockSpec((tm, D_MODEL), lambda i: (i, 0))
    const = lambda shape: pl.BlockSpec(shape, lambda i: (0, 0), pipeline_mode=pl.Buffered(1))
    return pl.pallas_call(
        functools.partial(_ffn_kernel, parts=parts),
        grid=(n // tm,),
        in_specs=[tok, const((D_MODEL, D_FF)), const((D_MODEL, D_FF)), const((D_FF, D_MODEL)),
                  const((1, D_MODEL)), const((1, D_MODEL))],
        out_specs=tok,
        out_shape=jax.ShapeDtypeStruct(x2.shape, F32),
        scratch_shapes=[pltpu.VMEM((tm, D_FF), BF16)],
        compiler_params=_cparams(1),
        name="ffn",
    )(x2, wg, wu, wd, g, b)


def _permute_w_in(w):
    c = w.shape[0]
    return jnp.concatenate([
        w[:, 0:1536], w[:, 1792:2304],
        w[:, 2576:2832],
        w[:, 1536:1792], w[:, 2320:2576],
        w[:, 2304:2320], jnp.zeros((c, LANES - GLA_GATE_RANK), w.dtype),
    ], axis=1)


def _mixers(x2, bsz, seq, w_in_p, ret_gn_g, gla_w_gate, gla_b_gate, gla_gn_g, s5_a_re, s5_a_im,
            s5_log_dt, s5_b_re, s5_b_im, s5_c_re, s5_c_im, s5_d):
    pa, pb, uv, pf = _inproj(x2, w_in_p)
    ya = _dilated_attention(pa, bsz, seq)
    yb = _retention(pb, pf, ret_gn_g.reshape(1, -1), bsz, seq)
    kwid = HEADS * GLA_DK
    w_gate_p = jnp.concatenate(
        [gla_w_gate, jnp.zeros((kwid - GLA_GATE_RANK, kwid), gla_w_gate.dtype)], axis=0).astype(BF16)
    yc = _gla(pb, pf, w_gate_p, gla_b_gate.reshape(1, -1), gla_gn_g.reshape(1, -1), bsz, seq)
    yd = _s5(uv, s5_a_re, s5_a_im, s5_log_dt, s5_b_re, s5_b_im, s5_c_re, s5_c_im, s5_d, bsz)
    return ya, yb, yc, yd


def kernel(x, mem, w_in, ret_gn_g, gla_w_gate, gla_b_gate, gla_gn_g, s5_a_re, s5_a_im, s5_log_dt, s5_b_re,
           s5_b_im, s5_c_re, s5_c_im, s5_d, s5_w_glu, s5_b_glu, w_mix_out, ln_mix_g, ln_mix_b, w_mem_q,
           w_mem_kv, w_mem_o, ln_mem_g, ln_mem_b, w_ff_gate, w_ff_up, w_ff_down, ln_ff_g, ln_ff_b):
    bsz, seq, dm = x.shape
    assert dm == D_MODEL and mem.shape == (bsz, MEM_LEN, D_MODEL)
    assert seq % DIL_SPAN == 0
    x2 = x.reshape(bsz * seq, dm)
    mem2 = mem.reshape(bsz * MEM_LEN, dm)
    row = lambda t: t.reshape(1, -1)
    for l in range(DEPTH):
        ys = _mixers(x2, bsz, seq, _permute_w_in(w_in[l]).astype(BF16), ret_gn_g[l], gla_w_gate[l], gla_b_gate[l],
                     gla_gn_g[l], s5_a_re[l], s5_a_im[l], s5_log_dt[l], s5_b_re[l], s5_b_im[l],
                     s5_c_re[l], s5_c_im[l], s5_d[l])
        kv = _kvproj(mem2, _to_bf16(w_mem_kv, l))
        x2 = _mixmem(x2, *ys, _to_bf16(s5_w_glu, l), row(s5_b_glu[l]), _to_bf16(w_mix_out, l),
                     row(ln_mix_g[l]), row(ln_mix_b[l]), _to_bf16(w_mem_q, l), kv, _to_bf16(w_mem_o, l),
                     row(ln_mem_g[l]), row(ln_mem_b[l]), bsz, seq)
        x2 = _ffn(x2, _to_bf16(w_ff_gate, l), _to_bf16(w_ff_up, l), _to_bf16(w_ff_down, l),
                  row(ln_ff_g[l]), row(ln_ff_b[l]))
    return x2.reshape(bsz, seq, dm)
```

```python
import functools
import math

import jax
import jax.numpy as jnp
from jax import lax
from jax.experimental import pallas as pl
from jax.experimental.pallas import tpu as pltpu

F32 = jnp.float32
BF16 = jnp.bfloat16
HIGHEST = lax.Precision.HIGHEST
LANES = 128

D_MODEL = 1024
DEPTH = 2
GROUP_WIDTH = 256
HEADS = 4
HEAD_DIM = 64
DIL_BRANCHES = ((128, 1), (512, 4), (2048, 16))
ATTN_BLOCK = 128
RET_CHUNK = 128
GLA_DK = 32
GLA_GATE_RANK = 16
GLA_TAU = 16.0
GLA_CHUNK = 64
S5_CH = 16
S5_GROUPS = 16
S5_STATE = 64
S5_T = 8
MEM_LEN = 256
MEM_HEADS = 4
MEM_HEAD_DIM = 256
D_FF = 2816
DEEPNORM_ALPHA = (2 * DEPTH) ** 0.25
LN_EPS = 1e-5
NEG_INF = -1e30
LOG2_E = math.log2(math.e)
LN_2 = math.log(2.0)

PA_COLS = 768
PB_COLS = 1280
PF_COLS = 640
W_IN_COLS = PA_COLS + PB_COLS + GROUP_WIDTH + PF_COLS

VMEM_LIMIT = 56 * 1024 * 1024


def _cparams(n_axes):
    return pltpu.CompilerParams(dimension_semantics=("arbitrary",) * n_axes,
                                vmem_limit_bytes=VMEM_LIMIT)


def _dot(a, b):
    return jnp.dot(a, b, preferred_element_type=F32)


def _dot_nt(a, b):
    return lax.dot_general(a, b, (((1,), (1,)), ((), ())), preferred_element_type=F32)


def _dot_tn(a, b):
    return lax.dot_general(a, b, (((0,), (0,)), ((), ())), preferred_element_type=F32)


def _store_pages(ref, page, val, rows=slice(None)):
    for k in range(val.shape[1] // LANES):
        ref[page + k, rows, :] = val[:, k * LANES:(k + 1) * LANES]


def _load_pages(ref, page, rows, npages=2):
    return jnp.concatenate([ref[page + k, rows, :] for k in range(npages)], axis=1)


def _layer_norm(z, g, b):
    mu = jnp.mean(z, axis=-1, keepdims=True)
    d = z - mu
    var = jnp.mean(d * d, axis=-1, keepdims=True)
    return d * lax.rsqrt(var + LN_EPS) * g + b


def _sigmoid(x):
    return 1.0 / (1.0 + jnp.exp(-x))


def _head_of_lane(shape, head_width):
    return lax.broadcasted_iota(jnp.int32, shape, len(shape) - 1) // head_width


def _stack_heads(t, head_width):
    lane_head = _head_of_lane(t.shape, head_width)
    zero = jnp.zeros_like(t)
    return jnp.concatenate([jnp.where(lane_head == h, t, zero) for h in range(HEADS)], axis=0)


def _stack_heads_paged(t):
    n = t.shape[0]
    per_page = LANES // HEAD_DIM
    first = _head_of_lane((n, LANES), HEAD_DIM)
    zero = jnp.zeros((n, LANES), t.dtype)
    copies = []
    for h in range(HEADS):
        page = h // per_page
        own = jnp.where(first == h % per_page, t[:, page * LANES:(page + 1) * LANES], zero)
        copies.append(jnp.concatenate([own if pg == page else zero for pg in range(GROUP_WIDTH // LANES)], axis=1))
    return jnp.concatenate(copies, axis=0)


def _select_heads_paged(t4, col4=None):
    n = t4.shape[0] // HEADS
    per_page = LANES // HEAD_DIM
    first = _head_of_lane((n, LANES), HEAD_DIM)
    pages = []
    for page in range(GROUP_WIDTH // LANES):
        out = None
        for k in range(per_page - 1, -1, -1):
            h = page * per_page + k
            blk = t4[h * n:(h + 1) * n, page * LANES:(page + 1) * LANES]
            if col4 is not None:
                blk = blk * col4[h * n:(h + 1) * n]
            out = blk if out is None else jnp.where(first == k, blk, out)
        pages.append(out)
    return pages


def _split_dot(x, w_bf16, terms):
    acc = None
    for _ in range(terms):
        piece = x.astype(BF16)
        part = _dot(piece, w_bf16)
        acc = part if acc is None else acc + part
        x = x - piece.astype(F32)
    return acc


def _head_norm(o, avg_bf16):
    d = o - _split_dot(o, avg_bf16, 2)
    var = _split_dot(d * d, avg_bf16, 1)
    return d * lax.rsqrt(var + LN_EPS)


def _head_avg_matrix(width, head_width):
    r = lax.broadcasted_iota(jnp.int32, (width, width), 0) // head_width
    c = lax.broadcasted_iota(jnp.int32, (width, width), 1) // head_width
    return jnp.where(r == c, 1.0 / head_width, 0.0).astype(BF16)


CAST_BLOCK_BYTES = 4 * 1024 * 1024


def _cast_kernel(w_ref, o_ref):
    o_ref[...] = w_ref[...].astype(BF16)


def _to_bf16(w, layer=None):
    rows, cols = w.shape[-2:]
    tr = rows
    while tr * cols * 4 > CAST_BLOCK_BYTES and tr % 16 == 0:
        tr //= 2
    if layer is None:
        in_spec = pl.BlockSpec((tr, cols), lambda i: (i, 0))
    else:
        in_spec = pl.BlockSpec((None, tr, cols), lambda i: (layer, i, 0))
    return pl.pallas_call(
        _cast_kernel,
        grid=(rows // tr,),
        in_specs=[in_spec],
        out_specs=pl.BlockSpec((tr, cols), lambda i: (i, 0)),
        out_shape=jax.ShapeDtypeStruct((rows, cols), BF16),
        compiler_params=_cparams(1),
        name="weight_cast",
    )(w)


def _inproj_kernel(x_ref, w_ref, pa_ref, pb_ref, suv_ref, pf_ref, su_scr):
    xb = x_ref[...].astype(BF16)
    step = 256
    for j in range(0, PA_COLS, step):
        _store_pages(pa_ref, j // LANES, _dot(xb, w_ref[:, j:j + step]))
    for j in range(0, PB_COLS, step):
        pb_ref[:, j:j + step] = _dot(xb, w_ref[:, PA_COLS + j:PA_COLS + j + step]).astype(BF16)
    off = PA_COLS + PB_COLS
    _store_pages(su_scr, 0, _dot(xb, w_ref[:, off:off + GROUP_WIDTH]))
    nrow = su_scr.shape[1] // S5_T
    for page in range(GROUP_WIDTH // LANES):
        for s in range(S5_T):
            suv_ref[page, :, s * LANES:(s + 1) * LANES] = su_scr[page, pl.ds(s, nrow, stride=S5_T), :]
    pf_ref[...] = _dot(xb, w_ref[:, off + GROUP_WIDTH:])


def _inproj(x2, w_p, tm=512):
    n = x2.shape[0]
    return pl.pallas_call(
        _inproj_kernel,
        grid=(n // tm,),
        in_specs=[pl.BlockSpec((tm, D_MODEL), lambda i: (i, 0)),
                  pl.BlockSpec((D_MODEL, W_IN_COLS), lambda i: (0, 0))],
        out_specs=[pl.BlockSpec((PA_COLS // LANES, tm, LANES), lambda i: (0, i, 0)),
                   pl.BlockSpec((tm, PB_COLS), lambda i: (i, 0)),
                   pl.BlockSpec((GROUP_WIDTH // LANES, tm // S5_T, S5_T * LANES), lambda i: (0, i, 0)),
                   pl.BlockSpec((tm, PF_COLS), lambda i: (i, 0))],
        out_shape=[jax.ShapeDtypeStruct((PA_COLS // LANES, n, LANES), F32),
                   jax.ShapeDtypeStruct((n, PB_COLS), BF16),
                   jax.ShapeDtypeStruct((GROUP_WIDTH // LANES, n // S5_T, S5_T * LANES), F32),
                   jax.ShapeDtypeStruct((n, PF_COLS), F32)],
        scratch_shapes=[pltpu.VMEM((GROUP_WIDTH // LANES, tm, LANES), F32)],
        compiler_params=_cparams(1),
        name="inproj",
    )(x2, w_p)


DIL_SPAN = ATTN_BLOCK * max(r for _, r in DIL_BRANCHES)
DIL_GROUP = 2
DIL_GROUP_INTERIOR = {1: 5, 4: 3}


def _dil_tables(dilation):
    blk = ATTN_BLOCK
    shape = (HEADS * blk, 2 * blk)
    row = lax.broadcasted_iota(jnp.int32, shape, 0)
    ki = lax.broadcasted_iota(jnp.int32, shape, 1)
    head = row // blk
    dist = row % blk + blk - ki
    valid = (dist >= 0) & (dist <= blk)
    slope = jnp.where(head == 0, 2.0 ** -2, jnp.where(head == 1, 2.0 ** -4,
                      jnp.where(head == 2, 2.0 ** -6, 2.0 ** -8))).astype(F32)
    alibi = -slope * (dist * dilation).astype(F32) * LOG2_E
    return jnp.where(valid, alibi, NEG_INF), jnp.where(valid & (ki >= blk), alibi, NEG_INF)


def _dil_attend(qs, kwins, vwins, tbs):
    scale = HEAD_DIM ** -0.5 * LOG2_E
    ss = [_dot_nt(_stack_heads_paged((q * scale).astype(BF16)), kwin.astype(BF16))
          for q, kwin in zip(qs, kwins)]
    ms, ps, ls = [], [], []
    for s, tb in zip(ss, tbs):
        s = s + tb
        m = jnp.max(s, axis=-1, keepdims=True)
        p = jnp.exp2(s - m)
        ms.append(m)
        ls.append(jnp.sum(p, axis=-1, keepdims=True))
        ps.append(p.astype(BF16))
    o4s = [_dot(p, vwin.astype(BF16)) for p, vwin in zip(ps, vwins)]
    outs = []
    for o4, m, l in zip(o4s, ms, ls):
        lse4 = jnp.broadcast_to((m + jnp.log2(l)) * LN_2, (o4.shape[0], LANES))
        outs.append((_select_heads_paged(o4, 1.0 / l),
                     _select_heads_paged(jnp.concatenate([lse4] * (GROUP_WIDTH // LANES), axis=1))))
    return outs


def _dil_kernel(q_ref, kp_ref, kc_ref, vp_ref, vc_ref, o_ref, tb_ref, ob_ref, lb_ref):
    blk = ATTN_BLOCK
    span = DIL_SPAN
    nbr = len(DIL_BRANCHES)

    @pl.when((pl.program_id(0) == 0) & (pl.program_id(1) == 0))
    def _init_tables():
        for bi, (_, r) in enumerate(DIL_BRANCHES):
            t_any, t_first = _dil_tables(r)
            tb_ref[bi, 0] = t_any
            tb_ref[bi, 1] = t_first

    first = jnp.where(pl.program_id(1) == 0, 1, 0)

    def rows(start, size, r):
        if r == 1:
            return pl.ds(pl.multiple_of(start, blk), size)
        return pl.ds(start, size, stride=r)

    def store(bi, rqs, outs):
        for rq, (o_pages, lse_pages) in zip(rqs, outs):
            for pg, (o, lse) in enumerate(zip(o_pages, lse_pages)):
                ob_ref[2 * bi + pg, rq, :] = o
                lb_ref[2 * bi + pg, rq, :] = lse

    def combine(bi, rqs, outs):
        for rq, (o_pages, lse_pages) in zip(rqs, outs):
            ya = []
            for pg in range(GROUP_WIDTH // LANES):
                lses = [lse_pages[pg]] + [lb_ref[2 * b + pg, rq, :] for b in range(nbr) if b != bi]
                os_ = [o_pages[pg]] + [ob_ref[2 * b + pg, rq, :] for b in range(nbr) if b != bi]
                mx = functools.reduce(jnp.maximum, lses)
                es = [jnp.exp(l - mx) for l in lses]
                num = functools.reduce(lambda x, y: x + y, [e * o for e, o in zip(es, os_)])
                ya.append(num / functools.reduce(lambda x, y: x + y, es))
            o_ref[rq, :] = jnp.concatenate(ya, axis=1).astype(BF16)

    order = sorted(range(nbr), key=lambda b: -DIL_BRANCHES[b][1])
    for bi in order:
        r = DIL_BRANCHES[bi][1]
        finish = combine if bi == order[-1] else store
        per_sub = span // (blk * r)
        group = min(r, DIL_GROUP)

        def boundary(g, carry, bi=bi, r=r, group=group, finish=finish):
            rqs = [rows(g * group + u, blk, r) for u in range(group)]
            rps = [rows(g * group + u + span - blk * r, blk, r) for u in range(group)]
            kwins = [jnp.concatenate([_load_pages(kp_ref, 0, rp), _load_pages(kc_ref, 0, rq)], axis=0)
                     for rp, rq in zip(rps, rqs)]
            vwins = [jnp.concatenate([_load_pages(vp_ref, 0, rp), _load_pages(vc_ref, 0, rq)], axis=0)
                     for rp, rq in zip(rps, rqs)]
            tb = tb_ref[bi, first]
            finish(bi, rqs, _dil_attend([_load_pages(q_ref, 0, rq) for rq in rqs], kwins, vwins, [tb] * group))
            return carry

        lax.fori_loop(0, r // group, boundary, 0)

        if per_sub > 1:
            group = DIL_GROUP_INTERIOR[r]

            def interior(g, carry, bi=bi, r=r, group=group, finish=finish):
                rqs, rks = [], []
                for u in range(group):
                    idx = g * group + u
                    c = lax.rem(idx, r)
                    j = 1 + lax.div(idx, r)
                    rqs.append(rows(c + blk * r * j, blk, r))
                    rks.append(rows(c + blk * r * (j - 1), 2 * blk, r))
                tb = tb_ref[bi, 0]
                finish(bi, rqs, _dil_attend([_load_pages(q_ref, 0, rq) for rq in rqs],
                                            [_load_pages(kc_ref, 0, rk) for rk in rks],
                                            [_load_pages(vc_ref, 0, rk) for rk in rks], [tb] * group))
                return carry

            lax.fori_loop(0, r * (per_sub - 1) // group, interior, 0)


def _dilated_attention(pa, bsz, seq):
    n = bsz * seq
    span = DIL_SPAN
    per_b = seq // span
    w = GROUP_WIDTH
    pages = w // LANES

    def cur(col):
        return pl.BlockSpec((pages, span, LANES), lambda b, i: (col, b * per_b + i, 0))

    def prev(col):
        return pl.BlockSpec((pages, span, LANES), lambda b, i: (col, b * per_b + jnp.maximum(i - 1, 0), 0))

    nbr = len(DIL_BRANCHES)
    return pl.pallas_call(
        _dil_kernel,
        grid=(bsz, per_b),
        in_specs=[cur(0), prev(1), cur(1), prev(2), cur(2)],
        out_specs=pl.BlockSpec((span, w), lambda b, i: (b * per_b + i, 0)),
        out_shape=jax.ShapeDtypeStruct((n, w), BF16),
        scratch_shapes=[pltpu.VMEM((nbr, 2, HEADS * ATTN_BLOCK, 2 * ATTN_BLOCK), F32),
                        pltpu.VMEM((nbr * pages, span, LANES), F32),
                        pltpu.VMEM((nbr * pages, span, LANES), F32)],
        compiler_params=_cparams(2),
        name="dilated_attention",
    )(pa, pa, pa, pa, pa)


def _ret_kernel(q_ref, k_ref, v_ref, g_ref, gn_ref, o_ref, r_ref, dtab_ref, qd_ref, kd_ref, o_scr, *, nchunk):
    c = RET_CHUNK
    width = GROUP_WIDTH
    scale = HEAD_DIM ** -0.5
    log_g = [math.log(1.0 - 2.0 ** (-5.0 - h)) for h in range(HEADS)]

    def head_log_g(head):
        return jnp.where(head == 0, log_g[0], jnp.where(head == 1, log_g[1],
                         jnp.where(head == 2, log_g[2], log_g[3]))).astype(F32)

    @pl.when((pl.program_id(0) == 0) & (pl.program_id(1) == 0))
    def _init_tables():
        qry = lax.broadcasted_iota(jnp.int32, (c, HEADS * c), 0)
        col = lax.broadcasted_iota(jnp.int32, (c, HEADS * c), 1)
        rel = (qry - col % c).astype(F32)
        decay = jnp.exp(jnp.maximum(rel, 0.0) * head_log_g(col // c))
        dtab_ref[...] = jnp.where(rel >= 0, decay, 0.0) * scale
        pos = (lax.broadcasted_iota(jnp.int32, (nchunk * c, width), 0) % c).astype(F32)
        lgl = head_log_g(_head_of_lane((nchunk * c, width), HEAD_DIM))
        qd_ref[...] = jnp.exp((pos + 1.0) * lgl)
        kd_ref[...] = jnp.exp((c - 1.0 - pos) * lgl) * scale

    @pl.when(pl.program_id(1) == 0)
    def _reset_state():
        r_ref[...] = jnp.zeros_like(r_ref)

    rr = lax.broadcasted_iota(jnp.int32, (width, width), 0) // HEAD_DIM
    cc = lax.broadcasted_iota(jnp.int32, (width, width), 1) // HEAD_DIM
    same_head = rr == cc
    chunk_decay = jnp.exp(c * head_log_g(_head_of_lane((1, width), HEAD_DIM)))
    q_dec = (q_ref[...].astype(F32) * qd_ref[...]).astype(BF16)
    k_dec = (k_ref[...].astype(F32) * kd_ref[...]).astype(BF16)

    sls = [slice(ci * c, (ci + 1) * c) for ci in range(nchunk)]
    scores = [_dot_nt(q_ref[sl, :], _stack_heads(k_ref[sl, :], HEAD_DIM)) for sl in sls]
    probs = [(sc * dtab_ref[...]).astype(BF16) for sc in scores]
    inner = [_dot(pr, _stack_heads(v_ref[sl, :], HEAD_DIM)) for pr, sl in zip(probs, sls)]
    upd = [_dot_tn(k_dec[sl, :], v_ref[sl, :]) for sl in sls]
    states = [r_ref[...]]
    for ci in range(nchunk):
        states.append(states[-1] * chunk_decay + jnp.where(same_head, upd[ci], 0.0))
    r_ref[...] = states[nchunk]
    cross = [_dot(q_dec[sl, :], states[ci].astype(BF16)) for ci, sl in enumerate(sls)]
    for ci, sl in enumerate(sls):
        o_scr[sl, :] = inner[ci] + cross[ci]

    o = _head_norm(o_scr[...], _head_avg_matrix(width, HEAD_DIM)) * gn_ref[...]
    g = g_ref[...]
    o_ref[...] = (o * (g * _sigmoid(g))).astype(BF16)


def _retention(pb, pf, gn, bsz, seq, tq=2048):
    n = bsz * seq
    per_b = seq // tq
    w = GROUP_WIDTH
    nchunk = tq // RET_CHUNK

    def tok(col):
        return pl.BlockSpec((tq, w), lambda b, i: (b * per_b + i, col))

    return pl.pallas_call(
        functools.partial(_ret_kernel, nchunk=nchunk),
        grid=(bsz, per_b),
        in_specs=[tok(0), tok(1), tok(2), tok(0), pl.BlockSpec((1, w), lambda b, i: (0, 0))],
        out_specs=tok(0),
        out_shape=jax.ShapeDtypeStruct((n, w), BF16),
        scratch_shapes=[pltpu.VMEM((w, w), F32),
                        pltpu.VMEM((RET_CHUNK, HEADS * RET_CHUNK), F32),
                        pltpu.VMEM((tq, w), F32),
                        pltpu.VMEM((tq, w), F32),
                        pltpu.VMEM((tq, w), F32)],
        compiler_params=_cparams(2),
        name="retention",
    )(pb, pb, pb, pf, gn)


def _gla_kernel(q_ref, k_ref, v_ref, gl_ref, og_ref, wg_ref, bg_ref, gn_ref, o_ref, rt_ref, cum_ref, o_scr,
                *, nchunk):
    c = GLA_CHUNK
    tq = nchunk * c
    kwid = HEADS * GLA_DK
    vwid = GROUP_WIDTH
    scale = GLA_DK ** -0.5

    @pl.when((pl.program_id(0) == 0) & (pl.program_id(1) == 0))
    def _init_tables():
        i = lax.broadcasted_iota(jnp.int32, (2 * c, 2 * c), 0)
        j = lax.broadcasted_iota(jnp.int32, (2 * c, 2 * c), 1)
        cum_ref[...] = jnp.where((i // c == j // c) & (i >= j), 1.0, 0.0).astype(BF16)

    @pl.when(pl.program_id(1) == 0)
    def _reset_state():
        rt_ref[...] = jnp.zeros_like(rt_ref)

    causal = lax.broadcasted_iota(jnp.int32, (c, HEADS * c), 0) >= (
        lax.broadcasted_iota(jnp.int32, (c, HEADS * c), 1) % c)
    rr = lax.broadcasted_iota(jnp.int32, (vwid, kwid), 0) // HEAD_DIM
    cc = lax.broadcasted_iota(jnp.int32, (vwid, kwid), 1) // GLA_DK
    same_head = rr == cc

    z = _dot(gl_ref[...].astype(BF16), wg_ref[...]) + bg_ref[...]
    log_a = (jnp.minimum(z, 0.0) - jnp.log(1.0 + jnp.exp(-jnp.abs(z)))) * (1.0 / GLA_TAU)
    hi = log_a.astype(BF16)
    rest = log_a - hi.astype(F32)
    mid = rest.astype(BF16)
    lo = (rest - mid.astype(F32)).astype(BF16)
    la3 = jnp.concatenate([hi, mid, lo], axis=1)
    sums = jnp.concatenate([_dot(cum_ref[...], la3[p:p + 2 * c, :]) for p in range(0, tq, 2 * c)], axis=0)
    bcum = sums[:, :kwid] + sums[:, kwid:2 * kwid] + sums[:, 2 * kwid:]
    blast = jnp.concatenate([jnp.broadcast_to(bcum[e - 1:e, :], (c, kwid)) for e in range(c, tq + c, c)],
                            axis=0)
    q = q_ref[...].astype(F32)
    k = k_ref[...].astype(F32)
    q_in = (q * scale * jnp.exp(bcum)).astype(BF16)
    k_out = (k * jnp.exp(-bcum)).astype(BF16)
    k_st = (k * jnp.exp(blast - bcum)).astype(BF16)
    chunk_decay = jnp.exp(blast)

    sls = [slice(ci * c, (ci + 1) * c) for ci in range(nchunk)]
    att = [_dot_nt(q_in[sl, :], _stack_heads(k_out[sl, :], GLA_DK)) for sl in sls]
    att = [jnp.where(causal, a, 0.0).astype(BF16) for a in att]
    intra = [_dot(a, _stack_heads(v_ref[sl, :], HEAD_DIM)) for a, sl in zip(att, sls)]
    upd = [_dot_tn(v_ref[sl, :], k_st[sl, :]) for sl in sls]
    states = [rt_ref[...]]
    for ci in range(nchunk):
        states.append(states[-1] * chunk_decay[ci * c:ci * c + 1, :] + jnp.where(same_head, upd[ci], 0.0))
    rt_ref[...] = states[nchunk]
    cross = [_dot_nt(q_in[sl, :], states[ci].astype(BF16)) for ci, sl in enumerate(sls)]
    for ci, sl in enumerate(sls):
        o_scr[sl, :] = intra[ci] + cross[ci]

    o = _head_norm(o_scr[...], _head_avg_matrix(vwid, HEAD_DIM)) * gn_ref[...]
    g = og_ref[...]
    o_ref[...] = (o * (g * _sigmoid(g))).astype(BF16)


def _gla(pb, pf, w_gate_p, b_gate, gn, bsz, seq, tq=2048):
    n = bsz * seq
    per_b = seq // tq
    kwid = HEADS * GLA_DK
    nchunk = tq // GLA_CHUNK

    def tok(width, col):
        return pl.BlockSpec((tq, width), lambda b, i: (b * per_b + i, col))

    def const(shape):
        return pl.BlockSpec(shape, lambda b, i: (0, 0))

    return pl.pallas_call(
        functools.partial(_gla_kernel, nchunk=nchunk),
        grid=(bsz, per_b),
        in_specs=[tok(kwid, 6), tok(kwid, 7), tok(GROUP_WIDTH, 4),
                  tok(kwid, 4), tok(GROUP_WIDTH, 1),
                  const((kwid, kwid)), const((1, kwid)), const((1, GROUP_WIDTH))],
        out_specs=tok(GROUP_WIDTH, 0),
        out_shape=jax.ShapeDtypeStruct((n, GROUP_WIDTH), BF16),
        scratch_shapes=[pltpu.VMEM((GROUP_WIDTH, kwid), F32),
                        pltpu.VMEM((2 * GLA_CHUNK, 2 * GLA_CHUNK), BF16),
                        pltpu.VMEM((tq, GROUP_WIDTH), F32)],
        compiler_params=_cparams(2),
        name="gla",
    )(pb, pb, pb, pf, pf, w_gate_p, b_gate, gn)


S5_PAGES = GROUP_WIDTH // LANES
S5_ROW = S5_T * LANES
S5_NSTATE = S5_GROUPS * S5_STATE // S5_PAGES
S5_SW = 2 * S5_NSTATE


def _s5prep_kernel(ar_ref, ai_ref, ld_ref, arc_ref, aic_ref, ldc_ref, bre_ref, bim_ref, cre_ref, cim_ref,
                   bst_ref, bigk_ref, ccr_ref, a8_ref):
    w = LANES

    def discretise(ar, ai, ld):
        dt = jnp.exp(ld)
        ea = jnp.exp(ar * dt)
        return ea * jnp.cos(ai * dt), ea * jnp.sin(ai * dt)

    def powers(a_re, a_im, n):
        pw = [(jnp.ones_like(a_re), jnp.zeros_like(a_re)), (a_re, a_im)]
        for _ in range(2, n + 1):
            pr, pi = pw[-1]
            pw.append((pr * a_re - pi * a_im, pr * a_im + pi * a_re))
        return pw

    ar, ai = ar_ref[...], ai_ref[...]
    a_re, a_im = discretise(ar, ai, ld_ref[...])
    pw = powers(a_re, a_im, S5_T)
    den = ar * ar + ai * ai
    nr = a_re - 1.0
    f_re = (nr * ar + a_im * ai) / den
    f_im = (a_im * ar - nr * ai) / den
    b_re, b_im = bre_ref[...], bim_ref[...]
    bb_re = f_re * b_re - f_im * b_im
    bb_im = f_re * b_im + f_im * b_re
    c_re, c_im = cre_ref[...], cim_ref[...]
    cmat = jnp.concatenate([c_re, -c_im], axis=0)

    bigk_ref[...] = jnp.zeros_like(bigk_ref)
    for lag in range(S5_T):
        pr, pi = pw[lag]
        lmat = jnp.concatenate([pr * bb_re - pi * bb_im, pr * bb_im + pi * bb_re], axis=1)
        s = S5_T - 1 - lag
        bst_ref[s * w:(s + 1) * w, :] = lmat.astype(BF16)
        m = jnp.dot(lmat, cmat, precision=HIGHEST, preferred_element_type=F32).astype(BF16)
        for s in range(S5_T - lag):
            t = s + lag
            bigk_ref[s * w:(s + 1) * w, t * w:(t + 1) * w] = m

    ac_re, ac_im = discretise(arc_ref[...], aic_ref[...], ldc_ref[...])
    pwc = powers(ac_re, ac_im, S5_T)
    for t in range(S5_T):
        pr, pi = pwc[t + 1]
        ccr_ref[:S5_NSTATE, t * w:(t + 1) * w] = (c_re * pr - c_im * pi).astype(BF16)
        ccr_ref[S5_NSTATE:, t * w:(t + 1) * w] = (-(c_re * pi + c_im * pr)).astype(BF16)

    a8_ref[...] = jnp.concatenate(pw[S5_T], axis=1)


def _s5_prep(a_re, a_im, log_dt, b_re, b_im, c_re, c_im):
    pg, g, p, ch = S5_PAGES, S5_GROUPS // S5_PAGES, S5_STATE, S5_CH
    eye = jnp.eye(g, dtype=bool)
    row = lambda t: t.reshape(pg, 1, g * p)
    col = lambda t: t.reshape(pg, g * p, 1)
    ld = jnp.broadcast_to(log_dt[:, None], (pg * g, p))

    def expand_b(b):
        t = jnp.transpose(b.reshape(pg, g, p, ch), (0, 1, 3, 2))[:, :, :, None, :]
        return jnp.where(eye[None, :, None, :, None], t, 0.0).reshape(pg, g * ch, g * p)

    def expand_c(c):
        t = jnp.transpose(c.reshape(pg, g, ch, p), (0, 1, 3, 2))[:, :, :, None, :]
        return jnp.where(eye[None, :, None, :, None], t, 0.0).reshape(pg, g * p, g * ch)

    sd = lambda shape, dt: jax.ShapeDtypeStruct((pg,) + shape, dt)
    spec = lambda shape: pl.BlockSpec((None,) + shape, lambda i: (i, 0, 0))
    return pl.pallas_call(
        _s5prep_kernel,
        grid=(pg,),
        in_specs=[spec((1, g * p))] * 3 + [spec((g * p, 1))] * 3 + [spec((g * ch, g * p))] * 2
                 + [spec((g * p, g * ch))] * 2,
        out_specs=[spec((S5_ROW, S5_SW)), spec((S5_ROW, S5_ROW)), spec((S5_SW, S5_ROW)), spec((1, S5_SW))],
        out_shape=[sd((S5_ROW, S5_SW), BF16), sd((S5_ROW, S5_ROW), BF16),
                   sd((S5_SW, S5_ROW), BF16), sd((1, S5_SW), F32)],
        compiler_params=_cparams(1),
        name="s5_prep",
    )(row(a_re), row(a_im), row(ld), col(a_re), col(a_im), col(ld),
      expand_b(b_re), expand_b(b_im), expand_c(c_re), expand_c(c_im))


def _s5_kernel(u_ref, bst_ref, bigk_ref, ccr_ref, a8_ref, d_ref, o_ref, v_scr, xs_scr, st_ref, *, tr):
    ns = S5_NSTATE
    w = LANES
    pages = range(S5_PAGES)

    @pl.when(pl.program_id(1) == 0)
    def _reset_state():
        st_ref[...] = jnp.zeros_like(st_ref)

    us = [u_ref[pg] for pg in pages]
    ubs = [u.astype(BF16) for u in us]
    for pg in pages:
        v_scr[pg] = _dot(ubs[pg], bst_ref[pg])

    a_re = [a8_ref[pg, :, :ns] for pg in pages]
    a_im = [a8_ref[pg, :, ns:] for pg in pages]

    def body(r, carry):
        new = []
        for pg in pages:
            xr, xi = carry[2 * pg], carry[2 * pg + 1]
            xs_scr[pg, pl.ds(r, 1), :ns] = xr
            xs_scr[pg, pl.ds(r, 1), ns:] = xi
            vr = v_scr[pg, pl.ds(r, 1), :ns]
            vi = v_scr[pg, pl.ds(r, 1), ns:]
            new += [a_re[pg] * xr - a_im[pg] * xi + vr, a_re[pg] * xi + a_im[pg] * xr + vi]
        return tuple(new)

    step = 2
    y0 = {(pg, t): _dot(ubs[pg][:, :(t + step) * w], bigk_ref[pg, :(t + step) * w, t * w:(t + step) * w])
          for pg in pages for t in range(0, S5_T, step)}

    init = []
    for pg in pages:
        init += [st_ref[pg, 0:1, :ns], st_ref[pg, 0:1, ns:]]
    fin = lax.fori_loop(0, tr, body, tuple(init), unroll=True)
    for pg in pages:
        st_ref[pg, 0:1, :ns] = fin[2 * pg]
        st_ref[pg, 0:1, ns:] = fin[2 * pg + 1]

    for pg in pages:
        xb = xs_scr[pg].astype(BF16)
        d_row = jnp.concatenate([d_ref[pg]] * step, axis=1)
        for t in range(0, S5_T, step):
            cols = slice(t * w, (t + step) * w)
            y = y0[pg, t] + _dot(xb, ccr_ref[pg, :, cols]) + d_row * us[pg][:, cols]
            y = 0.5 * y * (1.0 + jnp.tanh(math.sqrt(2.0 / math.pi) * (y + 0.044715 * (y * y * y))))
            for k in range(step):
                o_ref[pg, pl.ds(t + k, tr, stride=S5_T), :] = y[:, k * w:(k + 1) * w]


def _s5(uv, a_re, a_im, log_dt, b_re, b_im, c_re, c_im, d, bsz, tr=512):
    bst, bigk, ccr, a8 = _s5_prep(a_re, a_im, log_dt, b_re, b_im, c_re, c_im)
    pg, rows = uv.shape[:2]
    per_b = rows // bsz // tr
    const = lambda shape: pl.BlockSpec((pg,) + shape, lambda b, i: (0, 0, 0), pipeline_mode=pl.Buffered(1))
    return pl.pallas_call(
        functools.partial(_s5_kernel, tr=tr),
        grid=(bsz, per_b),
        in_specs=[pl.BlockSpec((pg, tr, S5_ROW), lambda b, i: (0, b * per_b + i, 0)),
                  const((S5_ROW, S5_SW)), const((S5_ROW, S5_ROW)), const((S5_SW, S5_ROW)),
                  const((1, S5_SW)), const((1, LANES))],
        out_specs=pl.BlockSpec((pg, tr * S5_T, LANES), lambda b, i: (0, b * per_b + i, 0)),
        out_shape=jax.ShapeDtypeStruct((pg, rows * S5_T, LANES), F32),
        scratch_shapes=[pltpu.VMEM((pg, tr, S5_SW), F32), pltpu.VMEM((pg, tr, S5_SW), F32),
                        pltpu.VMEM((pg, 8, S5_SW), F32)],
        compiler_params=_cparams(2),
        name="s5",
    )(uv, bst, bigk, ccr, a8, d.reshape(pg, 1, LANES))


def _kvproj_kernel(m_ref, w_ref, o_ref):
    mb = m_ref[...].astype(BF16)
    step = 512
    for j in range(0, 2 * D_MODEL, step):
        o_ref[:, j:j + step] = _dot(mb, w_ref[:, j:j + step]).astype(BF16)


def _kvproj(mem2, w_kv, tm=512):
    n = mem2.shape[0]
    return pl.pallas_call(
        _kvproj_kernel,
        grid=(n // tm,),
        in_specs=[pl.BlockSpec((tm, D_MODEL), lambda i: (i, 0)),
                  pl.BlockSpec((D_MODEL, 2 * D_MODEL), lambda i: (0, 0))],
        out_specs=pl.BlockSpec((tm, 2 * D_MODEL), lambda i: (i, 0)),
        out_shape=jax.ShapeDtypeStruct((n, 2 * D_MODEL), BF16),
        compiler_params=_cparams(1),
        name="mem_kv_proj",
    )(mem2, w_kv)


def _mixmem_kernel(x_ref, ya_ref, yb_ref, yc_ref, yd_ref, wglu_ref, bglu_ref, wmix_ref, gmix_ref, bmix_ref,
                   wq_ref, kv_ref, wo_ref, gmem_ref, bmem_ref, out_ref, *, parts):
    hd = MEM_HEAD_DIM
    rows = x_ref.shape[0] // parts
    sls = [slice(part * rows, (part + 1) * rows) for part in range(parts)]
    yds = [_load_pages(yd_ref, 0, sl) for sl in sls]
    gates = [_dot(yd.astype(BF16), wglu_ref[...]) for yd in yds]
    yds = [yd * _sigmoid(g + bglu_ref[...]) for yd, g in zip(yds, gates)]
    ys = [jnp.concatenate([ya_ref[sl, :], yb_ref[sl, :], yc_ref[sl, :], yd.astype(BF16)], axis=1)
          for sl, yd in zip(sls, yds)]
    hs = [_dot(y, wmix_ref[...]) for y in ys]
    xs = [_layer_norm(DEEPNORM_ALPHA * x_ref[sl, :] + h, gmix_ref[...], bmix_ref[...]) for sl, h in zip(sls, hs)]
    qs = [(_dot(x.astype(BF16), wq_ref[...]) * (hd ** -0.5)).astype(BF16) for x in xs]
    outs = [[] for _ in sls]
    for h in range(MEM_HEADS):
        k = kv_ref[:, h * hd:(h + 1) * hd]
        v = kv_ref[:, D_MODEL + h * hd:D_MODEL + (h + 1) * hd]
        ss = [_dot_nt(q[:, h * hd:(h + 1) * hd], k) for q in qs]
        ps = []
        for s in ss:
            p = jnp.exp(s - jnp.max(s, axis=-1, keepdims=True))
            ps.append((p / jnp.sum(p, axis=-1, keepdims=True)).astype(BF16))
        for o, p in zip(outs, ps):
            o.append(_dot(p, v).astype(BF16))
    hs = [_dot(jnp.concatenate(o, axis=1), wo_ref[...]) for o in outs]
    for sl, x, h in zip(sls, xs, hs):
        out_ref[sl, :] = _layer_norm(DEEPNORM_ALPHA * x + h, gmem_ref[...], bmem_ref[...])


def _mixmem(x2, ya, yb, yc, yd, wglu, bglu, wmix, gmix, bmix, wq, kv, wo, gmem, bmem, bsz, seq, tm=1024, parts=4):
    per_b = seq // tm
    w4 = GROUP_WIDTH
    const = lambda shape: pl.BlockSpec(shape, lambda bi, i: (0, 0), pipeline_mode=pl.Buffered(1))
    tok = lambda width: pl.BlockSpec((tm, width), lambda bi, i: (bi * per_b + i, 0))
    return pl.pallas_call(
        functools.partial(_mixmem_kernel, parts=parts),
        grid=(bsz, per_b),
        in_specs=[tok(D_MODEL), tok(w4), tok(w4), tok(w4),
                  pl.BlockSpec((w4 // LANES, tm, LANES), lambda bi, i: (0, bi * per_b + i, 0)),
                  const((w4, w4)), const((1, w4)), const((D_MODEL, D_MODEL)), const((1, D_MODEL)),
                  const((1, D_MODEL)), const((D_MODEL, D_MODEL)),
                  pl.BlockSpec((MEM_LEN, 2 * D_MODEL), lambda bi, i: (bi, 0)),
                  const((D_MODEL, D_MODEL)), const((1, D_MODEL)), const((1, D_MODEL))],
        out_specs=tok(D_MODEL),
        out_shape=jax.ShapeDtypeStruct(x2.shape, F32),
        compiler_params=_cparams(2),
        name="mix_out_mem_attn",
    )(x2, ya, yb, yc, yd, wglu, bglu, wmix, gmix, bmix, wq, kv, wo, gmem, bmem)


FFN_CHUNK = 256


def _ffn_kernel(x_ref, wg_ref, wu_ref, wd_ref, g_ref, b_ref, out_ref, h_ref, *, parts):
    rows = x_ref.shape[0] // parts
    for p in range(parts):
        sl = slice(p * rows, (p + 1) * rows)
        xb = x_ref[sl, :].astype(BF16)
        for j in range(0, D_FF, FFN_CHUNK):
            gate = _dot(xb, wg_ref[:, j:j + FFN_CHUNK])
            up = _dot(xb, wu_ref[:, j:j + FFN_CHUNK])
            h_ref[sl, j:j + FFN_CHUNK] = (gate * _sigmoid(gate) * up).astype(BF16)
        y = _dot(h_ref[sl, :], wd_ref[...])
        out_ref[sl, :] = _layer_norm(DEEPNORM_ALPHA * x_ref[sl, :] + y, g_ref[...], b_ref[...])


def _ffn(x2, wg, wu, wd, g, b, tm=1024, parts=4):
    n = x2.shape[0]
    tok = pl.BlockSpec((tm, D_MODEL), lambda i: (i, 0))
    const = lambda shape: pl.BlockSpec(shape, lambda i: (0, 0), pipeline_mode=pl.Buffered(1))
    return pl.pallas_call(
        functools.partial(_ffn_kernel, parts=parts),
        grid=(n // tm,),
        in_specs=[tok, const((D_MODEL, D_FF)), const((D_MODEL, D_FF)), const((D_FF, D_MODEL)),
                  const((1, D_MODEL)), const((1, D_MODEL))],
        out_specs=tok,
        out_shape=jax.ShapeDtypeStruct(x2.shape, F32),
        scratch_shapes=[pltpu.VMEM((tm, D_FF), BF16)],
        compiler_params=_cparams(1),
        name="ffn",
    )(x2, wg, wu, wd, g, b)


def _permute_w_in(w):
    c = w.shape[0]
    return jnp.concatenate([
        w[:, 0:1536], w[:, 1792:2304],
        w[:, 2576:2832],
        w[:, 1536:1792], w[:, 2320:2576],
        w[:, 2304:2320], jnp.zeros((c, LANES - GLA_GATE_RANK), w.dtype),
    ], axis=1)


def _mixers(x2, bsz, seq, w_in_p, ret_gn_g, gla_w_gate, gla_b_gate, gla_gn_g, s5_a_re, s5_a_im,
            s5_log_dt, s5_b_re, s5_b_im, s5_c_re, s5_c_im, s5_d):
    pa, pb, uv, pf = _inproj(x2, w_in_p)
    ya = _dilated_attention(pa, bsz, seq)
    yb = _retention(pb, pf, ret_gn_g.reshape(1, -1), bsz, seq)
    kwid = HEADS * GLA_DK
    w_gate_p = jnp.concatenate(
        [gla_w_gate, jnp.zeros((kwid - GLA_GATE_RANK, kwid), gla_w_gate.dtype)], axis=0).astype(BF16)
    yc = _gla(pb, pf, w_gate_p, gla_b_gate.reshape(1, -1), gla_gn_g.reshape(1, -1), bsz, seq)
    yd = _s5(uv, s5_a_re, s5_a_im, s5_log_dt, s5_b_re, s5_b_im, s5_c_re, s5_c_im, s5_d, bsz)
    return ya, yb, yc, yd


def kernel(x, mem, w_in, ret_gn_g, gla_w_gate, gla_b_gate, gla_gn_g, s5_a_re, s5_a_im, s5_log_dt, s5_b_re,
           s5_b_im, s5_c_re, s5_c_im, s5_d, s5_w_glu, s5_b_glu, w_mix_out, ln_mix_g, ln_mix_b, w_mem_q,
           w_mem_kv, w_mem_o, ln_mem_g, ln_mem_b, w_ff_gate, w_ff_up, w_ff_down, ln_ff_g, ln_ff_b):
    bsz, seq, dm = x.shape
    assert dm == D_MODEL and mem.shape == (bsz, MEM_LEN, D_MODEL)
    assert seq % DIL_SPAN == 0
    x2 = x.reshape(bsz * seq, dm)
    mem2 = mem.reshape(bsz * MEM_LEN, dm)
    row = lambda t: t.reshape(1, -1)
    for l in range(DEPTH):
        ys = _mixers(x2, bsz, seq, _permute_w_in(w_in[l]).astype(BF16), ret_gn_g[l], gla_w_gate[l], gla_b_gate[l],
                     gla_gn_g[l], s5_a_re[l], s5_a_im[l], s5_log_dt[l], s5_b_re[l], s5_b_im[l],
                     s5_c_re[l], s5_c_im[l], s5_d[l])
        kv = _kvproj(mem2, _to_bf16(w_mem_kv, l))
        x2 = _mixmem(x2, *ys, _to_bf16(s5_w_glu, l), row(s5_b_glu[l]), _to_bf16(w_mix_out, l),
                     row(ln_mix_g[l]), row(ln_mix_b[l]), _to_bf16(w_mem_q, l), kv, _to_bf16(w_mem_o, l),
                     row(ln_mem_g[l]), row(ln_mem_b[l]), bsz, seq)
        x2 = _ffn(x2, _to_bf16(w_ff_gate, l), _to_bf16(w_ff_up, l), _to_bf16(w_ff_down, l),
                  row(ln_ff_g[l]), row(ln_ff_b[l]))
    return x2.reshape(bsz, seq, dm)
```
